```python
import math
import jax
import jax.numpy as jnp
from jax import lax
import numpy as np

D_MODEL = 1024
BATCH = 4
SEQ = 8192
DEPTH = 2

GRID_W = 64
CTX_LEN = 256
ROPE_THETA = 10000.0
Q_BLOCK = 128
LN_EPS = 1e-6
RMS_EPS = 1e-6

MLA_HEADS = 8
MLA_Q_LORA = 384
MLA_KV_LORA = 256
MLA_NOPE = 64
MLA_ROPE = 32
MLA_V = 64
MLA_WIDTH = MLA_HEADS * MLA_V
MLA_SCALE = (MLA_NOPE + MLA_ROPE) ** -0.5

DIFF_HEADS = 4
DIFF_QK = 64
DIFF_V = 2 * DIFF_QK
DIFF_WIDTH = DIFF_HEADS * DIFF_V
DIFF_SCALE = DIFF_QK ** -0.5

ATTN_WIDTH = MLA_WIDTH + DIFF_WIDTH
EVEN_SPLITS = (MLA_Q_LORA, MLA_KV_LORA, MLA_ROPE,
               DIFF_HEADS * 2 * DIFF_QK, DIFF_HEADS * 2 * DIFF_QK, DIFF_WIDTH,
               ATTN_WIDTH)
EVEN_IN_WIDTH = sum(EVEN_SPLITS)

LRU_WIDTH = 1024
LRU_BLOCKS = 8
LRU_BLOCK_W = LRU_WIDTH // LRU_BLOCKS
LRU_C = 8.0
CONV_W = 4
CONV_PAD_L = CONV_W // 2
CONV_PAD_R = CONV_W - 1 - CONV_PAD_L

DEEPNORM_ALPHA = (2 * DEPTH) ** 0.25
DEEPNORM_BETA = (8 * DEPTH) ** -0.25

kernel_name = 'hybrid_mla_diffattn_rglru_prefix_dit'


def _layer_norm(x, g, b):
    x32 = x.astype(jnp.float32)
    mu = jnp.mean(x32, -1, keepdims=True)
    var = jnp.mean(jnp.square(x32 - mu), -1, keepdims=True)
    y = (x32 - mu) * lax.rsqrt(var + LN_EPS) * g.astype(jnp.float32) + b.astype(jnp.float32)
    return y.astype(x.dtype)


def _rms_norm(x, g):
    x32 = x.astype(jnp.float32)
    y = x32 * lax.rsqrt(jnp.mean(jnp.square(x32), -1, keepdims=True) + RMS_EPS)
    return (y * g.astype(jnp.float32)).astype(x.dtype)


def _adaln(cond, w, b):
    m = jax.nn.silu(cond) @ w + b
    return jnp.split(m, 3, axis=-1)


def _modulate(x, shift, scale):
    return x * (1 + scale) + shift


def _rope_tables(n, rot_dim):
    t = jnp.arange(n)
    rows = (t // GRID_W).astype(jnp.float32)
    cols = (t % GRID_W).astype(jnp.float32)
    n_freq = rot_dim // 4
    freqs = ROPE_THETA ** (-jnp.arange(n_freq, dtype=jnp.float32) / n_freq)
    ang = jnp.concatenate([rows[:, None] * freqs, cols[:, None] * freqs], -1)
    return jnp.cos(ang), jnp.sin(ang)


def _apply_rope(x, cos, sin):
    shape = (x.shape[1],) + (1,) * (x.ndim - 3) + (cos.shape[-1],)
    cos = cos.reshape(shape).astype(x.dtype)
    sin = sin.reshape(shape).astype(x.dtype)
    x1, x2 = jnp.split(x, 2, axis=-1)
    return jnp.concatenate([x1 * cos - x2 * sin, x2 * cos + x1 * sin], -1)


def _sweep_query_blocks(fn, qs):
    b, n = qs[0].shape[:2]
    nb = n // Q_BLOCK
    blocks = tuple(q.reshape((b, nb, Q_BLOCK) + q.shape[2:]).swapaxes(0, 1) for q in qs)
    out = lax.map(lambda qb: fn(*qb), blocks)
    out = out.swapaxes(0, 1)
    return out.reshape((b, n) + out.shape[3:])


def _softmax_attn(q, k, v, scale):
    s = jnp.einsum('bqhd,bkhd->bhqk', q, k).astype(jnp.float32) * scale
    p = jax.nn.softmax(s, axis=-1).astype(v.dtype)
    return jnp.einsum('bhqk,bkhd->bqhd', p, v)


def _diff_attn(q1, q2, k1, k2, v, lam, scale):
    s1 = jnp.einsum('bqhd,bkhd->bhqk', q1, k1).astype(jnp.float32) * scale
    s2 = jnp.einsum('bqhd,bkhd->bhqk', q2, k2).astype(jnp.float32) * scale
    p = (jax.nn.softmax(s1, axis=-1) - lam * jax.nn.softmax(s2, axis=-1)).astype(v.dtype)
    return jnp.einsum('bhqk,bkhd->bqhd', p, v)


def _project_even(z, q_norm_g, w_uq, kv_norm_g, w_ukv):
    idx = np.cumsum(EVEN_SPLITS)[:-1].tolist()
    cq, ckv, k_rope, dq, dk, dv, gate = jnp.split(z, idx, axis=-1)
    b, n = z.shape[:2]
    q = (_rms_norm(cq, q_norm_g) @ w_uq).reshape(b, n, MLA_HEADS, MLA_NOPE + MLA_ROPE)
    kv = (_rms_norm(ckv, kv_norm_g) @ w_ukv).reshape(b, n, MLA_HEADS, MLA_NOPE + MLA_V)
    return (q, kv[..., :MLA_NOPE], k_rope, kv[..., MLA_NOPE:],
            dq.reshape(b, n, DIFF_HEADS, 2, DIFF_QK),
            dk.reshape(b, n, DIFF_HEADS, 2, DIFF_QK),
            dv.reshape(b, n, DIFF_HEADS, DIFF_V), gate)


def _mla_keys(k_nope, k_rope):
    k_r = jnp.broadcast_to(k_rope[:, :, None, :], k_nope.shape[:3] + (MLA_ROPE,))
    return jnp.concatenate([k_nope, k_r], -1)


def _even_merge(o_a, o_b, gate, subln_g, lambda_init, w_out):
    b, n = o_a.shape[:2]
    o_b = _rms_norm(o_b, subln_g) * (1.0 - lambda_init)
    o = jnp.concatenate([o_a.reshape(b, n, MLA_WIDTH), o_b.reshape(b, n, DIFF_WIDTH)], -1)
    return (o * jax.nn.silu(gate)) @ w_out


def _even_layer(x, ctx, c, c_ctx, ada_w, ada_b, ln_g, ln_b, w_in, q_norm_g, w_uq, kv_norm_g, w_ukv,
                lam_q1, lam_k1, lam_q2, lam_k2, subln_g, w_out, lambda_init, need_ctx):
    n = x.shape[1]
    sh_x, sc_x, g_x = _adaln(c[:, None, :], ada_w, ada_b)
    sh_c, sc_c, g_c = _adaln(c_ctx[None, None, :], ada_w, ada_b)
    qx, knx, krx, vax, dqx, dkx, dvx, gx = _project_even(_modulate(x, sh_x, sc_x) @ w_in,
                                                          q_norm_g, w_uq, kv_norm_g, w_ukv)
    qc, knc, krc, vac, dqc, dkc, dvc, gc = _project_even(_modulate(ctx, sh_c, sc_c) @ w_in,
                                                          q_norm_g, w_uq, kv_norm_g, w_ukv)
    cos_a, sin_a = _rope_tables(n, MLA_ROPE)
    cos_b, sin_b = _rope_tables(n, DIFF_QK)
    qx = jnp.concatenate([qx[..., :MLA_NOPE], _apply_rope(qx[..., MLA_NOPE:], cos_a, sin_a)], -1)
    krx = _apply_rope(krx, cos_a, sin_a)
    dqx = _apply_rope(dqx, cos_b, sin_b)
    dkx = _apply_rope(dkx, cos_b, sin_b)
    kac = _mla_keys(knc, krc)
    ka_all = jnp.concatenate([_mla_keys(knx, krx), kac], 1)
    va_all = jnp.concatenate([vax, vac], 1)
    dk1_all = jnp.concatenate([dkx[..., 0, :], dkc[..., 0, :]], 1)
    dk2_all = jnp.concatenate([dkx[..., 1, :], dkc[..., 1, :]], 1)
    dv_all = jnp.concatenate([dvx, dvc], 1)
    lam = (jnp.exp(jnp.sum(lam_q1.astype(jnp.float32) * lam_k1.astype(jnp.float32)))
           - jnp.exp(jnp.sum(lam_q2.astype(jnp.float32) * lam_k2.astype(jnp.float32)))
           + lambda_init)
    o_a = _sweep_query_blocks(lambda qb: _softmax_attn(qb, ka_all, va_all, MLA_SCALE), (qx,))
    o_b = _sweep_query_blocks(
        lambda q1b, q2b: _diff_attn(q1b, q2b, dk1_all, dk2_all, dv_all, lam, DIFF_SCALE),
        (dqx[..., 0, :], dqx[..., 1, :]))
    y = _even_merge(o_a, o_b, gx, subln_g, lambda_init, w_out)
    x_new = _layer_norm(DEEPNORM_ALPHA * x + g_x * y, ln_g, ln_b)
    if not need_ctx:
        return x_new, None
    o_ac = _softmax_attn(qc, kac, vac, MLA_SCALE)
    o_bc = _diff_attn(dqc[..., 0, :], dqc[..., 1, :], dkc[..., 0, :], dkc[..., 1, :], dvc, lam, DIFF_SCALE)
    yc = _even_merge(o_ac, o_bc, gc, subln_g, lambda_init, w_out)
    ctx_new = _layer_norm(DEEPNORM_ALPHA * ctx + g_c * yc, ln_g, ln_b)
    return x_new, ctx_new


def _conv_centred(u, w, b):
    out = lax.conv_general_dilated(u, w[:, None, :].astype(u.dtype), window_strides=(1,),
                                   padding=[(CONV_PAD_L, CONV_PAD_R)],
                                   dimension_numbers=('NWC', 'WIO', 'NWC'),
                                   feature_group_count=u.shape[-1])
    return out + b


def _rglru_coeffs(u, wa, ba, wx, bx, lam):
    b, n, _ = u.shape
    ub = u.reshape(b, n, LRU_BLOCKS, LRU_BLOCK_W)
    r = jax.nn.sigmoid(jnp.einsum('bnki,kij->bnkj', ub, wa) + ba).reshape(b, n, LRU_WIDTH).astype(jnp.float32)
    i = jax.nn.sigmoid(jnp.einsum('bnki,kij->bnkj', ub, wx) + bx).reshape(b, n, LRU_WIDTH).astype(jnp.float32)
    log_a = -LRU_C * r * jax.nn.softplus(-lam.astype(jnp.float32))
    a = jnp.exp(log_a)
    gated = jnp.sqrt(-jnp.expm1(2.0 * log_a)) * (i * u.astype(jnp.float32))
    return a, gated


def _linear_scan(a, bx, h0):
    bx = bx.at[:, 0].add(a[:, 0] * h0)
    def combine(l, r):
        return (l[0] * r[0], r[0] * l[1] + r[1])
    return lax.associative_scan(combine, (a, bx), axis=1)[1]


def _rglru_direction(ux, uc, wa, ba, wx, bx, lam):
    a_c, b_c = _rglru_coeffs(uc, wa, ba, wx, bx, lam)
    a_x, b_x = _rglru_coeffs(ux, wa, ba, wx, bx, lam)
    h_c = _linear_scan(a_c, b_c, jnp.zeros((uc.shape[0], LRU_WIDTH), jnp.float32))
    h_x = _linear_scan(a_x, b_x, h_c[:, -1])
    return h_x, h_c


def _odd_layer(x, ctx, c, c_ctx, ada_w, ada_b, ln_g, ln_b, w_in, conv_w, conv_b,
               ga_w, ga_b, gx_w, gx_b, lru_lambda, w_out, need_ctx):
    sh_x, sc_x, g_x = _adaln(c[:, None, :], ada_w, ada_b)
    sh_c, sc_c, g_c = _adaln(c_ctx[None, None, :], ada_w, ada_b)
    ux, gate_x = jnp.split(_modulate(x, sh_x, sc_x) @ w_in, 2, axis=-1)
    uc, gate_c = jnp.split(_modulate(ctx, sh_c, sc_c) @ w_in, 2, axis=-1)
    ux = _conv_centred(ux, conv_w, conv_b)
    uc = _conv_centred(uc, conv_w, conv_b)
    hx_f, hc_f = _rglru_direction(ux, uc, ga_w[0], ga_b[0], gx_w[0], gx_b[0], lru_lambda[0])
    hx_b, hc_b = _rglru_direction(ux[:, ::-1], uc[:, ::-1], ga_w[1], ga_b[1], gx_w[1], gx_b[1], lru_lambda[1])
    hx = (hx_f + hx_b[:, ::-1]).astype(x.dtype)
    y = (hx * jax.nn.silu(gate_x)) @ w_out
    x_new = _layer_norm(DEEPNORM_ALPHA * x + g_x * y, ln_g, ln_b)
    if not need_ctx:
        return x_new, None
    hc = (hc_f + hc_b[:, ::-1]).astype(ctx.dtype)
    yc = (hc * jax.nn.silu(gate_c)) @ w_out
    ctx_new = _layer_norm(DEEPNORM_ALPHA * ctx + g_c * yc, ln_g, ln_b)
    return x_new, ctx_new


def setup_inputs(seed: int = 0) -> dict:
    key = jax.random.key(seed)
    ks = jax.random.split(key, 32)
    n_even = (DEPTH + 1) // 2
    n_odd = DEPTH // 2

    def nrm(k, shape, scale):
        return jax.random.normal(k, shape, jnp.float32) * scale

    u = jax.random.uniform(ks[27], (n_odd, 2, LRU_WIDTH), jnp.float32, 0.9, 0.999)
    a0 = u ** (1.0 / LRU_C)
    return {
        'x': nrm(ks[0], (BATCH, SEQ, D_MODEL), 1.0),
        'c': nrm(ks[1], (BATCH, D_MODEL), 1.0),
        'ctx': nrm(ks[2], (BATCH, CTX_LEN, D_MODEL), 1.0),
        'c_ctx': nrm(ks[3], (D_MODEL,), 1.0),
        'ada_w': nrm(ks[4], (DEPTH, D_MODEL, 3 * D_MODEL), 0.5 * D_MODEL ** -0.5),
        'ada_b': nrm(ks[5], (DEPTH, 3 * D_MODEL), 0.02),
        'post_ln_g': 1.0 + nrm(ks[6], (DEPTH, D_MODEL), 0.02),
        'post_ln_b': nrm(ks[7], (DEPTH, D_MODEL), 0.02),
        'e_w_in': nrm(ks[8], (n_even, D_MODEL, EVEN_IN_WIDTH), D_MODEL ** -0.5),
        'e_q_norm_g': 1.0 + nrm(ks[9], (n_even, MLA_Q_LORA), 0.02),
        'e_w_uq': nrm(ks[10], (n_even, MLA_Q_LORA, MLA_HEADS * (MLA_NOPE + MLA_ROPE)), MLA_Q_LORA ** -0.5),
        'e_kv_norm_g': 1.0 + nrm(ks[11], (n_even, MLA_KV_LORA), 0.02),
        'e_w_ukv': nrm(ks[12], (n_even, MLA_KV_LORA, MLA_HEADS * (MLA_NOPE + MLA_V)), MLA_KV_LORA ** -0.5),
        'e_lam_q1': nrm(ks[13], (n_even, DIFF_QK), 0.1),
        'e_lam_k1': nrm(ks[14], (n_even, DIFF_QK), 0.1),
        'e_lam_q2': nrm(ks[15], (n_even, DIFF_QK), 0.1),
        'e_lam_k2': nrm(ks[16], (n_even, DIFF_QK), 0.1),
        'e_subln_g': 1.0 + nrm(ks[17], (n_even, DIFF_V), 0.02),
        'e_w_out': nrm(ks[18], (n_even, ATTN_WIDTH, D_MODEL), ATTN_WIDTH ** -0.5 * DEEPNORM_BETA),
        'o_w_in': nrm(ks[19], (n_odd, D_MODEL, 2 * LRU_WIDTH), D_MODEL ** -0.5),
        'o_conv_w': nrm(ks[20], (n_odd, CONV_W, LRU_WIDTH), CONV_W ** -0.5),
        'o_conv_b': nrm(ks[21], (n_odd, LRU_WIDTH), 0.02),
        'o_gate_a_w': nrm(ks[22], (n_odd, 2, LRU_BLOCKS, LRU_BLOCK_W, LRU_BLOCK_W), LRU_BLOCK_W ** -0.5),
        'o_gate_a_b': nrm(ks[23], (n_odd, 2, LRU_BLOCKS, LRU_BLOCK_W), 0.02),
        'o_gate_x_w': nrm(ks[24], (n_odd, 2, LRU_BLOCKS, LRU_BLOCK_W, LRU_BLOCK_W), LRU_BLOCK_W ** -0.5),
        'o_gate_x_b': nrm(ks[25], (n_odd, 2, LRU_BLOCKS, LRU_BLOCK_W), 0.02),
        'o_lru_lambda': jnp.log(a0) - jnp.log1p(-a0),
        'o_w_out': nrm(ks[26], (n_odd, LRU_WIDTH, D_MODEL), LRU_WIDTH ** -0.5 * DEEPNORM_BETA),
    }


def reference(x, c, ctx, c_ctx, ada_w, ada_b, post_ln_g, post_ln_b,
              e_w_in, e_q_norm_g, e_w_uq, e_kv_norm_g, e_w_ukv,
              e_lam_q1, e_lam_k1, e_lam_q2, e_lam_k2, e_subln_g, e_w_out,
              o_w_in, o_conv_w, o_conv_b, o_gate_a_w, o_gate_a_b, o_gate_x_w, o_gate_x_b,
              o_lru_lambda, o_w_out):
    for i in range(DEPTH):
        need_ctx = i < DEPTH - 1
        j = i // 2
        if i % 2 == 0:
            lambda_init = 0.8 - 0.6 * math.exp(-0.3 * i)
            x, ctx = _even_layer(x, ctx, c, c_ctx, ada_w[i], ada_b[i], post_ln_g[i], post_ln_b[i],
                                 e_w_in[j], e_q_norm_g[j], e_w_uq[j], e_kv_norm_g[j], e_w_ukv[j],
                                 e_lam_q1[j], e_lam_k1[j], e_lam_q2[j], e_lam_k2[j], e_subln_g[j],
                                 e_w_out[j], lambda_init, need_ctx)
        else:
            x, ctx = _odd_layer(x, ctx, c, c_ctx, ada_w[i], ada_b[i], post_ln_g[i], post_ln_b[i],
                                o_w_in[j], o_conv_w[j], o_conv_b[j], o_gate_a_w[j], o_gate_a_b[j],
                                o_gate_x_w[j], o_gate_x_b[j], o_lru_lambda[j], o_w_out[j], need_ctx)
    return x
```

```python
import functools
import math

import jax
import jax.numpy as jnp
from jax import lax
from jax.experimental import pallas as pl
from jax.experimental.pallas import tpu as pltpu

GRID_W = 64
ROPE_THETA = 10000.0
LN_EPS = 1e-6
RMS_EPS = 1e-6

MLA_HEADS = 8
MLA_Q_LORA = 384
MLA_KV_LORA = 256
MLA_NOPE = 64
MLA_ROPE = 32
MLA_V = 64
MLA_QK = MLA_NOPE + MLA_ROPE
MLA_SCALE = MLA_QK ** -0.5

DIFF_HEADS = 4
DIFF_QK = 64
DIFF_V = 2 * DIFF_QK
DIFF_SCALE = DIFF_QK ** -0.5

LRU_BLOCKS = 8
LRU_C = 8.0
CONV_W = 4

DEPTH = 2
DEEPNORM_ALPHA = (2 * DEPTH) ** 0.25
LAMBDA_INIT_0 = 0.8 - 0.6 * math.exp(-0.3 * 0)

LANE = 128
SUBLANE = 8
LOG2E = 1.4426950408889634
NEG_BIG = -1e30

TOKEN_TILE = 256
Q_TILE = 512
KV_CHUNK = 768
SCAN_TILE = 512
VMEM_LIMIT = 56 * 1024 * 1024

_G_CQ = (0, 384)
_G_CKV = (384, 640)
_G_KR = (640, 768)
_G_DQ = (768, 1280)
_G_DK = (1280, 1792)
_G_DV = (1792, 2304)
_G_GATE = (2304, 3328)
_EVEN_W = 3328


def _cparams(*sem):
    return pltpu.CompilerParams(dimension_semantics=sem, vmem_limit_bytes=VMEM_LIMIT)


def _bf16_dot(a, b):
    return jnp.dot(a.astype(jnp.bfloat16), b.astype(jnp.bfloat16), preferred_element_type=jnp.float32)


def _sigmoid(x):
    return 1.0 / (1.0 + jnp.exp(-x))


def _adaln_kernel(cond_ref, w_ref, b_ref, lamv_ref, mod_ref, lam_ref):
    cond = cond_ref[...]
    h = cond * _sigmoid(cond)
    mod_ref[0] = jnp.dot(h, w_ref[0], preferred_element_type=jnp.float32,
                         precision=lax.Precision.HIGHEST) + b_ref[0]
    lv = lamv_ref[...]
    d1 = jnp.sum(lv[0:1] * lv[1:2], axis=-1, keepdims=True)
    d2 = jnp.sum(lv[2:3] * lv[3:4], axis=-1, keepdims=True)
    lam = jnp.exp(d1) - jnp.exp(d2) + LAMBDA_INIT_0
    lam_ref[...] = jnp.broadcast_to(lam, lam_ref.shape)


def _adaln(cond, ada_w, ada_b, lamv):
    depth, d, d3 = ada_w.shape
    nj = d3 // d
    return pl.pallas_call(
        _adaln_kernel,
        grid=(depth, nj),
        in_specs=[
            pl.BlockSpec((SUBLANE, d), lambda l, j: (0, 0)),
            pl.BlockSpec((1, d, d), lambda l, j: (l, 0, j)),
            pl.BlockSpec((1, 1, d), lambda l, j: (l, 0, j)),
            pl.BlockSpec(lamv.shape, lambda l, j: (0, 0)),
        ],
        out_specs=[
            pl.BlockSpec((1, SUBLANE, d), lambda l, j: (l, 0, j)),
            pl.BlockSpec((SUBLANE, LANE), lambda l, j: (0, 0)),
        ],
        out_shape=[
            jax.ShapeDtypeStruct((depth, SUBLANE, d3), jnp.float32),
            jax.ShapeDtypeStruct((SUBLANE, LANE), jnp.float32),
        ],
        compiler_params=_cparams("arbitrary", "arbitrary"),
        name="adaln",
    )(cond, ada_w, ada_b.reshape(depth, 1, d3), lamv)


def _rms(x, g):
    return x * lax.rsqrt(jnp.mean(x * x, axis=-1, keepdims=True) + RMS_EPS) * g


def _rope_group(x, cos, sin, half, first_half_mask):
    partner = jnp.where(first_half_mask, pltpu.roll(x, LANE - half, 1), pltpu.roll(x, half, 1))
    return x * cos + partner * sin


def _even_proj_kernel(nx, x_ref, c_ref, mod_ref, w1_ref, qg_ref, wuq_ref, kvg_ref, wkv_ref,
                      ca_ref, sa_ref, cb_ref, sb_ref,
                      qt_ref, ka_ref, vat_ref, dqt_ref, dk_ref, dvt_ref, gate_ref):
    i = pl.program_id(1)
    xin = jnp.where(i >= nx, c_ref[0], x_ref[0])
    mod = mod_ref[0, 0]
    xm = xin * (1.0 + mod[1:2]) + mod[0:1]
    z = _bf16_dot(xm, w1_ref[...])

    lane = lax.broadcasted_iota(jnp.int32, (1, LANE), 1)
    mla_first = jnp.logical_and(lane >= MLA_NOPE, lane < MLA_NOPE + MLA_ROPE // 2)
    diff_first = (lane % DIFF_QK) < DIFF_QK // 2
    ca, sa, cb, sb = ca_ref[...], sa_ref[...], cb_ref[...], sb_ref[...]

    cqn = _rms(z[:, _G_CQ[0]:_G_CQ[1]], qg_ref[...])
    q = _bf16_dot(cqn, wuq_ref[...])
    q_heads = []
    for h in range(MLA_HEADS):
        qh = _rope_group(q[:, h * LANE:(h + 1) * LANE], ca, sa, MLA_ROPE // 2, mla_first)
        q_heads.append(qh * (MLA_SCALE * LOG2E))
    qt_ref[0] = jnp.concatenate(q_heads, axis=1).T.astype(qt_ref.dtype)

    ckvn = _rms(z[:, _G_CKV[0]:_G_CKV[1]], kvg_ref[...])
    kv = _bf16_dot(ckvn, wkv_ref[...])
    kr = _rope_group(z[:, _G_KR[0]:_G_KR[1]], ca, sa, MLA_ROPE // 2, mla_first)
    for h in range(MLA_HEADS):
        ka_ref[0, :, h * LANE:(h + 1) * LANE] = (kv[:, h * LANE:(h + 1) * LANE] + kr).astype(ka_ref.dtype)
    vat_ref[0] = kv[:, MLA_HEADS * LANE:].T.astype(vat_ref.dtype)

    dq_heads = []
    for h in range(DIFF_HEADS):
        s0 = _G_DQ[0] + h * LANE
        dq_heads.append(_rope_group(z[:, s0:s0 + LANE], cb, sb, DIFF_QK // 2, diff_first) * (DIFF_SCALE * LOG2E))
        s0 = _G_DK[0] + h * LANE
        dk_ref[0, :, h * LANE:(h + 1) * LANE] = _rope_group(
            z[:, s0:s0 + LANE], cb, sb, DIFF_QK // 2, diff_first).astype(dk_ref.dtype)
    dqt_ref[0] = jnp.concatenate(dq_heads, axis=1).T.astype(dqt_ref.dtype)
    dvt_ref[0] = z[:, _G_DV[0]:_G_DV[1]].T.astype(dvt_ref.dtype)
    gate_ref[0] = z[:, _G_GATE[0]:_G_GATE[1]]


def _even_proj(x, ctx, modsel, w1, qg, wuq, kvg, wkv, tabs):
    b, n, d = x.shape
    c = ctx.shape[1]
    t = TOKEN_TILE
    nx, nc = n // t, c // t
    nt = n + c
    full = lambda a: pl.BlockSpec(a.shape, lambda bi, i: (0,) * a.ndim)
    tab_spec = pl.BlockSpec((t, LANE), lambda bi, i: (i, 0))
    tok = lambda w: pl.BlockSpec((1, t, w), lambda bi, i: (bi, i, 0))
    tok_t = lambda w: pl.BlockSpec((1, w, t), lambda bi, i: (bi, 0, i))
    bf = jnp.bfloat16
    return pl.pallas_call(
        functools.partial(_even_proj_kernel, nx),
        grid=(b, nx + nc),
        in_specs=[
            pl.BlockSpec((1, t, d), lambda bi, i: (bi, jnp.minimum(i, nx - 1), 0)),
            pl.BlockSpec((1, t, d), lambda bi, i: (bi, jnp.maximum(i - nx, 0), 0)),
            pl.BlockSpec((1, 1, 3, d), lambda bi, i: (bi, jnp.where(i >= nx, 1, 0), 0, 0)),
            full(w1), full(qg), full(wuq), full(kvg), full(wkv),
            tab_spec, tab_spec, tab_spec, tab_spec,
        ],
        out_specs=[
            tok_t(MLA_HEADS * LANE), tok(MLA_HEADS * LANE), tok_t(MLA_HEADS * MLA_V),
            tok_t(DIFF_HEADS * LANE), tok(DIFF_HEADS * LANE), tok_t(DIFF_HEADS * DIFF_V),
            tok(MLA_HEADS * MLA_V + DIFF_HEADS * DIFF_V),
        ],
        out_shape=[
            jax.ShapeDtypeStruct((b, MLA_HEADS * LANE, nt), bf),
            jax.ShapeDtypeStruct((b, nt, MLA_HEADS * LANE), bf),
            jax.ShapeDtypeStruct((b, MLA_HEADS * MLA_V, nt), bf),
            jax.ShapeDtypeStruct((b, DIFF_HEADS * LANE, nt), bf),
            jax.ShapeDtypeStruct((b, nt, DIFF_HEADS * LANE), bf),
            jax.ShapeDtypeStruct((b, DIFF_HEADS * DIFF_V, nt), bf),
            jax.ShapeDtypeStruct((b, nt, MLA_HEADS * MLA_V + DIFF_HEADS * DIFF_V), jnp.float32),
        ],
        compiler_params=_cparams("parallel", "parallel"),
        name="even_proj",
    )(x, ctx, modsel, w1, qg, wuq, kvg, wkv, *tabs)


def _softmax_chunk(k, qt, vt, m, l, acc):
    s = jnp.dot(k, qt, preferred_element_type=jnp.float32)
    m_new = jnp.maximum(m, jnp.max(s, axis=0, keepdims=True))
    alpha = jnp.exp2(m - m_new)
    p = jnp.exp2(s - m_new)
    l = alpha * l + jnp.sum(p, axis=0, keepdims=True)
    acc = alpha * acc + jnp.dot(vt, p.astype(vt.dtype), preferred_element_type=jnp.float32)
    return m_new, l, acc


def _mla_attn_kernel(n_chunks, tk, qt_ref, k_ref, vt_ref, o_ref):
    qt = qt_ref[0]
    tq = qt.shape[1]

    def body(j, carry):
        off = pl.multiple_of(j * tk, LANE)
        k = k_ref[0, pl.ds(off, tk), :]
        vt = vt_ref[0, :, pl.ds(off, tk)]
        return _softmax_chunk(k, qt, vt, *carry)

    init = (jnp.full((1, tq), NEG_BIG, jnp.float32), jnp.zeros((1, tq), jnp.float32),
            jnp.zeros((MLA_V, tq), jnp.float32))
    m, l, acc = lax.fori_loop(0, n_chunks, body, init)
    o_ref[0] = acc / l


def _mla_attn(qt, ka, vat, nq, q_off):
    b, _, nk = vat.shape
    tq = min(Q_TILE, nq)
    tk = min(KV_CHUNK, nk)
    qo = q_off // tq
    return pl.pallas_call(
        functools.partial(_mla_attn_kernel, nk // tk, tk),
        grid=(b, MLA_HEADS, nq // tq),
        in_specs=[
            pl.BlockSpec((1, LANE, tq), lambda bi, h, i: (bi, h, i + qo)),
            pl.BlockSpec((1, nk, LANE), lambda bi, h, i: (bi, 0, h)),
            pl.BlockSpec((1, MLA_V, nk), lambda bi, h, i: (bi, h, 0)),
        ],
        out_specs=pl.BlockSpec((1, MLA_V, tq), lambda bi, h, i: (bi, h, i)),
        out_shape=jax.ShapeDtypeStruct((b, MLA_HEADS * MLA_V, nq), jnp.float32),
        compiler_params=_cparams("parallel", "parallel", "parallel"),
        name="mla_attn",
    )(qt, ka, vat)


def _diff_attn_kernel(n_chunks, tk, qt_ref, k_ref, vt_ref, lam_ref, g_ref, o_ref):
    qt = qt_ref[0]
    tq = qt.shape[1]
    row = lax.broadcasted_iota(jnp.int32, (LANE, 1), 0)
    zero = jnp.zeros_like(qt)
    q1 = jnp.where(row < DIFF_QK, qt, zero)
    q2 = jnp.where(row >= DIFF_QK, qt, zero)

    def body(j, carry):
        off = pl.multiple_of(j * tk, LANE)
        k = k_ref[0, pl.ds(off, tk), :]
        vt = vt_ref[0, :, pl.ds(off, tk)]
        c1 = _softmax_chunk(k, q1, vt, *carry[:3])
        c2 = _softmax_chunk(k, q2, vt, *carry[3:])
        return c1 + c2

    one = (jnp.full((1, tq), NEG_BIG, jnp.float32), jnp.zeros((1, tq), jnp.float32),
           jnp.zeros((DIFF_V, tq), jnp.float32))
    _, l1, a1, _, l2, a2 = lax.fori_loop(0, n_chunks, body, one + one)
    lam = lam_ref[0:1, 0:1]
    o = a1 / l1 - lam * (a2 / l2)
    o = o * lax.rsqrt(jnp.mean(o * o, axis=0, keepdims=True) + RMS_EPS) * g_ref[...]
    o_ref[0] = o * (1.0 - LAMBDA_INIT_0)


def _diff_attn(dqt, dk, dvt, lam, subln_col, nq, q_off):
    b, _, nk = dvt.shape
    tq = min(Q_TILE, nq)
    tk = min(KV_CHUNK, nk)
    qo = q_off // tq
    return pl.pallas_call(
        functools.partial(_diff_attn_kernel, nk // tk, tk),
        grid=(b, DIFF_HEADS, nq // tq),
        in_specs=[
            pl.BlockSpec((1, LANE, tq), lambda bi, h, i: (bi, h, i + qo)),
            pl.BlockSpec((1, nk, LANE), lambda bi, h, i: (bi, 0, h)),
            pl.BlockSpec((1, DIFF_V, nk), lambda bi, h, i: (bi, h, 0)),
            pl.BlockSpec(lam.shape, lambda bi, h, i: (0, 0)),
            pl.BlockSpec(subln_col.shape, lambda bi, h, i: (0, 0)),
        ],
        out_specs=pl.BlockSpec((1, DIFF_V, tq), lambda bi, h, i: (bi, h, i)),
        out_shape=jax.ShapeDtypeStruct((b, DIFF_HEADS * DIFF_V, nq), jnp.float32),
        compiler_params=_cparams("parallel", "parallel", "parallel"),
        name="diff_attn",
    )(dqt, dk, dvt, lam, subln_col)


def _residual_ln(xin, y, gate_vec, ln_g, ln_b):
    r = DEEPNORM_ALPHA * xin + gate_vec * y
    mu = jnp.mean(r, axis=-1, keepdims=True)
    rc = r - mu
    var = jnp.mean(rc * rc, axis=-1, keepdims=True)
    return rc * lax.rsqrt(var + LN_EPS) * ln_g + ln_b


def _even_merge_kernel(nx, x_ref, c_ref, mod_ref, oax_ref, oac_ref, obx_ref, obc_ref, gate_ref,
                       wout_ref, lng_ref, lnb_ref, out_ref):
    i = pl.program_id(1)
    is_ctx = i >= nx
    xin = jnp.where(is_ctx, c_ref[0], x_ref[0])
    oa = jnp.where(is_ctx, oac_ref[0], oax_ref[0]).T
    ob = jnp.where(is_ctx, obc_ref[0], obx_ref[0]).T
    o = jnp.concatenate([oa, ob], axis=1)
    g = gate_ref[0]
    y = _bf16_dot(o * (g * _sigmoid(g)), wout_ref[...])
    out_ref[0] = _residual_ln(xin, y, mod_ref[0, 0][2:3], lng_ref[...], lnb_ref[...])


def _even_merge(x, ctx, modsel, oax, oac, obx, obc, gate, wout, ln_g, ln_b):
    b, n, d = x.shape
    c = ctx.shape[1]
    t = TOKEN_TILE
    nx, nc = n // t, c // t
    wa, wb = oax.shape[1], obx.shape[1]
    full = lambda a: pl.BlockSpec(a.shape, lambda bi, i: (0,) * a.ndim)
    xi = lambda bi, i: (bi, jnp.minimum(i, nx - 1), 0)
    ci = lambda bi, i: (bi, jnp.maximum(i - nx, 0), 0)
    xit = lambda bi, i: (bi, 0, jnp.minimum(i, nx - 1))
    cit = lambda bi, i: (bi, 0, jnp.maximum(i - nx, 0))
    return pl.pallas_call(
        functools.partial(_even_merge_kernel, nx),
        grid=(b, nx + nc),
        in_specs=[
            pl.BlockSpec((1, t, d), xi),
            pl.BlockSpec((1, t, d), ci),
            pl.BlockSpec((1, 1, 3, d), lambda bi, i: (bi, jnp.where(i >= nx, 1, 0), 0, 0)),
            pl.BlockSpec((1, wa, t), xit), pl.BlockSpec((1, wa, t), cit),
            pl.BlockSpec((1, wb, t), xit), pl.BlockSpec((1, wb, t), cit),
            pl.BlockSpec((1, t, wa + wb), lambda bi, i: (bi, i, 0)),
            full(wout), full(ln_g), full(ln_b),
        ],
        out_specs=pl.BlockSpec((1, t, d), lambda bi, i: (bi, i, 0)),
        out_shape=jax.ShapeDtypeStruct((b, n + c, d), jnp.float32),
        compiler_params=_cparams("parallel", "parallel"),
        name="even_merge",
    )(x, ctx, modsel, oax, oac, obx, obc, gate, wout, ln_g, ln_b)


def _odd_proj_kernel(xc_ref, mod_ref, w_ref, u_ref, gate_ref):
    mod = mod_ref[0, 0]
    xm = xc_ref[0] * (1.0 + mod[1:2]) + mod[0:1]
    z = _bf16_dot(xm, w_ref[...])
    w = u_ref.shape[2]
    u_ref[0] = z[:, :w]
    gate_ref[0] = z[:, w:]


def _odd_proj(xc, modsel, w_in, nx):
    b, nt, d = xc.shape
    t = TOKEN_TILE
    w = w_in.shape[1] // 2
    return pl.pallas_call(
        _odd_proj_kernel,
        grid=(b, nt // t),
        in_specs=[
            pl.BlockSpec((1, t, d), lambda bi, i: (bi, i, 0)),
            pl.BlockSpec((1, 1, 3, d), lambda bi, i: (bi, jnp.where(i >= nx, 1, 0), 0, 0)),
            pl.BlockSpec(w_in.shape, lambda bi, i: (0, 0)),
        ],
        out_specs=[pl.BlockSpec((1, t, w), lambda bi, i: (bi, i, 0))] * 2,
        out_shape=[jax.ShapeDtypeStruct((b, nt, w), jnp.float32)] * 2,
        compiler_params=_cparams("parallel", "parallel"),
        name="odd_proj",
    )(xc, modsel, w_in)


def _conv_tile(u, prev, nxt, cw, cb):
    t = u.shape[0]
    row = lax.broadcasted_iota(jnp.int32, (t, 1), 0)
    m1 = jnp.where(row == 0, prev[7:8], pltpu.roll(u, 1, 0))
    m2 = jnp.where(row == 0, prev[6:7], jnp.where(row == 1, prev[7:8], pltpu.roll(u, 2, 0)))
    p1 = jnp.where(row == t - 1, nxt[0:1], pltpu.roll(u, t - 1, 0))
    return cw[0:1] * m2 + cw[1:2] * m1 + cw[2:3] * u + cw[3:4] * p1 + cb


def _lru_coeffs(uc, wa_ref, ba, wx_ref, bx, lam, a_ref, b_ref):
    sp = jnp.log1p(jnp.exp(-lam))
    for k in range(LRU_BLOCKS):
        sl = slice(k * LANE, (k + 1) * LANE)
        ub = uc[:, sl]
        r = _sigmoid(_bf16_dot(ub, wa_ref[k]) + ba[:, sl])
        ig = _sigmoid(_bf16_dot(ub, wx_ref[k]) + bx[:, sl])
        log_a = (-LRU_C) * r * sp[:, sl]
        a = jnp.exp(log_a)
        a_ref[:, sl] = a
        b_ref[:, sl] = jnp.sqrt(-jnp.tanh(log_a) * (a * a + 1.0)) * (ig * ub)


def _scan_rows(a_ref, b_ref, h_out_ref, h0, t, reverse):
    def step(s, h):
        r = (t - 1 - s) if reverse else s
        h = a_ref[pl.ds(r, 1), :] * h + b_ref[pl.ds(r, 1), :]
        h_out_ref[pl.ds(r, 1), :] = h
        return h

    return lax.fori_loop(0, t, step, h0, unroll=8)


def _lru_ctx_kernel(u_ref, cw_ref, cb_ref, wa_ref, ba_ref, wx_ref, bx_ref, lam_ref, h_ref,
                    a_s, b_s, hs):
    u = u_ref[0]
    zero8 = jnp.zeros((SUBLANE, u.shape[1]), jnp.float32)
    uc = _conv_tile(u, zero8, zero8, cw_ref[...], cb_ref[...])
    t = u.shape[0]
    for d in range(2):
        _lru_coeffs(uc, wa_ref.at[d], ba_ref[d], wx_ref.at[d], bx_ref[d], lam_ref[d], a_s, b_s)
        h = _scan_rows(a_s, b_s, hs, jnp.zeros((1, u.shape[1]), jnp.float32), t, reverse=(d == 1))
        h_ref[0, d:d + 1, :] = h


def _lru_ctx(u_all, n, c, cw, cb, wa, ba, wx, bx, lam):
    b, _, w = u_all.shape
    full = lambda a: pl.BlockSpec(a.shape, lambda bi: (0,) * a.ndim)
    return pl.pallas_call(
        _lru_ctx_kernel,
        grid=(b,),
        in_specs=[pl.BlockSpec((1, c, w), lambda bi: (bi, n // c, 0)),
                  full(cw), full(cb), full(wa), full(ba), full(wx), full(bx), full(lam)],
        out_specs=pl.BlockSpec((1, 2, w), lambda bi: (bi, 0, 0)),
        out_shape=jax.ShapeDtypeStruct((b, 2, w), jnp.float32),
        scratch_shapes=[pltpu.VMEM((c, w), jnp.float32)] * 3,
        compiler_params=_cparams("parallel"),
        name="lru_ctx",
    )(u_all, cw, cb, wa, ba, wx, bx, lam)


def _lru_dir_kernel(direction, n_tiles, *refs):
    reverse = direction == 1
    if reverse:
        (u_ref, up_ref, un_ref, h0_ref, cw_ref, cb_ref, wa_ref, ba_ref, wx_ref, bx_ref, lam_ref,
         hf_ref, gate_ref, x_ref, mod_ref, wout_ref, lng_ref, lnb_ref, out_ref, a_s, b_s, hs, carry) = refs
    else:
        (u_ref, up_ref, un_ref, h0_ref, cw_ref, cb_ref, wa_ref, ba_ref, wx_ref, bx_ref, lam_ref,
         out_ref, a_s, b_s, carry) = refs
        hs = out_ref.at[0]
    step = pl.program_id(1)
    tile = (n_tiles - 1 - step) if reverse else step

    @pl.when(step == 0)
    def _():
        carry[...] = h0_ref[0, direction:direction + 1, :]

    u = u_ref[0]
    t = u.shape[0]
    prev = jnp.where(tile == 0, 0.0, up_ref[0])
    nxt = jnp.where(tile == n_tiles - 1, 0.0, un_ref[0])
    uc = _conv_tile(u, prev, nxt, cw_ref[...], cb_ref[...])
    _lru_coeffs(uc, wa_ref.at[direction], ba_ref[direction], wx_ref.at[direction], bx_ref[direction],
                lam_ref[direction], a_s, b_s)
    carry[...] = _scan_rows(a_s, b_s, hs, carry[...], t, reverse)

    if reverse:
        g = gate_ref[0]
        hx = hf_ref[0] + hs[...]
        y = _bf16_dot(hx * (g * _sigmoid(g)), wout_ref[...])
        out_ref[0] = _residual_ln(x_ref[0], y, mod_ref[0, 0][2:3], lng_ref[...], lnb_ref[...])


def _lru_dir(direction, u_all, n, h0, cw, cb, wa, ba, wx, bx, lam, extra=None):
    b, _, w = u_all.shape
    t = min(SCAN_TILE, n)
    n_tiles = n // t
    r = t // SUBLANE
    if direction == 1:
        tidx = lambda s: n_tiles - 1 - s
    else:
        tidx = lambda s: s
    full = lambda a: pl.BlockSpec(a.shape, lambda bi, s: (0,) * a.ndim)
    tile_spec = lambda width: pl.BlockSpec((1, t, width), lambda bi, s: (bi, tidx(s), 0))
    in_specs = [
        tile_spec(w),
        pl.BlockSpec((1, SUBLANE, w), lambda bi, s: (bi, jnp.maximum(tidx(s) * r - 1, 0), 0)),
        pl.BlockSpec((1, SUBLANE, w), lambda bi, s: (bi, (tidx(s) + 1) * r, 0)),
        pl.BlockSpec((1, 2, w), lambda bi, s: (bi, 0, 0)),
        full(cw), full(cb), full(wa), full(ba), full(wx), full(bx), full(lam),
    ]
    args = [u_all, u_all, u_all, h0, cw, cb, wa, ba, wx, bx, lam]
    scratch = [pltpu.VMEM((t, w), jnp.float32), pltpu.VMEM((t, w), jnp.float32)]
    if direction == 1:
        hf, gate, xc, modsel, wout, ln_g, ln_b = extra
        d = xc.shape[2]
        in_specs += [tile_spec(w), tile_spec(w), tile_spec(d),
                     pl.BlockSpec((1, 1, 3, d), lambda bi, s: (bi, 0, 0, 0)),
                     full(wout), full(ln_g), full(ln_b)]
        args += [hf, gate, xc, modsel, wout, ln_g, ln_b]
        scratch.append(pltpu.VMEM((t, w), jnp.float32))
        out_w = d
    else:
        out_w = w
    scratch.append(pltpu.VMEM((1, w), jnp.float32))
    return pl.pallas_call(
        functools.partial(_lru_dir_kernel, direction, n_tiles),
        grid=(b, n_tiles),
        in_specs=in_specs,
        out_specs=pl.BlockSpec((1, t, out_w), lambda bi, s: (bi, tidx(s), 0)),
        out_shape=jax.ShapeDtypeStruct((b, n, out_w), jnp.float32),
        scratch_shapes=scratch,
        compiler_params=_cparams("parallel", "arbitrary"),
        name="lru_bwd_out" if direction == 1 else "lru_fwd",
    )(*args)


def _rope_tables(n, c):
    t = jnp.arange(n)
    rows = (t // GRID_W).astype(jnp.float32)
    cols = (t % GRID_W).astype(jnp.float32)

    def ang(rot_dim):
        n_freq = rot_dim // 4
        freqs = ROPE_THETA ** (-jnp.arange(n_freq, dtype=jnp.float32) / n_freq)
        return jnp.concatenate([rows[:, None] * freqs, cols[:, None] * freqs], -1)

    aa, ab = ang(MLA_ROPE), ang(DIFF_QK)
    one = jnp.ones((n, MLA_NOPE), jnp.float32)
    zero = jnp.zeros((n, MLA_NOPE), jnp.float32)
    pad1 = jnp.ones((n, LANE - MLA_QK), jnp.float32)
    pad0 = jnp.zeros((n, LANE - MLA_QK), jnp.float32)
    ca = jnp.concatenate([one, jnp.cos(aa), jnp.cos(aa), pad1], 1)
    sa = jnp.concatenate([zero, -jnp.sin(aa), jnp.sin(aa), pad0], 1)
    cb = jnp.concatenate([jnp.cos(ab), jnp.cos(ab)] * 2, 1)
    sb = jnp.concatenate([-jnp.sin(ab), jnp.sin(ab)] * 2, 1)
    ident = lambda tab, v: jnp.concatenate([tab, jnp.full((c, LANE), v, jnp.float32)], 0)
    return ident(ca, 1.0), ident(sa, 0.0), ident(cb, 1.0), ident(sb, 0.0)


def _even_weights(w_in, w_uq, w_ukv):
    d = w_in.shape[0]
    bf = jnp.bfloat16
    o = [0, 384, 640, 672, 1184, 1696, 2208, 3232]
    cq, ckv, kr, dq, dk, dv, gate = (w_in[:, o[i]:o[i + 1]] for i in range(7))
    kr_group = jnp.concatenate([jnp.zeros((d, MLA_NOPE), w_in.dtype), kr,
                                jnp.zeros((d, LANE - MLA_QK), w_in.dtype)], 1)
    w1 = jnp.concatenate([cq, ckv, kr_group, dq, dk, dv, gate], 1).astype(bf)
    wuq = jnp.pad(w_uq.reshape(MLA_Q_LORA, MLA_HEADS, MLA_QK),
                  ((0, 0), (0, 0), (0, LANE - MLA_QK))).reshape(MLA_Q_LORA, MLA_HEADS * LANE).astype(bf)
    ukv = w_ukv.reshape(MLA_KV_LORA, MLA_HEADS, MLA_NOPE + MLA_V)
    wuk = jnp.pad(ukv[..., :MLA_NOPE], ((0, 0), (0, 0), (0, LANE - MLA_NOPE))).reshape(MLA_KV_LORA, MLA_HEADS * LANE)
    wuv = ukv[..., MLA_NOPE:].reshape(MLA_KV_LORA, MLA_HEADS * MLA_V)
    wkv = jnp.concatenate([wuk, wuv], 1).astype(bf)
    return w1, wuq, wkv


def _mod_select(mods_l, b):
    d = mods_l.shape[1] // 3
    mx = mods_l[:b].reshape(b, 3, d)
    mc = jnp.broadcast_to(mods_l[b].reshape(1, 3, d), (b, 3, d))
    return jnp.stack([mx, mc], axis=1)


def kernel(x, c, ctx, c_ctx, ada_w, ada_b, post_ln_g, post_ln_b, e_w_in, e_q_norm_g, e_w_uq, e_kv_norm_g, e_w_ukv, e_lam_q1, e_lam_k1, e_lam_q2, e_lam_k2, e_subln_g, e_w_out, o_w_in, o_conv_w, o_conv_b, o_gate_a_w, o_gate_a_b, o_gate_x_w, o_gate_x_b, o_lru_lambda, o_w_out):
    b, n, d = x.shape
    cl = ctx.shape[1]
    assert ada_w.shape[0] == DEPTH and b < SUBLANE
    assert n % Q_TILE == 0 and cl % TOKEN_TILE == 0 and n % cl == 0 and (n + cl) % KV_CHUNK == 0
    bf = jnp.bfloat16

    cond = jnp.concatenate([c, c_ctx[None], jnp.zeros((SUBLANE - b - 1, d), c.dtype)], 0)
    lamv = jnp.concatenate([e_lam_q1, e_lam_k1, e_lam_q2, e_lam_k2], 0)
    mods, lam = _adaln(cond, ada_w, ada_b, lamv)
    mod0, mod1 = _mod_select(mods[0], b), _mod_select(mods[1], b)

    w1, wuq, wkv = _even_weights(e_w_in[0], e_w_uq[0], e_w_ukv[0])
    tabs = _rope_tables(n, cl)
    qt, ka, vat, dqt, dk, dvt, gate = _even_proj(
        x, ctx, mod0, w1, e_q_norm_g[0][None], wuq, e_kv_norm_g[0][None], wkv, tabs)
    subln_col = e_subln_g[0][:, None]
    oax = _mla_attn(qt, ka, vat, n, 0)
    obx = _diff_attn(dqt, dk, dvt, lam, subln_col, n, 0)
    oac = _mla_attn(qt[:, :, n:], ka[:, n:], vat[:, :, n:], cl, 0)
    obc = _diff_attn(dqt[:, :, n:], dk[:, n:], dvt[:, :, n:], lam, subln_col, cl, 0)
    xc = _even_merge(x, ctx, mod0, oax, oac, obx, obc, gate, e_w_out[0].astype(bf),
                     post_ln_g[0][None], post_ln_b[0][None])

    u_all, gate1 = _odd_proj(xc, mod1, o_w_in[0].astype(bf), n // TOKEN_TILE)
    cw, cb = o_conv_w[0], o_conv_b[0][None]
    wa, wx = o_gate_a_w[0].astype(bf), o_gate_x_w[0].astype(bf)
    w = u_all.shape[2]
    ba = o_gate_a_b[0].reshape(2, 1, w)
    bx = o_gate_x_b[0].reshape(2, 1, w)
    lru_lam = o_lru_lambda[0].reshape(2, 1, w)
    h0 = _lru_ctx(u_all, n, cl, cw, cb, wa, ba, wx, bx, lru_lam)
    hf = _lru_dir(0, u_all, n, h0, cw, cb, wa, ba, wx, bx, lru_lam)
    return _lru_dir(1, u_all, n, h0, cw, cb, wa, ba, wx, bx, lru_lam,
                    extra=(hf, gate1, xc, mod1, o_w_out[0].astype(bf), post_ln_g[1][None], post_ln_b[1][None]))
```

```python
import functools
import math

import jax
import jax.numpy as jnp
from jax import lax
from jax.experimental import pallas as pl
from jax.experimental.pallas import tpu as pltpu

GRID_W = 64
ROPE_THETA = 10000.0
LN_EPS = 1e-6
RMS_EPS = 1e-6

MLA_HEADS = 8
MLA_Q_LORA = 384
MLA_KV_LORA = 256
MLA_NOPE = 64
MLA_ROPE = 32
MLA_V = 64
MLA_QK = MLA_NOPE + MLA_ROPE
MLA_SCALE = MLA_QK ** -0.5

DIFF_HEADS = 4
DIFF_QK = 64
DIFF_V = 2 * DIFF_QK
DIFF_SCALE = DIFF_QK ** -0.5

LRU_BLOCKS = 8
LRU_C = 8.0
CONV_W = 4

DEPTH = 2
DEEPNORM_ALPHA = (2 * DEPTH) ** 0.25
LAMBDA_INIT_0 = 0.8 - 0.6 * math.exp(-0.3 * 0)

LANE = 128
SUBLANE = 8
LOG2E = 1.4426950408889634
NEG_BIG = -1e30

TOKEN_TILE = 256
Q_TILE = 512
KV_CHUNK = 1408
SCAN_TILE = 512
VMEM_LIMIT = 56 * 1024 * 1024

_G_CQ = (0, 384)
_G_CKV = (384, 640)
_G_KR = (640, 768)
_G_DQ = (768, 1280)
_G_DK = (1280, 1792)
_G_DV = (1792, 2304)
_G_GATE = (2304, 3328)
_EVEN_W = 3328


def _cparams(*sem):
    return pltpu.CompilerParams(dimension_semantics=sem, vmem_limit_bytes=VMEM_LIMIT)


def _bf16_dot(a, b):
    return jnp.dot(a.astype(jnp.bfloat16), b.astype(jnp.bfloat16), preferred_element_type=jnp.float32)


def _sigmoid(x):
    return 0.5 * jnp.tanh(0.5 * x) + 0.5


def _adaln_kernel(cond_ref, w_ref, b_ref, lamv_ref, mod_ref, lam_ref):
    cond = cond_ref[...]
    h = cond * _sigmoid(cond)
    mod_ref[0] = jnp.dot(h, w_ref[0], preferred_element_type=jnp.float32,
                         precision=lax.Precision.HIGHEST) + b_ref[0]
    lv = lamv_ref[...]
    d1 = jnp.sum(lv[0:1] * lv[1:2], axis=-1, keepdims=True)
    d2 = jnp.sum(lv[2:3] * lv[3:4], axis=-1, keepdims=True)
    lam = jnp.exp(d1) - jnp.exp(d2) + LAMBDA_INIT_0
    lam_ref[...] = jnp.broadcast_to(lam, lam_ref.shape)


def _adaln(cond, ada_w, ada_b, lamv):
    depth, d, d3 = ada_w.shape
    nj = d3 // d
    return pl.pallas_call(
        _adaln_kernel,
        grid=(depth, nj),
        in_specs=[
            pl.BlockSpec((SUBLANE, d), lambda l, j: (0, 0)),
            pl.BlockSpec((1, d, d), lambda l, j: (l, 0, j)),
            pl.BlockSpec((1, 1, d), lambda l, j: (l, 0, j)),
            pl.BlockSpec(lamv.shape, lambda l, j: (0, 0)),
        ],
        out_specs=[
            pl.BlockSpec((1, SUBLANE, d), lambda l, j: (l, 0, j)),
            pl.BlockSpec((SUBLANE, LANE), lambda l, j: (0, 0)),
        ],
        out_shape=[
            jax.ShapeDtypeStruct((depth, SUBLANE, d3), jnp.float32),
            jax.ShapeDtypeStruct((SUBLANE, LANE), jnp.float32),
        ],
        compiler_params=_cparams("arbitrary", "arbitrary"),
        name="adaln",
    )(cond, ada_w, ada_b.reshape(depth, 1, d3), lamv)


def _rms(x, g):
    return x * lax.rsqrt(jnp.mean(x * x, axis=-1, keepdims=True) + RMS_EPS) * g


def _rope_group(x, cos, sin, half, first_half_mask):
    partner = jnp.where(first_half_mask, pltpu.roll(x, LANE - half, 1), pltpu.roll(x, half, 1))
    return x * cos + partner * sin


def _even_proj_kernel(nx, x_ref, c_ref, mod_ref, w1_ref, qg_ref, wuq_ref, kvg_ref, wkv_ref,
                      ca_ref, sa_ref, cb_ref, sb_ref,
                      qt_ref, ka_ref, vat_ref, dqt_ref, dk_ref, dvt_ref, gate_ref):
    i = pl.program_id(1)
    xin = jnp.where(i >= nx, c_ref[0], x_ref[0])
    mod = mod_ref[0, 0]
    xm = xin * (1.0 + mod[1:2]) + mod[0:1]
    z = _bf16_dot(xm, w1_ref[...])

    lane = lax.broadcasted_iota(jnp.int32, (1, LANE), 1)
    mla_first = jnp.logical_and(lane >= MLA_NOPE, lane < MLA_NOPE + MLA_ROPE // 2)
    diff_first = (lane % DIFF_QK) < DIFF_QK // 2
    ca, sa, cb, sb = ca_ref[...], sa_ref[...], cb_ref[...], sb_ref[...]

    cqn = _rms(z[:, _G_CQ[0]:_G_CQ[1]], qg_ref[...])
    q = _bf16_dot(cqn, wuq_ref[...])
    q_heads = []
    for h in range(MLA_HEADS):
        qh = _rope_group(q[:, h * LANE:(h + 1) * LANE], ca, sa, MLA_ROPE // 2, mla_first)
        q_heads.append(qh * (MLA_SCALE * LOG2E))
    qt_ref[0] = jnp.concatenate(q_heads, axis=1).T.astype(qt_ref.dtype)

    ckvn = _rms(z[:, _G_CKV[0]:_G_CKV[1]], kvg_ref[...])
    kv = _bf16_dot(ckvn, wkv_ref[...])
    kr = _rope_group(z[:, _G_KR[0]:_G_KR[1]], ca, sa, MLA_ROPE // 2, mla_first)
    for h in range(MLA_HEADS):
        ka_ref[0, :, h * LANE:(h + 1) * LANE] = (kv[:, h * LANE:(h + 1) * LANE] + kr).astype(ka_ref.dtype)
    vat_ref[0] = kv[:, MLA_HEADS * LANE:].T.astype(vat_ref.dtype)

    dq_heads = []
    for h in range(DIFF_HEADS):
        s0 = _G_DQ[0] + h * LANE
        dq_heads.append(_rope_group(z[:, s0:s0 + LANE], cb, sb, DIFF_QK // 2, diff_first) * (DIFF_SCALE * LOG2E))
        s0 = _G_DK[0] + h * LANE
        dk_ref[0, :, h * LANE:(h + 1) * LANE] = _rope_group(
            z[:, s0:s0 + LANE], cb, sb, DIFF_QK // 2, diff_first).astype(dk_ref.dtype)
    dqt_ref[0] = jnp.concatenate(dq_heads, axis=1).T.astype(dqt_ref.dtype)
    dvt_ref[0] = z[:, _G_DV[0]:_G_DV[1]].T.astype(dvt_ref.dtype)
    gate_ref[0] = z[:, _G_GATE[0]:_G_GATE[1]]


def _even_proj(x, ctx, modsel, w1, qg, wuq, kvg, wkv, tabs):
    b, n, d = x.shape
    c = ctx.shape[1]
    t = TOKEN_TILE
    nx, nc = n // t, c // t
    nt = n + c
    full = lambda a: pl.BlockSpec(a.shape, lambda bi, i: (0,) * a.ndim)
    tab_spec = pl.BlockSpec((t, LANE), lambda bi, i: (i, 0))
    tok = lambda w: pl.BlockSpec((1, t, w), lambda bi, i: (bi, i, 0))
    tok_t = lambda w: pl.BlockSpec((1, w, t), lambda bi, i: (bi, 0, i))
    bf = jnp.bfloat16
    return pl.pallas_call(
        functools.partial(_even_proj_kernel, nx),
        grid=(b, nx + nc),
        in_specs=[
            pl.BlockSpec((1, t, d), lambda bi, i: (bi, jnp.minimum(i, nx - 1), 0)),
            pl.BlockSpec((1, t, d), lambda bi, i: (bi, jnp.maximum(i - nx, 0), 0)),
            pl.BlockSpec((1, 1, 3, d), lambda bi, i: (bi, jnp.where(i >= nx, 1, 0), 0, 0)),
            full(w1), full(qg), full(wuq), full(kvg), full(wkv),
            tab_spec, tab_spec, tab_spec, tab_spec,
        ],
        out_specs=[
            tok_t(MLA_HEADS * LANE), tok(MLA_HEADS * LANE), tok_t(MLA_HEADS * MLA_V),
            tok_t(DIFF_HEADS * LANE), tok(DIFF_HEADS * LANE), tok_t(DIFF_HEADS * DIFF_V),
            tok(MLA_HEADS * MLA_V + DIFF_HEADS * DIFF_V),
        ],
        out_shape=[
            jax.ShapeDtypeStruct((b, MLA_HEADS * LANE, nt), bf),
            jax.ShapeDtypeStruct((b, nt, MLA_HEADS * LANE), bf),
            jax.ShapeDtypeStruct((b, MLA_HEADS * MLA_V, nt), bf),
            jax.ShapeDtypeStruct((b, DIFF_HEADS * LANE, nt), bf),
            jax.ShapeDtypeStruct((b, nt, DIFF_HEADS * LANE), bf),
            jax.ShapeDtypeStruct((b, DIFF_HEADS * DIFF_V, nt), bf),
            jax.ShapeDtypeStruct((b, nt, MLA_HEADS * MLA_V + DIFF_HEADS * DIFF_V), jnp.float32),
        ],
        compiler_params=_cparams("parallel", "parallel"),
        name="even_proj",
    )(x, ctx, modsel, w1, qg, wuq, kvg, wkv, *tabs)


def _kv_chunk(nk):
    if nk <= KV_CHUNK:
        return nk
    for m in range(KV_CHUNK // LANE, 0, -1):
        if nk % (m * LANE) == 0 and (nk // (m * LANE)) % 2 == 0:
            return m * LANE
    raise ValueError(f"no even chunking of {nk} keys")


def _attn_sweep(n_chunks, tk, tq, n_qt, q_of, k_ref, vt_ref, s_scr, finish):
    dv = vt_ref.shape[1]
    ones = jnp.ones((2 * SUBLANE, tk), vt_ref.dtype)

    def scores(t, j, slot):
        k = k_ref[0, pl.ds(pl.multiple_of(j * tk, LANE), tk), :]
        cms = []
        for si, qt in enumerate(q_of(t)):
            s = jnp.dot(k, qt, preferred_element_type=jnp.float32)
            s_scr[slot, si] = s
            cms.append(jnp.max(s, axis=0, keepdims=True))
        return tuple(cms)

    def consume(j, slot, cms, state):
        vt = vt_ref[0, :, pl.ds(pl.multiple_of(j * tk, LANE), tk)]
        vt_ext = jnp.concatenate([vt, ones], axis=0)
        out = []
        for si in range(len(cms)):
            m, acc = state[2 * si], state[2 * si + 1]
            m_new = jnp.maximum(m, cms[si])
            alpha = jnp.exp2(m - m_new)
            p = jnp.exp2(s_scr[slot, si] - m_new).astype(vt.dtype)
            acc = alpha * acc + jnp.dot(vt_ext, p, preferred_element_type=jnp.float32)
            out += [m_new, acc]
        return tuple(out)

    n_ops = s_scr.shape[1]
    init = (jnp.full((1, tq), NEG_BIG, jnp.float32), jnp.zeros((dv + 2 * SUBLANE, tq), jnp.float32)) * n_ops
    accs = lambda st: [st[2 * si + 1] for si in range(n_ops)]

    if n_chunks == 1:
        def tile(t, carry):
            finish(t, accs(consume(0, 0, scores(t, 0, 0), init)))
            return carry
        lax.fori_loop(0, n_qt, tile, 0)
        return

    assert n_chunks % 2 == 0

    def tile(t, cm):
        def pair(i, carry):
            cm_a, st = carry
            cm_b = scores(t, 2 * i + 1, 1)
            st = consume(2 * i, 0, cm_a, st)
            cm_a = scores(t, 2 * i + 2, 0)
            st = consume(2 * i + 1, 1, cm_b, st)
            return cm_a, st

        cm_a, st = lax.fori_loop(0, n_chunks // 2 - 1, pair, (cm, init))
        cm_b = scores(t, n_chunks - 1, 1)
        st = consume(n_chunks - 2, 0, cm_a, st)
        cm_next = scores(jnp.minimum(t + 1, n_qt - 1), 0, 0)
        st = consume(n_chunks - 1, 1, cm_b, st)
        finish(t, accs(st))
        return cm_next

    lax.fori_loop(0, n_qt, tile, scores(0, 0, 0))


def _q_cols(t, tq):
    return pl.ds(pl.multiple_of(t * tq, LANE), tq)


def _mla_attn_kernel(n_chunks, tk, tq, qt_ref, k_ref, vt_ref, o_ref, s_scr):
    def finish(t, accs):
        acc = accs[0]
        o_ref[0, :, _q_cols(t, tq)] = acc[:MLA_V] / acc[MLA_V:MLA_V + 1]

    _attn_sweep(n_chunks, tk, tq, qt_ref.shape[2] // tq, lambda t: [qt_ref[0, :, _q_cols(t, tq)]],
                k_ref, vt_ref, s_scr, finish)


def _attn_call(kernel_fn, name, heads, dv, n_ops, qt, k, vt, nq, extra=()):
    b, _, nk = vt.shape
    tq = min(Q_TILE, nq)
    tk = _kv_chunk(nk)
    return pl.pallas_call(
        functools.partial(kernel_fn, nk // tk, tk, tq),
        grid=(b, heads),
        in_specs=[
            pl.BlockSpec((1, LANE, nq), lambda bi, h: (bi, h, 0)),
            pl.BlockSpec((1, nk, LANE), lambda bi, h: (bi, 0, h)),
            pl.BlockSpec((1, dv, nk), lambda bi, h: (bi, h, 0)),
        ] + [pl.BlockSpec(a.shape, lambda bi, h: (0, 0)) for a in extra],
        out_specs=pl.BlockSpec((1, dv, nq), lambda bi, h: (bi, h, 0)),
        out_shape=jax.ShapeDtypeStruct((b, heads * dv, nq), jnp.float32),
        scratch_shapes=[pltpu.VMEM((2, n_ops, tk, tq), jnp.float32)],
        compiler_params=_cparams("parallel", "parallel"),
        name=name,
    )(qt, k, vt, *extra)


def _mla_attn(qt, ka, vat, nq):
    return _attn_call(_mla_attn_kernel, "mla_attn", MLA_HEADS, MLA_V, 1, qt, ka, vat, nq)


def _diff_attn_kernel(n_chunks, tk, tq, qt_ref, k_ref, vt_ref, lam_ref, g_ref, o_ref, s_scr):
    row = lax.broadcasted_iota(jnp.int32, (LANE, 1), 0)

    def q_of(t):
        qt = qt_ref[0, :, _q_cols(t, tq)]
        zero = jnp.zeros_like(qt)
        return [jnp.where(row < DIFF_QK, qt, zero), jnp.where(row >= DIFF_QK, qt, zero)]

    def finish(t, accs):
        a1, a2 = accs
        lam = lam_ref[0:1, 0:1]
        o = a1[:DIFF_V] / a1[DIFF_V:DIFF_V + 1] - lam * (a2[:DIFF_V] / a2[DIFF_V:DIFF_V + 1])
        o = o * lax.rsqrt(jnp.mean(o * o, axis=0, keepdims=True) + RMS_EPS) * g_ref[...]
        o_ref[0, :, _q_cols(t, tq)] = o * (1.0 - LAMBDA_INIT_0)

    _attn_sweep(n_chunks, tk, tq, qt_ref.shape[2] // tq, q_of, k_ref, vt_ref, s_scr, finish)


def _diff_attn(dqt, dk, dvt, lam, subln_col, nq):
    return _attn_call(_diff_attn_kernel, "diff_attn", DIFF_HEADS, DIFF_V, 2, dqt, dk, dvt, nq,
                      extra=(lam, subln_col))


def _residual_ln(xin, y, gate_vec, ln_g, ln_b):
    r = DEEPNORM_ALPHA * xin + gate_vec * y
    mu = jnp.mean(r, axis=-1, keepdims=True)
    rc = r - mu
    var = jnp.mean(rc * rc, axis=-1, keepdims=True)
    return rc * lax.rsqrt(var + LN_EPS) * ln_g + ln_b


def _even_merge_kernel(nx, x_ref, c_ref, mod_ref, oax_ref, oac_ref, obx_ref, obc_ref, gate_ref,
                       wout_ref, lng_ref, lnb_ref, out_ref):
    i = pl.program_id(1)
    is_ctx = i >= nx
    xin = jnp.where(is_ctx, c_ref[0], x_ref[0])
    oa = jnp.where(is_ctx, oac_ref[0], oax_ref[0]).T
    ob = jnp.where(is_ctx, obc_ref[0], obx_ref[0]).T
    o = jnp.concatenate([oa, ob], axis=1)
    g = gate_ref[0]
    y = _bf16_dot(o * (g * _sigmoid(g)), wout_ref[...])
    out_ref[0] = _residual_ln(xin, y, mod_ref[0, 0][2:3], lng_ref[...], lnb_ref[...])


def _even_merge(x, ctx, modsel, oax, oac, obx, obc, gate, wout, ln_g, ln_b):
    b, n, d = x.shape
    c = ctx.shape[1]
    t = TOKEN_TILE
    nx, nc = n // t, c // t
    wa, wb = oax.shape[1], obx.shape[1]
    full = lambda a: pl.BlockSpec(a.shape, lambda bi, i: (0,) * a.ndim)
    xi = lambda bi, i: (bi, jnp.minimum(i, nx - 1), 0)
    ci = lambda bi, i: (bi, jnp.maximum(i - nx, 0), 0)
    xit = lambda bi, i: (bi, 0, jnp.minimum(i, nx - 1))
    cit = lambda bi, i: (bi, 0, jnp.maximum(i - nx, 0))
    return pl.pallas_call(
        functools.partial(_even_merge_kernel, nx),
        grid=(b, nx + nc),
        in_specs=[
            pl.BlockSpec((1, t, d), xi),
            pl.BlockSpec((1, t, d), ci),
            pl.BlockSpec((1, 1, 3, d), lambda bi, i: (bi, jnp.where(i >= nx, 1, 0), 0, 0)),
            pl.BlockSpec((1, wa, t), xit), pl.BlockSpec((1, wa, t), cit),
            pl.BlockSpec((1, wb, t), xit), pl.BlockSpec((1, wb, t), cit),
            pl.BlockSpec((1, t, wa + wb), lambda bi, i: (bi, i, 0)),
            full(wout), full(ln_g), full(ln_b),
        ],
        out_specs=pl.BlockSpec((1, t, d), lambda bi, i: (bi, i, 0)),
        out_shape=jax.ShapeDtypeStruct((b, n + c, d), jnp.float32),
        compiler_params=_cparams("parallel", "parallel"),
        name="even_merge",
    )(x, ctx, modsel, oax, oac, obx, obc, gate, wout, ln_g, ln_b)


def _odd_proj_kernel(xc_ref, mod_ref, w_ref, u_ref, gate_ref):
    mod = mod_ref[0, 0]
    xm = xc_ref[0] * (1.0 + mod[1:2]) + mod[0:1]
    z = _bf16_dot(xm, w_ref[...])
    w = u_ref.shape[2]
    u_ref[0] = z[:, :w]
    gate_ref[0] = z[:, w:]


def _odd_proj(xc, modsel, w_in, nx):
    b, nt, d = xc.shape
    t = TOKEN_TILE
    w = w_in.shape[1] // 2
    return pl.pallas_call(
        _odd_proj_kernel,
        grid=(b, nt // t),
        in_specs=[
            pl.BlockSpec((1, t, d), lambda bi, i: (bi, i, 0)),
            pl.BlockSpec((1, 1, 3, d), lambda bi, i: (bi, jnp.where(i >= nx, 1, 0), 0, 0)),
            pl.BlockSpec(w_in.shape, lambda bi, i: (0, 0)),
        ],
        out_specs=[pl.BlockSpec((1, t, w), lambda bi, i: (bi, i, 0))] * 2,
        out_shape=[jax.ShapeDtypeStruct((b, nt, w), jnp.float32)] * 2,
        compiler_params=_cparams("parallel", "parallel"),
        name="odd_proj",
    )(xc, modsel, w_in)


def _conv_tile(u, prev, nxt, cw, cb):
    t = u.shape[0]
    row = lax.broadcasted_iota(jnp.int32, (SUBLANE, 1), 0)
    r1, r2, rp = pltpu.roll(u, 1, 0), pltpu.roll(u, 2, 0), pltpu.roll(u, t - 1, 0)
    m1 = jnp.concatenate([jnp.where(row == 0, prev[7:8], r1[:SUBLANE]), r1[SUBLANE:]], 0)
    m2 = jnp.concatenate([jnp.where(row == 0, prev[6:7], jnp.where(row == 1, prev[7:8], r2[:SUBLANE])),
                          r2[SUBLANE:]], 0)
    p1 = jnp.concatenate([rp[:t - SUBLANE], jnp.where(row == SUBLANE - 1, nxt[0:1], rp[t - SUBLANE:])], 0)
    return cw[0:1] * m2 + cw[1:2] * m1 + cw[2:3] * u + cw[3:4] * p1 + cb


def _lru_coeffs(uc, wa_ref, ba, wx_ref, bx, lam, a_ref, b_ref):
    sp = jnp.log1p(jnp.exp(-lam))
    for k in range(LRU_BLOCKS):
        sl = slice(k * LANE, (k + 1) * LANE)
        ub = uc[:, sl]
        r = _sigmoid(_bf16_dot(ub, wa_ref[k]) + ba[:, sl])
        ig = _sigmoid(_bf16_dot(ub, wx_ref[k]) + bx[:, sl])
        log_a = (-LRU_C) * r * sp[:, sl]
        a = jnp.exp(log_a)
        a_ref[:, sl] = a
        b_ref[:, sl] = jnp.sqrt(-jnp.tanh(log_a) * (a * a + 1.0)) * (ig * ub)


def _scan_rows(a_ref, b_ref, h_out_ref, h0, t, reverse):
    def step(s, h):
        r = (t - 1 - s) if reverse else s
        h = a_ref[pl.ds(r, 1), :] * h + b_ref[pl.ds(r, 1), :]
        h_out_ref[pl.ds(r, 1), :] = h
        return h

    return lax.fori_loop(0, t, step, h0, unroll=8)


def _lru_ctx_kernel(u_ref, cw_ref, cb_ref, wa_ref, ba_ref, wx_ref, bx_ref, lam_ref, h_ref,
                    a_s, b_s, hs):
    u = u_ref[0]
    zero8 = jnp.zeros((SUBLANE, u.shape[1]), jnp.float32)
    uc = _conv_tile(u, zero8, zero8, cw_ref[...], cb_ref[...])
    t = u.shape[0]
    for d in range(2):
        _lru_coeffs(uc, wa_ref.at[d], ba_ref[d], wx_ref.at[d], bx_ref[d], lam_ref[d], a_s, b_s)
        h = _scan_rows(a_s, b_s, hs, jnp.zeros((1, u.shape[1]), jnp.float32), t, reverse=(d == 1))
        h_ref[0, d:d + 1, :] = h


def _lru_ctx(u_all, n, c, cw, cb, wa, ba, wx, bx, lam):
    b, _, w = u_all.shape
    full = lambda a: pl.BlockSpec(a.shape, lambda bi: (0,) * a.ndim)
    return pl.pallas_call(
        _lru_ctx_kernel,
        grid=(b,),
        in_specs=[pl.BlockSpec((1, c, w), lambda bi: (bi, n // c, 0)),
                  full(cw), full(cb), full(wa), full(ba), full(wx), full(bx), full(lam)],
        out_specs=pl.BlockSpec((1, 2, w), lambda bi: (bi, 0, 0)),
        out_shape=jax.ShapeDtypeStruct((b, 2, w), jnp.float32),
        scratch_shapes=[pltpu.VMEM((c, w), jnp.float32)] * 3,
        compiler_params=_cparams("parallel"),
        name="lru_ctx",
    )(u_all, cw, cb, wa, ba, wx, bx, lam)


def _lru_dir_kernel(direction, n_tiles, *refs):
    reverse = direction == 1
    if reverse:
        (u_ref, up_ref, un_ref, h0_ref, cw_ref, cb_ref, wa_ref, ba_ref, wx_ref, bx_ref, lam_ref,
         hf_ref, gate_ref, x_ref, mod_ref, wout_ref, lng_ref, lnb_ref, out_ref, a_s, b_s, hs, carry) = refs
    else:
        (u_ref, up_ref, un_ref, h0_ref, cw_ref, cb_ref, wa_ref, ba_ref, wx_ref, bx_ref, lam_ref,
         out_ref, a_s, b_s, carry) = refs
        hs = out_ref.at[0]
    step = pl.program_id(1)
    tile = (n_tiles - 1 - step) if reverse else step

    @pl.when(step == 0)
    def _():
        carry[...] = h0_ref[0, direction:direction + 1, :]

    u = u_ref[0]
    t = u.shape[0]
    prev = jnp.where(tile == 0, 0.0, up_ref[0])
    nxt = jnp.where(tile == n_tiles - 1, 0.0, un_ref[0])
    uc = _conv_tile(u, prev, nxt, cw_ref[...], cb_ref[...])
    _lru_coeffs(uc, wa_ref.at[direction], ba_ref[direction], wx_ref.at[direction], bx_ref[direction],
                lam_ref[direction], a_s, b_s)
    carry[...] = _scan_rows(a_s, b_s, hs, carry[...], t, reverse)

    if reverse:
        g = gate_ref[0]
        hx = hf_ref[0] + hs[...]
        y = _bf16_dot(hx * (g * _sigmoid(g)), wout_ref[...])
        out_ref[0] = _residual_ln(x_ref[0], y, mod_ref[0, 0][2:3], lng_ref[...], lnb_ref[...])


def _lru_dir(direction, u_all, n, h0, cw, cb, wa, ba, wx, bx, lam, extra=None):
    b, _, w = u_all.shape
    t = min(SCAN_TILE, n)
    n_tiles = n // t
    r = t // SUBLANE
    if direction == 1:
        tidx = lambda s: n_tiles - 1 - s
    else:
        tidx = lambda s: s
    full = lambda a: pl.BlockSpec(a.shape, lambda bi, s: (0,) * a.ndim)
    tile_spec = lambda width: pl.BlockSpec((1, t, width), lambda bi, s: (bi, tidx(s), 0))
    in_specs = [
        tile_spec(w),
        pl.BlockSpec((1, SUBLANE, w), lambda bi, s: (bi, jnp.maximum(tidx(s) * r - 1, 0), 0)),
        pl.BlockSpec((1, SUBLANE, w), lambda bi, s: (bi, (tidx(s) + 1) * r, 0)),
        pl.BlockSpec((1, 2, w), lambda bi, s: (bi, 0, 0)),
        full(cw), full(cb), full(wa), full(ba), full(wx), full(bx), full(lam),
    ]
    args = [u_all, u_all, u_all, h0, cw, cb, wa, ba, wx, bx, lam]
    scratch = [pltpu.VMEM((t, w), jnp.float32), pltpu.VMEM((t, w), jnp.float32)]
    if direction == 1:
        hf, gate, xc, modsel, wout, ln_g, ln_b = extra
        d = xc.shape[2]
        in_specs += [tile_spec(w), tile_spec(w), tile_spec(d),
                     pl.BlockSpec((1, 1, 3, d), lambda bi, s: (bi, 0, 0, 0)),
                     full(wout), full(ln_g), full(ln_b)]
        args += [hf, gate, xc, modsel, wout, ln_g, ln_b]
        scratch.append(pltpu.VMEM((t, w), jnp.float32))
        out_w = d
    else:
        out_w = w
    scratch.append(pltpu.VMEM((1, w), jnp.float32))
    return pl.pallas_call(
        functools.partial(_lru_dir_kernel, direction, n_tiles),
        grid=(b, n_tiles),
        in_specs=in_specs,
        out_specs=pl.BlockSpec((1, t, out_w), lambda bi, s: (bi, tidx(s), 0)),
        out_shape=jax.ShapeDtypeStruct((b, n, out_w), jnp.float32),
        scratch_shapes=scratch,
        compiler_params=_cparams("parallel", "arbitrary"),
        name="lru_bwd_out" if direction == 1 else "lru_fwd",
    )(*args)


def _rope_tables(n, c):
    t = jnp.arange(n)
    rows = (t // GRID_W).astype(jnp.float32)
    cols = (t % GRID_W).astype(jnp.float32)

    def ang(rot_dim):
        n_freq = rot_dim // 4
        freqs = ROPE_THETA ** (-jnp.arange(n_freq, dtype=jnp.float32) / n_freq)
        return jnp.concatenate([rows[:, None] * freqs, cols[:, None] * freqs], -1)

    aa, ab = ang(MLA_ROPE), ang(DIFF_QK)
    one = jnp.ones((n, MLA_NOPE), jnp.float32)
    zero = jnp.zeros((n, MLA_NOPE), jnp.float32)
    pad1 = jnp.ones((n, LANE - MLA_QK), jnp.float32)
    pad0 = jnp.zeros((n, LANE - MLA_QK), jnp.float32)
    ca = jnp.concatenate([one, jnp.cos(aa), jnp.cos(aa), pad1], 1)
    sa = jnp.concatenate([zero, -jnp.sin(aa), jnp.sin(aa), pad0], 1)
    cb = jnp.concatenate([jnp.cos(ab), jnp.cos(ab)] * 2, 1)
    sb = jnp.concatenate([-jnp.sin(ab), jnp.sin(ab)] * 2, 1)
    ident = lambda tab, v: jnp.concatenate([tab, jnp.full((c, LANE), v, jnp.float32)], 0)
    return ident(ca, 1.0), ident(sa, 0.0), ident(cb, 1.0), ident(sb, 0.0)


def _even_weights(w_in, w_uq, w_ukv):
    d = w_in.shape[0]
    bf = jnp.bfloat16
    o = [0, 384, 640, 672, 1184, 1696, 2208, 3232]
    cq, ckv, kr, dq, dk, dv, gate = (w_in[:, o[i]:o[i + 1]] for i in range(7))
    kr_group = jnp.concatenate([jnp.zeros((d, MLA_NOPE), w_in.dtype), kr,
                                jnp.zeros((d, LANE - MLA_QK), w_in.dtype)], 1)
    w1 = jnp.concatenate([cq, ckv, kr_group, dq, dk, dv, gate], 1).astype(bf)
    wuq = jnp.pad(w_uq.reshape(MLA_Q_LORA, MLA_HEADS, MLA_QK),
                  ((0, 0), (0, 0), (0, LANE - MLA_QK))).reshape(MLA_Q_LORA, MLA_HEADS * LANE).astype(bf)
    ukv = w_ukv.reshape(MLA_KV_LORA, MLA_HEADS, MLA_NOPE + MLA_V)
    wuk = jnp.pad(ukv[..., :MLA_NOPE], ((0, 0), (0, 0), (0, LANE - MLA_NOPE))).reshape(MLA_KV_LORA, MLA_HEADS * LANE)
    wuv = ukv[..., MLA_NOPE:].reshape(MLA_KV_LORA, MLA_HEADS * MLA_V)
    wkv = jnp.concatenate([wuk, wuv], 1).astype(bf)
    return w1, wuq, wkv


def _mod_select(mods_l, b):
    d = mods_l.shape[1] // 3
    mx = mods_l[:b].reshape(b, 3, d)
    mc = jnp.broadcast_to(mods_l[b].reshape(1, 3, d), (b, 3, d))
    return jnp.stack([mx, mc], axis=1)


def kernel(x, c, ctx, c_ctx, ada_w, ada_b, post_ln_g, post_ln_b, e_w_in, e_q_norm_g, e_w_uq, e_kv_norm_g, e_w_ukv, e_lam_q1, e_lam_k1, e_lam_q2, e_lam_k2, e_subln_g, e_w_out, o_w_in, o_conv_w, o_conv_b, o_gate_a_w, o_gate_a_b, o_gate_x_w, o_gate_x_b, o_lru_lambda, o_w_out):
    b, n, d = x.shape
    cl = ctx.shape[1]
    assert ada_w.shape[0] == DEPTH and b < SUBLANE
    assert n % Q_TILE == 0 and cl % TOKEN_TILE == 0 and n % cl == 0
    bf = jnp.bfloat16

    cond = jnp.concatenate([c, c_ctx[None], jnp.zeros((SUBLANE - b - 1, d), c.dtype)], 0)
    lamv = jnp.concatenate([e_lam_q1, e_lam_k1, e_lam_q2, e_lam_k2], 0)
    mods, lam = _adaln(cond, ada_w, ada_b, lamv)
    mod0, mod1 = _mod_select(mods[0], b), _mod_select(mods[1], b)

    w1, wuq, wkv = _even_weights(e_w_in[0], e_w_uq[0], e_w_ukv[0])
    tabs = _rope_tables(n, cl)
    qt, ka, vat, dqt, dk, dvt, gate = _even_proj(
        x, ctx, mod0, w1, e_q_norm_g[0][None], wuq, e_kv_norm_g[0][None], wkv, tabs)
    subln_col = e_subln_g[0][:, None]
    oax = _mla_attn(qt, ka, vat, n)
    obx = _diff_attn(dqt, dk, dvt, lam, subln_col, n)
    oac = _mla_attn(qt[:, :, n:], ka[:, n:], vat[:, :, n:], cl)
    obc = _diff_attn(dqt[:, :, n:], dk[:, n:], dvt[:, :, n:], lam, subln_col, cl)
    xc = _even_merge(x, ctx, mod0, oax, oac, obx, obc, gate, e_w_out[0].astype(bf),
                     post_ln_g[0][None], post_ln_b[0][None])

    u_all, gate1 = _odd_proj(xc, mod1, o_w_in[0].astype(bf), n // TOKEN_TILE)
    cw, cb = o_conv_w[0], o_conv_b[0][None]
    wa, wx = o_gate_a_w[0].astype(bf), o_gate_x_w[0].astype(bf)
    w = u_all.shape[2]
    ba = o_gate_a_b[0].reshape(2, 1, w)
    bx = o_gate_x_b[0].reshape(2, 1, w)
    lru_lam = o_lru_lambda[0].reshape(2, 1, w)
    h0 = _lru_ctx(u_all, n, cl, cw, cb, wa, ba, wx, bx, lru_lam)
    hf = _lru_dir(0, u_all, n, h0, cw, cb, wa, ba, wx, bx, lru_lam)
    return _lru_dir(1, u_all, n, h0, cw, cb, wa, ba, wx, bx, lru_lam,
                    extra=(hf, gate1, xc, mod1, o_w_out[0].astype(bf), post_ln_g[1][None], post_ln_b[1][None]))
```

```python
import functools
import math

import jax
import jax.numpy as jnp
from jax import lax
from jax.experimental import pallas as pl
from jax.experimental.pallas import tpu as pltpu

GRID_W = 64
ROPE_THETA = 10000.0
LN_EPS = 1e-6
RMS_EPS = 1e-6

MLA_HEADS = 8
MLA_Q_LORA = 384
MLA_KV_LORA = 256
MLA_NOPE = 64
MLA_ROPE = 32
MLA_V = 64
MLA_QK = MLA_NOPE + MLA_ROPE
MLA_SCALE = MLA_QK ** -0.5

DIFF_HEADS = 4
DIFF_QK = 64
DIFF_V = 2 * DIFF_QK
DIFF_SCALE = DIFF_QK ** -0.5

LRU_BLOCKS = 8
LRU_C = 8.0
CONV_W = 4

DEPTH = 2
DEEPNORM_ALPHA = (2 * DEPTH) ** 0.25
LAMBDA_INIT_0 = 0.8 - 0.6 * math.exp(-0.3 * 0)

LANE = 128
SUBLANE = 8
MXU_DEPTH = 256
LOG2E = 1.4426950408889634
NEG_BIG = -1e30

TOKEN_TILE = 256
Q_TILE = 1024
KV_CHUNK = 1408
SCAN_TILE = 512
VMEM_LIMIT = 56 * 1024 * 1024

_G_CQ = (0, 384)
_G_CKV = (384, 640)
_G_KR = (640, 768)
_G_DQ = (768, 1280)
_G_DK = (1280, 1792)
_G_DV = (1792, 2304)
_G_GATE = (2304, 3328)
_EVEN_W = 3328


def _cparams(*sem):
    return pltpu.CompilerParams(dimension_semantics=sem, vmem_limit_bytes=VMEM_LIMIT)


def _bf16_dot(a, b):
    return jnp.dot(a.astype(jnp.bfloat16), b.astype(jnp.bfloat16), preferred_element_type=jnp.float32)


def _sigmoid(x):
    return 1.0 / (1.0 + jnp.exp(-x))


def _adaln_kernel(cond_ref, w_ref, b_ref, lamv_ref, mod_ref, lam_ref):
    cond = cond_ref[...]
    h = cond * _sigmoid(cond)
    mod_ref[0] = jnp.dot(h, w_ref[0], preferred_element_type=jnp.float32,
                         precision=lax.Precision.HIGHEST) + b_ref[0]
    lv = lamv_ref[...]
    d1 = jnp.sum(lv[0:1] * lv[1:2], axis=-1, keepdims=True)
    d2 = jnp.sum(lv[2:3] * lv[3:4], axis=-1, keepdims=True)
    lam = jnp.exp(d1) - jnp.exp(d2) + LAMBDA_INIT_0
    lam_ref[...] = jnp.broadcast_to(lam, lam_ref.shape)


def _adaln(cond, ada_w, ada_b, lamv):
    depth, d, d3 = ada_w.shape
    nj = d3 // d
    return pl.pallas_call(
        _adaln_kernel,
        grid=(depth, nj),
        in_specs=[
            pl.BlockSpec((SUBLANE, d), lambda l, j: (0, 0)),
            pl.BlockSpec((1, d, d), lambda l, j: (l, 0, j)),
            pl.BlockSpec((1, 1, d), lambda l, j: (l, 0, j)),
            pl.BlockSpec(lamv.shape, lambda l, j: (0, 0)),
        ],
        out_specs=[
            pl.BlockSpec((1, SUBLANE, d), lambda l, j: (l, 0, j)),
            pl.BlockSpec((SUBLANE, LANE), lambda l, j: (0, 0)),
        ],
        out_shape=[
            jax.ShapeDtypeStruct((depth, SUBLANE, d3), jnp.float32),
            jax.ShapeDtypeStruct((SUBLANE, LANE), jnp.float32),
        ],
        compiler_params=_cparams("arbitrary", "arbitrary"),
        name="adaln",
    )(cond, ada_w, ada_b.reshape(depth, 1, d3), lamv)


def _rms(x, g):
    return x * lax.rsqrt(jnp.mean(x * x, axis=-1, keepdims=True) + RMS_EPS) * g


def _rope_group(x, cos, sin, half, first_half_mask):
    partner = jnp.where(first_half_mask, pltpu.roll(x, LANE - half, 1), pltpu.roll(x, half, 1))
    return x * cos + partner * sin


def _even_proj_kernel(nx, x_ref, c_ref, mod_ref, w1_ref, qg_ref, wuq_ref, kvg_ref, wkv_ref,
                      ca_ref, sa_ref, cb_ref, sb_ref,
                      qt_ref, ka_ref, vat_ref, dqt_ref, dk_ref, dvt_ref, gate_ref):
    i = pl.program_id(1)
    xin = jnp.where(i >= nx, c_ref[0], x_ref[0])
    mod = mod_ref[0, 0]
    xm = xin * (1.0 + mod[1:2]) + mod[0:1]
    z = _bf16_dot(xm, w1_ref[...])

    lane = lax.broadcasted_iota(jnp.int32, (1, LANE), 1)
    mla_first = jnp.logical_and(lane >= MLA_NOPE, lane < MLA_NOPE + MLA_ROPE // 2)
    diff_first = (lane % DIFF_QK) < DIFF_QK // 2
    ca, sa, cb, sb = ca_ref[...], sa_ref[...], cb_ref[...], sb_ref[...]

    cqn = _rms(z[:, _G_CQ[0]:_G_CQ[1]], qg_ref[...])
    q = _bf16_dot(cqn, wuq_ref[...])
    q_heads = []
    for h in range(MLA_HEADS):
        qh = _rope_group(q[:, h * LANE:(h + 1) * LANE], ca, sa, MLA_ROPE // 2, mla_first)
        q_heads.append(qh * (MLA_SCALE * LOG2E))
    qt_ref[0] = jnp.concatenate(q_heads, axis=1).T.astype(qt_ref.dtype)

    ckvn = _rms(z[:, _G_CKV[0]:_G_CKV[1]], kvg_ref[...])
    kv = _bf16_dot(ckvn, wkv_ref[...])
    kr = _rope_group(z[:, _G_KR[0]:_G_KR[1]], ca, sa, MLA_ROPE // 2, mla_first)
    for h in range(MLA_HEADS):
        ka_ref[0, :, h * LANE:(h + 1) * LANE] = (kv[:, h * LANE:(h + 1) * LANE] + kr).astype(ka_ref.dtype)
    vat_ref[0] = kv[:, MLA_HEADS * LANE:].T.astype(vat_ref.dtype)

    dq_heads = []
    for h in range(DIFF_HEADS):
        s0 = _G_DQ[0] + h * LANE
        dq_heads.append(_rope_group(z[:, s0:s0 + LANE], cb, sb, DIFF_QK // 2, diff_first) * (DIFF_SCALE * LOG2E))
        s0 = _G_DK[0] + h * LANE
        dk_ref[0, :, h * LANE:(h + 1) * LANE] = _rope_group(
            z[:, s0:s0 + LANE], cb, sb, DIFF_QK // 2, diff_first).astype(dk_ref.dtype)
    dqt_ref[0] = jnp.concatenate(dq_heads, axis=1).T.astype(dqt_ref.dtype)
    dvt_ref[0] = z[:, _G_DV[0]:_G_DV[1]].T.astype(dvt_ref.dtype)
    gate_ref[0] = z[:, _G_GATE[0]:_G_GATE[1]]


def _even_proj(x, ctx, modsel, w1, qg, wuq, kvg, wkv, tabs):
    b, n, d = x.shape
    c = ctx.shape[1]
    t = TOKEN_TILE
    nx, nc = n // t, c // t
    nt = n + c
    full = lambda a: pl.BlockSpec(a.shape, lambda bi, i: (0,) * a.ndim)
    tab_spec = pl.BlockSpec((t, LANE), lambda bi, i: (i, 0))
    tok = lambda w: pl.BlockSpec((1, t, w), lambda bi, i: (bi, i, 0))
    tok_t = lambda w: pl.BlockSpec((1, w, t), lambda bi, i: (bi, 0, i))
    bf = jnp.bfloat16
    return pl.pallas_call(
        functools.partial(_even_proj_kernel, nx),
        grid=(b, nx + nc),
        in_specs=[
            pl.BlockSpec((1, t, d), lambda bi, i: (bi, jnp.minimum(i, nx - 1), 0)),
            pl.BlockSpec((1, t, d), lambda bi, i: (bi, jnp.maximum(i - nx, 0), 0)),
            pl.BlockSpec((1, 1, 3, d), lambda bi, i: (bi, jnp.where(i >= nx, 1, 0), 0, 0)),
            full(w1), full(qg), full(wuq), full(kvg), full(wkv),
            tab_spec, tab_spec, tab_spec, tab_spec,
        ],
        out_specs=[
            tok_t(MLA_HEADS * LANE), tok(MLA_HEADS * LANE), tok_t(MLA_HEADS * MLA_V),
            tok_t(DIFF_HEADS * LANE), tok(DIFF_HEADS * LANE), tok_t(DIFF_HEADS * DIFF_V),
            tok(MLA_HEADS * MLA_V + DIFF_HEADS * DIFF_V),
        ],
        out_shape=[
            jax.ShapeDtypeStruct((b, MLA_HEADS * LANE, nt), bf),
            jax.ShapeDtypeStruct((b, nt, MLA_HEADS * LANE), bf),
            jax.ShapeDtypeStruct((b, MLA_HEADS * MLA_V, nt), bf),
            jax.ShapeDtypeStruct((b, DIFF_HEADS * LANE, nt), bf),
            jax.ShapeDtypeStruct((b, nt, DIFF_HEADS * LANE), bf),
            jax.ShapeDtypeStruct((b, DIFF_HEADS * DIFF_V, nt), bf),
            jax.ShapeDtypeStruct((b, nt, MLA_HEADS * MLA_V + DIFF_HEADS * DIFF_V), jnp.float32),
        ],
        compiler_params=_cparams("parallel", "parallel"),
        name="even_proj",
    )(x, ctx, modsel, w1, qg, wuq, kvg, wkv, *tabs)


def _kv_chunk(nk):
    if nk <= KV_CHUNK:
        return nk
    for m in range(KV_CHUNK // LANE, 0, -1):
        if nk % (m * LANE) == 0 and (nk // (m * LANE)) % 2 == 0:
            return m * LANE
    raise ValueError(f"no even chunking of {nk} keys")


def _attn_sweep(n_chunks, tk, tq, n_qt, q_of, k_ref, vt_ref, s_scr, finish):
    dv = vt_ref.shape[1]

    def scores(t, j, slot):
        k = k_ref[0, pl.ds(pl.multiple_of(j * tk, LANE), tk), :]
        cms = []
        for si, qt in enumerate(q_of(t)):
            s = jnp.dot(k, qt, preferred_element_type=jnp.float32)
            s_scr[slot, si] = s
            cms.append(jnp.max(s, axis=0, keepdims=True))
        return tuple(cms)

    n_ops = s_scr.shape[1]
    ones = jnp.ones((2 * SUBLANE, MXU_DEPTH), vt_ref.dtype)
    init = (jnp.full((1, tq), NEG_BIG, jnp.float32), jnp.zeros((dv + 2 * SUBLANE, tq), jnp.float32)) * n_ops
    blocks = [(r0, min(MXU_DEPTH, tk - r0)) for r0 in range(0, tk, MXU_DEPTH)]

    def step(nxt, j_cur, slot_cur, cms, state):
        off_cur = pl.multiple_of(j_cur * tk, LANE)
        if nxt is not None:
            t_nxt, j_nxt, slot_nxt = nxt
            off_nxt = pl.multiple_of(j_nxt * tk, LANE)
            q_nxt = q_of(t_nxt)
        m_new = [jnp.maximum(state[2 * si], cms[si]) for si in range(n_ops)]
        cm_nxt, pv = [None] * n_ops, [None] * n_ops
        for r0, r in blocks:
            if nxt is not None:
                k = k_ref[0, pl.ds(off_nxt + r0, r), :]
            vt = vt_ref[0, :, pl.ds(off_cur + r0, r)]
            vt_ext = jnp.concatenate([vt, ones[:, :r]], axis=0)
            for si in range(n_ops):
                if nxt is not None:
                    s = jnp.dot(k, q_nxt[si], preferred_element_type=jnp.float32)
                    s_scr[slot_nxt, si, r0:r0 + r] = s
                    cmax = jnp.max(s, axis=0, keepdims=True)
                    cm_nxt[si] = cmax if cm_nxt[si] is None else jnp.maximum(cm_nxt[si], cmax)
                p = jnp.exp2(s_scr[slot_cur, si, r0:r0 + r] - m_new[si]).astype(vt.dtype)
                d = jnp.dot(vt_ext, p, preferred_element_type=jnp.float32)
                pv[si] = d if pv[si] is None else pv[si] + d
        out = []
        for si in range(n_ops):
            alpha = jnp.exp2(state[2 * si] - m_new[si])
            out += [m_new[si], alpha * state[2 * si + 1] + pv[si]]
        return tuple(cm_nxt), tuple(out)

    def result(st):
        return [st[2 * si + 1][:dv] / st[2 * si + 1][dv:dv + 1] for si in range(n_ops)]

    if n_chunks == 1:
        def tile(t, carry):
            finish(t, result(step(None, 0, 0, scores(t, 0, 0), init)[1]))
            return carry
        lax.fori_loop(0, n_qt, tile, 0)
        return

    assert n_chunks % 2 == 0

    def tile(t, cm):
        def pair(i, carry):
            cm_a, st = carry
            cm_b, st = step((t, 2 * i + 1, 1), 2 * i, 0, cm_a, st)
            cm_a, st = step((t, 2 * i + 2, 0), 2 * i + 1, 1, cm_b, st)
            return cm_a, st

        cm_a, st = lax.fori_loop(0, n_chunks // 2 - 1, pair, (cm, init))
        cm_b, st = step((t, n_chunks - 1, 1), n_chunks - 2, 0, cm_a, st)
        cm_next, st = step((jnp.minimum(t + 1, n_qt - 1), 0, 0), n_chunks - 1, 1, cm_b, st)
        finish(t, result(st))
        return cm_next

    lax.fori_loop(0, n_qt, tile, scores(0, 0, 0))


def _q_cols(t, tq):
    return pl.ds(pl.multiple_of(t * tq, LANE), tq)


def _mla_attn_kernel(n_chunks, tk, tq, qt_ref, k_ref, vt_ref, o_ref, s_scr):
    def finish(t, outs):
        o_ref[0, :, _q_cols(t, tq)] = outs[0]

    _attn_sweep(n_chunks, tk, tq, qt_ref.shape[2] // tq, lambda t: [qt_ref[0, :, _q_cols(t, tq)]],
                k_ref, vt_ref, s_scr, finish)


def _attn_call(kernel_fn, name, heads, dv, n_ops, qt, k, vt, nq, extra=()):
    b, _, nk = vt.shape
    tq = min(Q_TILE // n_ops, nq)
    tk = _kv_chunk(nk)
    return pl.pallas_call(
        functools.partial(kernel_fn, nk // tk, tk, tq),
        grid=(b, heads),
        in_specs=[
            pl.BlockSpec((1, LANE, nq), lambda bi, h: (bi, h, 0)),
            pl.BlockSpec((1, nk, LANE), lambda bi, h: (bi, 0, h)),
            pl.BlockSpec((1, dv, nk), lambda bi, h: (bi, h, 0)),
        ] + [pl.BlockSpec(a.shape, lambda bi, h: (0, 0)) for a in extra],
        out_specs=pl.BlockSpec((1, dv, nq), lambda bi, h: (bi, h, 0)),
        out_shape=jax.ShapeDtypeStruct((b, heads * dv, nq), jnp.float32),
        scratch_shapes=[pltpu.VMEM((2, n_ops, tk, tq), jnp.float32)],
        compiler_params=_cparams("parallel", "parallel"),
        name=name,
    )(qt, k, vt, *extra)


def _mla_attn(qt, ka, vat, nq):
    return _attn_call(_mla_attn_kernel, "mla_attn", MLA_HEADS, MLA_V, 1, qt, ka, vat, nq)


def _diff_attn_kernel(n_chunks, tk, tq, qt_ref, k_ref, vt_ref, lam_ref, g_ref, o_ref, s_scr):
    row = lax.broadcasted_iota(jnp.int32, (LANE, 1), 0)

    def q_of(t):
        qt = qt_ref[0, :, _q_cols(t, tq)]
        zero = jnp.zeros_like(qt)
        return [jnp.where(row < DIFF_QK, qt, zero), jnp.where(row >= DIFF_QK, qt, zero)]

    def finish(t, outs):
        o = outs[0] - lam_ref[0:1, 0:1] * outs[1]
        o = o * lax.rsqrt(jnp.mean(o * o, axis=0, keepdims=True) + RMS_EPS) * g_ref[...]
        o_ref[0, :, _q_cols(t, tq)] = o * (1.0 - LAMBDA_INIT_0)

    _attn_sweep(n_chunks, tk, tq, qt_ref.shape[2] // tq, q_of, k_ref, vt_ref, s_scr, finish)


def _diff_attn(dqt, dk, dvt, lam, subln_col, nq):
    return _attn_call(_diff_attn_kernel, "diff_attn", DIFF_HEADS, DIFF_V, 2, dqt, dk, dvt, nq,
                      extra=(lam, subln_col))


def _residual_ln(xin, y, gate_vec, ln_g, ln_b):
    r = DEEPNORM_ALPHA * xin + gate_vec * y
    mu = jnp.mean(r, axis=-1, keepdims=True)
    rc = r - mu
    var = jnp.mean(rc * rc, axis=-1, keepdims=True)
    return rc * lax.rsqrt(var + LN_EPS) * ln_g + ln_b


def _even_merge_kernel(nx, x_ref, c_ref, mod_ref, oax_ref, oac_ref, obx_ref, obc_ref, gate_ref,
                       wout_ref, lng_ref, lnb_ref, out_ref):
    i = pl.program_id(1)
    is_ctx = i >= nx
    xin = jnp.where(is_ctx, c_ref[0], x_ref[0])
    oa = jnp.where(is_ctx, oac_ref[0], oax_ref[0]).T
    ob = jnp.where(is_ctx, obc_ref[0], obx_ref[0]).T
    o = jnp.concatenate([oa, ob], axis=1)
    g = gate_ref[0]
    y = _bf16_dot(o * (g * _sigmoid(g)), wout_ref[...])
    out_ref[0] = _residual_ln(xin, y, mod_ref[0, 0][2:3], lng_ref[...], lnb_ref[...])


def _even_merge(x, ctx, modsel, oax, oac, obx, obc, gate, wout, ln_g, ln_b):
    b, n, d = x.shape
    c = ctx.shape[1]
    t = TOKEN_TILE
    nx, nc = n // t, c // t
    wa, wb = oax.shape[1], obx.shape[1]
    full = lambda a: pl.BlockSpec(a.shape, lambda bi, i: (0,) * a.ndim)
    xi = lambda bi, i: (bi, jnp.minimum(i, nx - 1), 0)
    ci = lambda bi, i: (bi, jnp.maximum(i - nx, 0), 0)
    xit = lambda bi, i: (bi, 0, jnp.minimum(i, nx - 1))
    cit = lambda bi, i: (bi, 0, jnp.maximum(i - nx, 0))
    return pl.pallas_call(
        functools.partial(_even_merge_kernel, nx),
        grid=(b, nx + nc),
        in_specs=[
            pl.BlockSpec((1, t, d), xi),
            pl.BlockSpec((1, t, d), ci),
            pl.BlockSpec((1, 1, 3, d), lambda bi, i: (bi, jnp.where(i >= nx, 1, 0), 0, 0)),
            pl.BlockSpec((1, wa, t), xit), pl.BlockSpec((1, wa, t), cit),
            pl.BlockSpec((1, wb, t), xit), pl.BlockSpec((1, wb, t), cit),
            pl.BlockSpec((1, t, wa + wb), lambda bi, i: (bi, i, 0)),
            full(wout), full(ln_g), full(ln_b),
        ],
        out_specs=pl.BlockSpec((1, t, d), lambda bi, i: (bi, i, 0)),
        out_shape=jax.ShapeDtypeStruct((b, n + c, d), jnp.float32),
        compiler_params=_cparams("parallel", "parallel"),
        name="even_merge",
    )(x, ctx, modsel, oax, oac, obx, obc, gate, wout, ln_g, ln_b)


def _odd_proj_kernel(xc_ref, mod_ref, w_ref, u_ref, gate_ref):
    mod = mod_ref[0, 0]
    xm = xc_ref[0] * (1.0 + mod[1:2]) + mod[0:1]
    z = _bf16_dot(xm, w_ref[...])
    w = u_ref.shape[2]
    u_ref[0] = z[:, :w]
    gate_ref[0] = z[:, w:]


def _odd_proj(xc, modsel, w_in, nx):
    b, nt, d = xc.shape
    t = TOKEN_TILE
    w = w_in.shape[1] // 2
    return pl.pallas_call(
        _odd_proj_kernel,
        grid=(b, nt // t),
        in_specs=[
            pl.BlockSpec((1, t, d), lambda bi, i: (bi, i, 0)),
            pl.BlockSpec((1, 1, 3, d), lambda bi, i: (bi, jnp.where(i >= nx, 1, 0), 0, 0)),
            pl.BlockSpec(w_in.shape, lambda bi, i: (0, 0)),
        ],
        out_specs=[pl.BlockSpec((1, t, w), lambda bi, i: (bi, i, 0))] * 2,
        out_shape=[jax.ShapeDtypeStruct((b, nt, w), jnp.float32)] * 2,
        compiler_params=_cparams("parallel", "parallel"),
        name="odd_proj",
    )(xc, modsel, w_in)


def _conv_tile(u, prev, nxt, cw, cb):
    t = u.shape[0]
    row = lax.broadcasted_iota(jnp.int32, (SUBLANE, 1), 0)
    r1, r2, rp = pltpu.roll(u, 1, 0), pltpu.roll(u, 2, 0), pltpu.roll(u, t - 1, 0)
    m1 = jnp.concatenate([jnp.where(row == 0, prev[7:8], r1[:SUBLANE]), r1[SUBLANE:]], 0)
    m2 = jnp.concatenate([jnp.where(row == 0, prev[6:7], jnp.where(row == 1, prev[7:8], r2[:SUBLANE])),
                          r2[SUBLANE:]], 0)
    p1 = jnp.concatenate([rp[:t - SUBLANE], jnp.where(row == SUBLANE - 1, nxt[0:1], rp[t - SUBLANE:])], 0)
    return cw[0:1] * m2 + cw[1:2] * m1 + cw[2:3] * u + cw[3:4] * p1 + cb


def _lru_coeffs(uc, wa_ref, ba, wx_ref, bx, lam, a_ref, b_ref):
    sp = jnp.log1p(jnp.exp(-lam))
    for k in range(LRU_BLOCKS):
        sl = slice(k * LANE, (k + 1) * LANE)
        ub = uc[:, sl]
        r = _sigmoid(_bf16_dot(ub, wa_ref[k]) + ba[:, sl])
        ig = _sigmoid(_bf16_dot(ub, wx_ref[k]) + bx[:, sl])
        log_a = (-LRU_C) * r * sp[:, sl]
        a = jnp.exp(log_a)
        a_ref[:, sl] = a
        b_ref[:, sl] = jnp.sqrt(-jnp.tanh(log_a) * (a * a + 1.0)) * (ig * ub)


def _scan_rows(a_ref, b_ref, h_out_ref, h0, t, reverse):
    def step(s, h):
        r = (t - 1 - s) if reverse else s
        h = a_ref[pl.ds(r, 1), :] * h + b_ref[pl.ds(r, 1), :]
        h_out_ref[pl.ds(r, 1), :] = h
        return h

    return lax.fori_loop(0, t, step, h0, unroll=8)


def _lru_ctx_kernel(u_ref, cw_ref, cb_ref, wa_ref, ba_ref, wx_ref, bx_ref, lam_ref, h_ref,
                    a_s, b_s, hs):
    u = u_ref[0]
    zero8 = jnp.zeros((SUBLANE, u.shape[1]), jnp.float32)
    uc = _conv_tile(u, zero8, zero8, cw_ref[...], cb_ref[...])
    t = u.shape[0]
    for d in range(2):
        _lru_coeffs(uc, wa_ref.at[d], ba_ref[d], wx_ref.at[d], bx_ref[d], lam_ref[d], a_s, b_s)
        h = _scan_rows(a_s, b_s, hs, jnp.zeros((1, u.shape[1]), jnp.float32), t, reverse=(d == 1))
        h_ref[0, d:d + 1, :] = h


def _lru_ctx(u_all, n, c, cw, cb, wa, ba, wx, bx, lam):
    b, _, w = u_all.shape
    full = lambda a: pl.BlockSpec(a.shape, lambda bi: (0,) * a.ndim)
    return pl.pallas_call(
        _lru_ctx_kernel,
        grid=(b,),
        in_specs=[pl.BlockSpec((1, c, w), lambda bi: (bi, n // c, 0)),
                  full(cw), full(cb), full(wa), full(ba), full(wx), full(bx), full(lam)],
        out_specs=pl.BlockSpec((1, 2, w), lambda bi: (bi, 0, 0)),
        out_shape=jax.ShapeDtypeStruct((b, 2, w), jnp.float32),
        scratch_shapes=[pltpu.VMEM((c, w), jnp.float32)] * 3,
        compiler_params=_cparams("parallel"),
        name="lru_ctx",
    )(u_all, cw, cb, wa, ba, wx, bx, lam)


def _lru_dir_kernel(direction, n_tiles, *refs):
    reverse = direction == 1
    if reverse:
        (u_ref, up_ref, un_ref, h0_ref, cw_ref, cb_ref, wa_ref, ba_ref, wx_ref, bx_ref, lam_ref,
         hf_ref, gate_ref, x_ref, mod_ref, wout_ref, lng_ref, lnb_ref, out_ref, a_s, b_s, hs, carry) = refs
    else:
        (u_ref, up_ref, un_ref, h0_ref, cw_ref, cb_ref, wa_ref, ba_ref, wx_ref, bx_ref, lam_ref,
         out_ref, a_s, b_s, carry) = refs
        hs = out_ref.at[0]
    step = pl.program_id(1)
    tile = (n_tiles - 1 - step) if reverse else step

    @pl.when(step == 0)
    def _():
        carry[...] = h0_ref[0, direction:direction + 1, :]

    u = u_ref[0]
    t = u.shape[0]
    prev = jnp.where(tile == 0, 0.0, up_ref[0])
    nxt = jnp.where(tile == n_tiles - 1, 0.0, un_ref[0])
    uc = _conv_tile(u, prev, nxt, cw_ref[...], cb_ref[...])
    _lru_coeffs(uc, wa_ref.at[direction], ba_ref[direction], wx_ref.at[direction], bx_ref[direction],
                lam_ref[direction], a_s, b_s)
    carry[...] = _scan_rows(a_s, b_s, hs, carry[...], t, reverse)

    if reverse:
        g = gate_ref[0]
        hx = hf_ref[0] + hs[...]
        y = _bf16_dot(hx * (g * _sigmoid(g)), wout_ref[...])
        out_ref[0] = _residual_ln(x_ref[0], y, mod_ref[0, 0][2:3], lng_ref[...], lnb_ref[...])


def _lru_dir(direction, u_all, n, h0, cw, cb, wa, ba, wx, bx, lam, extra=None):
    b, _, w = u_all.shape
    t = min(SCAN_TILE, n)
    n_tiles = n // t
    r = t // SUBLANE
    if direction == 1:
        tidx = lambda s: n_tiles - 1 - s
    else:
        tidx = lambda s: s
    full = lambda a: pl.BlockSpec(a.shape, lambda bi, s: (0,) * a.ndim)
    tile_spec = lambda width: pl.BlockSpec((1, t, width), lambda bi, s: (bi, tidx(s), 0))
    in_specs = [
        tile_spec(w),
        pl.BlockSpec((1, SUBLANE, w), lambda bi, s: (bi, jnp.maximum(tidx(s) * r - 1, 0), 0)),
        pl.BlockSpec((1, SUBLANE, w), lambda bi, s: (bi, (tidx(s) + 1) * r, 0)),
        pl.BlockSpec((1, 2, w), lambda bi, s: (bi, 0, 0)),
        full(cw), full(cb), full(wa), full(ba), full(wx), full(bx), full(lam),
    ]
    args = [u_all, u_all, u_all, h0, cw, cb, wa, ba, wx, bx, lam]
    scratch = [pltpu.VMEM((t, w), jnp.float32), pltpu.VMEM((t, w), jnp.float32)]
    if direction == 1:
        hf, gate, xc, modsel, wout, ln_g, ln_b = extra
        d = xc.shape[2]
        in_specs += [tile_spec(w), tile_spec(w), tile_spec(d),
                     pl.BlockSpec((1, 1, 3, d), lambda bi, s: (bi, 0, 0, 0)),
                     full(wout), full(ln_g), full(ln_b)]
        args += [hf, gate, xc, modsel, wout, ln_g, ln_b]
        scratch.append(pltpu.VMEM((t, w), jnp.float32))
        out_w = d
    else:
        out_w = w
    scratch.append(pltpu.VMEM((1, w), jnp.float32))
    return pl.pallas_call(
        functools.partial(_lru_dir_kernel, direction, n_tiles),
        grid=(b, n_tiles),
        in_specs=in_specs,
        out_specs=pl.BlockSpec((1, t, out_w), lambda bi, s: (bi, tidx(s), 0)),
        out_shape=jax.ShapeDtypeStruct((b, n, out_w), jnp.float32),
        scratch_shapes=scratch,
        compiler_params=_cparams("parallel", "arbitrary"),
        name="lru_bwd_out" if direction == 1 else "lru_fwd",
    )(*args)


def _rope_tables(n, c):
    t = jnp.arange(n)
    rows = (t // GRID_W).astype(jnp.float32)
    cols = (t % GRID_W).astype(jnp.float32)

    def ang(rot_dim):
        n_freq = rot_dim // 4
        freqs = ROPE_THETA ** (-jnp.arange(n_freq, dtype=jnp.float32) / n_freq)
        return jnp.concatenate([rows[:, None] * freqs, cols[:, None] * freqs], -1)

    aa, ab = ang(MLA_ROPE), ang(DIFF_QK)
    one = jnp.ones((n, MLA_NOPE), jnp.float32)
    zero = jnp.zeros((n, MLA_NOPE), jnp.float32)
    pad1 = jnp.ones((n, LANE - MLA_QK), jnp.float32)
    pad0 = jnp.zeros((n, LANE - MLA_QK), jnp.float32)
    ca = jnp.concatenate([one, jnp.cos(aa), jnp.cos(aa), pad1], 1)
    sa = jnp.concatenate([zero, -jnp.sin(aa), jnp.sin(aa), pad0], 1)
    cb = jnp.concatenate([jnp.cos(ab), jnp.cos(ab)] * 2, 1)
    sb = jnp.concatenate([-jnp.sin(ab), jnp.sin(ab)] * 2, 1)
    ident = lambda tab, v: jnp.concatenate([tab, jnp.full((c, LANE), v, jnp.float32)], 0)
    return ident(ca, 1.0), ident(sa, 0.0), ident(cb, 1.0), ident(sb, 0.0)


def _even_weights(w_in, w_uq, w_ukv):
    d = w_in.shape[0]
    bf = jnp.bfloat16
    o = [0, 384, 640, 672, 1184, 1696, 2208, 3232]
    cq, ckv, kr, dq, dk, dv, gate = (w_in[:, o[i]:o[i + 1]] for i in range(7))
    kr_group = jnp.concatenate([jnp.zeros((d, MLA_NOPE), w_in.dtype), kr,
                                jnp.zeros((d, LANE - MLA_QK), w_in.dtype)], 1)
    w1 = jnp.concatenate([cq, ckv, kr_group, dq, dk, dv, gate], 1).astype(bf)
    wuq = jnp.pad(w_uq.reshape(MLA_Q_LORA, MLA_HEADS, MLA_QK),
                  ((0, 0), (0, 0), (0, LANE - MLA_QK))).reshape(MLA_Q_LORA, MLA_HEADS * LANE).astype(bf)
    ukv = w_ukv.reshape(MLA_KV_LORA, MLA_HEADS, MLA_NOPE + MLA_V)
    wuk = jnp.pad(ukv[..., :MLA_NOPE], ((0, 0), (0, 0), (0, LANE - MLA_NOPE))).reshape(MLA_KV_LORA, MLA_HEADS * LANE)
    wuv = ukv[..., MLA_NOPE:].reshape(MLA_KV_LORA, MLA_HEADS * MLA_V)
    wkv = jnp.concatenate([wuk, wuv], 1).astype(bf)
    return w1, wuq, wkv


def _mod_select(mods_l, b):
    d = mods_l.shape[1] // 3
    mx = mods_l[:b].reshape(b, 3, d)
    mc = jnp.broadcast_to(mods_l[b].reshape(1, 3, d), (b, 3, d))
    return jnp.stack([mx, mc], axis=1)


def kernel(x, c, ctx, c_ctx, ada_w, ada_b, post_ln_g, post_ln_b, e_w_in, e_q_norm_g, e_w_uq, e_kv_norm_g, e_w_ukv, e_lam_q1, e_lam_k1, e_lam_q2, e_lam_k2, e_subln_g, e_w_out, o_w_in, o_conv_w, o_conv_b, o_gate_a_w, o_gate_a_b, o_gate_x_w, o_gate_x_b, o_lru_lambda, o_w_out):
    b, n, d = x.shape
    cl = ctx.shape[1]
    assert ada_w.shape[0] == DEPTH and b < SUBLANE
    assert n % Q_TILE == 0 and cl % TOKEN_TILE == 0 and n % cl == 0
    bf = jnp.bfloat16

    cond = jnp.concatenate([c, c_ctx[None], jnp.zeros((SUBLANE - b - 1, d), c.dtype)], 0)
    lamv = jnp.concatenate([e_lam_q1, e_lam_k1, e_lam_q2, e_lam_k2], 0)
    mods, lam = _adaln(cond, ada_w, ada_b, lamv)
    mod0, mod1 = _mod_select(mods[0], b), _mod_select(mods[1], b)

    w1, wuq, wkv = _even_weights(e_w_in[0], e_w_uq[0], e_w_ukv[0])
    tabs = _rope_tables(n, cl)
    qt, ka, vat, dqt, dk, dvt, gate = _even_proj(
        x, ctx, mod0, w1, e_q_norm_g[0][None], wuq, e_kv_norm_g[0][None], wkv, tabs)
    subln_col = e_subln_g[0][:, None]
    oax = _mla_attn(qt, ka, vat, n)
    obx = _diff_attn(dqt, dk, dvt, lam, subln_col, n)
    oac = _mla_attn(qt[:, :, n:], ka[:, n:], vat[:, :, n:], cl)
    obc = _diff_attn(dqt[:, :, n:], dk[:, n:], dvt[:, :, n:], lam, subln_col, cl)
    xc = _even_merge(x, ctx, mod0, oax, oac, obx, obc, gate, e_w_out[0].astype(bf),
                     post_ln_g[0][None], post_ln_b[0][None])

    u_all, gate1 = _odd_proj(xc, mod1, o_w_in[0].astype(bf), n // TOKEN_TILE)
    cw, cb = o_conv_w[0], o_conv_b[0][None]
    wa, wx = o_gate_a_w[0].astype(bf), o_gate_x_w[0].astype(bf)
    w = u_all.shape[2]
    ba = o_gate_a_b[0].reshape(2, 1, w)
    bx = o_gate_x_b[0].reshape(2, 1, w)
    lru_lam = o_lru_lambda[0].reshape(2, 1, w)
    h0 = _lru_ctx(u_all, n, cl, cw, cb, wa, ba, wx, bx, lru_lam)
    hf = _lru_dir(0, u_all, n, h0, cw, cb, wa, ba, wx, bx, lru_lam)
    return _lru_dir(1, u_all, n, h0, cw, cb, wa, ba, wx, bx, lru_lam,
                    extra=(hf, gate1, xc, mod1, o_w_out[0].astype(bf), post_ln_g[1][None], post_ln_b[1][None]))
```

```python
import functools
import math

import jax
import jax.numpy as jnp
from jax import lax
from jax.experimental import pallas as pl
from jax.experimental.pallas import tpu as pltpu

GRID_W = 64
ROPE_THETA = 10000.0
LN_EPS = 1e-6
RMS_EPS = 1e-6

MLA_HEADS = 8
MLA_Q_LORA = 384
MLA_KV_LORA = 256
MLA_NOPE = 64
MLA_ROPE = 32
MLA_V = 64
MLA_QK = MLA_NOPE + MLA_ROPE
MLA_SCALE = MLA_QK ** -0.5

DIFF_HEADS = 4
DIFF_QK = 64
DIFF_V = 2 * DIFF_QK
DIFF_SCALE = DIFF_QK ** -0.5

LRU_BLOCKS = 8
LRU_C = 8.0
CONV_W = 4

DEPTH = 2
DEEPNORM_ALPHA = (2 * DEPTH) ** 0.25
LAMBDA_INIT_0 = 0.8 - 0.6 * math.exp(-0.3 * 0)

LANE = 128
SUBLANE = 8
MXU_DEPTH = 256
LOG2E = 1.4426950408889634
NEG_BIG = -1e30

TOKEN_TILE = 256
Q_TILE = 512
KV_CHUNK = 1408
SCAN_TILE = 512
VMEM_LIMIT = 56 * 1024 * 1024

_G_CQ = (0, 384)
_G_CKV = (384, 640)
_G_KR = (640, 768)
_G_DQ = (768, 1280)
_G_DK = (1280, 1792)
_G_DV = (1792, 2304)
_G_GATE = (2304, 3328)
_EVEN_W = 3328


def _cparams(*sem):
    return pltpu.CompilerParams(dimension_semantics=sem, vmem_limit_bytes=VMEM_LIMIT)


def _bf16_dot(a, b):
    return jnp.dot(a.astype(jnp.bfloat16), b.astype(jnp.bfloat16), preferred_element_type=jnp.float32)


def _sigmoid(x):
    return 1.0 / (1.0 + jnp.exp(-x))


def _adaln_kernel(cond_ref, w_ref, b_ref, lamv_ref, mod_ref, lam_ref):
    cond = cond_ref[...]
    h = cond * _sigmoid(cond)
    mod_ref[0] = jnp.dot(h, w_ref[0], preferred_element_type=jnp.float32,
                         precision=lax.Precision.HIGHEST) + b_ref[0]
    lv = lamv_ref[...]
    d1 = jnp.sum(lv[0:1] * lv[1:2], axis=-1, keepdims=True)
    d2 = jnp.sum(lv[2:3] * lv[3:4], axis=-1, keepdims=True)
    lam = jnp.exp(d1) - jnp.exp(d2) + LAMBDA_INIT_0
    lam_ref[...] = jnp.broadcast_to(lam, lam_ref.shape)


def _adaln(cond, ada_w, ada_b, lamv):
    depth, d, d3 = ada_w.shape
    nj = d3 // d
    return pl.pallas_call(
        _adaln_kernel,
        grid=(depth, nj),
        in_specs=[
            pl.BlockSpec((SUBLANE, d), lambda l, j: (0, 0)),
            pl.BlockSpec((1, d, d), lambda l, j: (l, 0, j)),
            pl.BlockSpec((1, 1, d), lambda l, j: (l, 0, j)),
            pl.BlockSpec(lamv.shape, lambda l, j: (0, 0)),
        ],
        out_specs=[
            pl.BlockSpec((1, SUBLANE, d), lambda l, j: (l, 0, j)),
            pl.BlockSpec((SUBLANE, LANE), lambda l, j: (0, 0)),
        ],
        out_shape=[
            jax.ShapeDtypeStruct((depth, SUBLANE, d3), jnp.float32),
            jax.ShapeDtypeStruct((SUBLANE, LANE), jnp.float32),
        ],
        compiler_params=_cparams("arbitrary", "arbitrary"),
        name="adaln",
    )(cond, ada_w, ada_b.reshape(depth, 1, d3), lamv)


def _rms(x, g):
    return x * lax.rsqrt(jnp.mean(x * x, axis=-1, keepdims=True) + RMS_EPS) * g


def _rope_group(x, cos, sin, half, first_half_mask):
    partner = jnp.where(first_half_mask, pltpu.roll(x, LANE - half, 1), pltpu.roll(x, half, 1))
    return x * cos + partner * sin


def _even_proj_kernel(nx, x_ref, c_ref, mod_ref, w1_ref, qg_ref, wuq_ref, kvg_ref, wkv_ref,
                      ca_ref, sa_ref, cb_ref, sb_ref,
                      qt_ref, ka_ref, vat_ref, dqt_ref, dk_ref, dvt_ref, gate_ref):
    i = pl.program_id(1)
    xin = jnp.where(i >= nx, c_ref[0], x_ref[0])
    mod = mod_ref[0, 0]
    xm = (xin * (1.0 + mod[1:2]) + mod[0:1]).astype(w1_ref.dtype)

    def zcols(lo, hi):
        return jnp.dot(xm, w1_ref[:, lo:hi], preferred_element_type=jnp.float32)

    lane = lax.broadcasted_iota(jnp.int32, (1, LANE), 1)
    mla_first = jnp.logical_and(lane >= MLA_NOPE, lane < MLA_NOPE + MLA_ROPE // 2)
    diff_first = (lane % DIFF_QK) < DIFF_QK // 2
    ca, sa, cb, sb = ca_ref[...], sa_ref[...], cb_ref[...], sb_ref[...]

    z_lat = zcols(_G_CQ[0], _G_KR[1])
    z_dq = zcols(*_G_DQ)
    z_dk = zcols(*_G_DK)

    cqn = _rms(z_lat[:, _G_CQ[0]:_G_CQ[1]], qg_ref[...])
    q = _bf16_dot(cqn, wuq_ref[...])
    ckvn = _rms(z_lat[:, _G_CKV[0]:_G_CKV[1]], kvg_ref[...])
    kv = _bf16_dot(ckvn, wkv_ref[...])
    z_dv = zcols(*_G_DV)
    gate_ref[0] = zcols(*_G_GATE)

    q_heads = []
    for h in range(MLA_HEADS):
        qh = _rope_group(q[:, h * LANE:(h + 1) * LANE], ca, sa, MLA_ROPE // 2, mla_first)
        q_heads.append(qh * (MLA_SCALE * LOG2E))
    qt_ref[0] = jnp.concatenate(q_heads, axis=1).T.astype(qt_ref.dtype)

    kr = _rope_group(z_lat[:, _G_KR[0]:_G_KR[1]], ca, sa, MLA_ROPE // 2, mla_first)
    for h in range(MLA_HEADS):
        ka_ref[0, :, h * LANE:(h + 1) * LANE] = (kv[:, h * LANE:(h + 1) * LANE] + kr).astype(ka_ref.dtype)
    vat_ref[0] = kv[:, MLA_HEADS * LANE:].T.astype(vat_ref.dtype)

    dq_heads = []
    for h in range(DIFF_HEADS):
        sl = slice(h * LANE, (h + 1) * LANE)
        dq_heads.append(_rope_group(z_dq[:, sl], cb, sb, DIFF_QK // 2, diff_first) * (DIFF_SCALE * LOG2E))
        dk_ref[0, :, sl] = _rope_group(z_dk[:, sl], cb, sb, DIFF_QK // 2, diff_first).astype(dk_ref.dtype)
    dqt_ref[0] = jnp.concatenate(dq_heads, axis=1).T.astype(dqt_ref.dtype)
    dvt_ref[0] = z_dv.T.astype(dvt_ref.dtype)


def _even_proj(x, ctx, modsel, w1, qg, wuq, kvg, wkv, tabs):
    b, n, d = x.shape
    c = ctx.shape[1]
    t = TOKEN_TILE
    nx, nc = n // t, c // t
    nt = n + c
    full = lambda a: pl.BlockSpec(a.shape, lambda bi, i: (0,) * a.ndim)
    tab_spec = pl.BlockSpec((t, LANE), lambda bi, i: (i, 0))
    tok = lambda w: pl.BlockSpec((1, t, w), lambda bi, i: (bi, i, 0))
    tok_t = lambda w: pl.BlockSpec((1, w, t), lambda bi, i: (bi, 0, i))
    bf = jnp.bfloat16
    return pl.pallas_call(
        functools.partial(_even_proj_kernel, nx),
        grid=(b, nx + nc),
        in_specs=[
            pl.BlockSpec((1, t, d), lambda bi, i: (bi, jnp.minimum(i, nx - 1), 0)),
            pl.BlockSpec((1, t, d), lambda bi, i: (bi, jnp.maximum(i - nx, 0), 0)),
            pl.BlockSpec((1, 1, 3, d), lambda bi, i: (bi, jnp.where(i >= nx, 1, 0), 0, 0)),
            full(w1), full(qg), full(wuq), full(kvg), full(wkv),
            tab_spec, tab_spec, tab_spec, tab_spec,
        ],
        out_specs=[
            tok_t(MLA_HEADS * LANE), tok(MLA_HEADS * LANE), tok_t(MLA_HEADS * MLA_V),
            tok_t(DIFF_HEADS * LANE), tok(DIFF_HEADS * LANE), tok_t(DIFF_HEADS * DIFF_V),
            tok(MLA_HEADS * MLA_V + DIFF_HEADS * DIFF_V),
        ],
        out_shape=[
            jax.ShapeDtypeStruct((b, MLA_HEADS * LANE, nt), bf),
            jax.ShapeDtypeStruct((b, nt, MLA_HEADS * LANE), bf),
            jax.ShapeDtypeStruct((b, MLA_HEADS * MLA_V, nt), bf),
            jax.ShapeDtypeStruct((b, DIFF_HEADS * LANE, nt), bf),
            jax.ShapeDtypeStruct((b, nt, DIFF_HEADS * LANE), bf),
            jax.ShapeDtypeStruct((b, DIFF_HEADS * DIFF_V, nt), bf),
            jax.ShapeDtypeStruct((b, nt, MLA_HEADS * MLA_V + DIFF_HEADS * DIFF_V), jnp.float32),
        ],
        compiler_params=_cparams("parallel", "parallel"),
        name="even_proj",
    )(x, ctx, modsel, w1, qg, wuq, kvg, wkv, *tabs)


def _kv_chunk(nk):
    if nk <= KV_CHUNK:
        return nk
    for m in range(KV_CHUNK // LANE, 0, -1):
        if nk % (m * LANE) == 0 and (nk // (m * LANE)) % 2 == 0:
            return m * LANE
    raise ValueError(f"no even chunking of {nk} keys")


def _attn_sweep(n_chunks, tk, tq, n_qt, q_of, k_ref, vt_ref, s_scr, finish):
    dv = vt_ref.shape[1]

    def scores(t, j, slot):
        k = k_ref[0, pl.ds(pl.multiple_of(j * tk, LANE), tk), :]
        cms = []
        for si, qt in enumerate(q_of(t)):
            s = jnp.dot(k, qt, preferred_element_type=jnp.float32)
            s_scr[slot, si] = s
            cms.append(jnp.max(s, axis=0, keepdims=True))
        return tuple(cms)

    n_ops = s_scr.shape[1]
    ones = jnp.ones((2 * SUBLANE, MXU_DEPTH), vt_ref.dtype)
    init = (jnp.full((1, tq), NEG_BIG, jnp.float32), jnp.zeros((dv + 2 * SUBLANE, tq), jnp.float32)) * n_ops
    blocks = [(r0, min(MXU_DEPTH, tk - r0)) for r0 in range(0, tk, MXU_DEPTH)]

    def step(nxt, j_cur, slot_cur, cms, state):
        off_cur = pl.multiple_of(j_cur * tk, LANE)
        if nxt is not None:
            t_nxt, j_nxt, slot_nxt = nxt
            off_nxt = pl.multiple_of(j_nxt * tk, LANE)
            q_nxt = q_of(t_nxt)
        m_new = [jnp.maximum(state[2 * si], cms[si]) for si in range(n_ops)]
        cm_nxt, pv = [None] * n_ops, [None] * n_ops
        for r0, r in blocks:
            if nxt is not None:
                k = k_ref[0, pl.ds(off_nxt + r0, r), :]
            vt = vt_ref[0, :, pl.ds(off_cur + r0, r)]
            vt_ext = jnp.concatenate([vt, ones[:, :r]], axis=0)
            for si in range(n_ops):
                if nxt is not None:
                    s = jnp.dot(k, q_nxt[si], preferred_element_type=jnp.float32)
                    s_scr[slot_nxt, si, r0:r0 + r] = s
                    cmax = jnp.max(s, axis=0, keepdims=True)
                    cm_nxt[si] = cmax if cm_nxt[si] is None else jnp.maximum(cm_nxt[si], cmax)
                p = jnp.exp2(s_scr[slot_cur, si, r0:r0 + r] - m_new[si]).astype(vt.dtype)
                d = jnp.dot(vt_ext, p, preferred_element_type=jnp.float32)
                pv[si] = d if pv[si] is None else pv[si] + d
        out = []
        for si in range(n_ops):
            alpha = jnp.exp2(state[2 * si] - m_new[si])
            out += [m_new[si], alpha * state[2 * si + 1] + pv[si]]
        return tuple(cm_nxt), tuple(out)

    def result(st):
        return [st[2 * si + 1][:dv] / st[2 * si + 1][dv:dv + 1] for si in range(n_ops)]

    if n_chunks == 1:
        def tile(t, carry):
            finish(t, result(step(None, 0, 0, scores(t, 0, 0), init)[1]))
            return carry
        lax.fori_loop(0, n_qt, tile, 0)
        return

    assert n_chunks % 2 == 0

    def tile(t, cm):
        def pair(i, carry):
            cm_a, st = carry
            cm_b, st = step((t, 2 * i + 1, 1), 2 * i, 0, cm_a, st)
            cm_a, st = step((t, 2 * i + 2, 0), 2 * i + 1, 1, cm_b, st)
            return cm_a, st

        cm_a, st = lax.fori_loop(0, n_chunks // 2 - 1, pair, (cm, init))
        cm_b, st = step((t, n_chunks - 1, 1), n_chunks - 2, 0, cm_a, st)
        cm_next, st = step((jnp.minimum(t + 1, n_qt - 1), 0, 0), n_chunks - 1, 1, cm_b, st)
        finish(t, result(st))
        return cm_next

    lax.fori_loop(0, n_qt, tile, scores(0, 0, 0))


def _q_cols(t, tq):
    return pl.ds(pl.multiple_of(t * tq, LANE), tq)


def _mla_attn_kernel(n_chunks, tk, tq, qt_ref, k_ref, vt_ref, o_ref, s_scr):
    n_ops = s_scr.shape[1]

    def q_of(t):
        return [qt_ref[0, :, _q_cols(t * n_ops + i, tq)] for i in range(n_ops)]

    def finish(t, outs):
        for i in range(n_ops):
            o_ref[0, :, _q_cols(t * n_ops + i, tq)] = outs[i]

    _attn_sweep(n_chunks, tk, tq, qt_ref.shape[2] // (tq * n_ops), q_of, k_ref, vt_ref, s_scr, finish)


def _attn_call(kernel_fn, name, heads, dv, n_ops, qt, k, vt, nq, extra=()):
    b, _, nk = vt.shape
    tq = min(Q_TILE, nq)
    tk = _kv_chunk(nk)
    return pl.pallas_call(
        functools.partial(kernel_fn, nk // tk, tk, tq),
        grid=(b, heads),
        in_specs=[
            pl.BlockSpec((1, LANE, nq), lambda bi, h: (bi, h, 0)),
            pl.BlockSpec((1, nk, LANE), lambda bi, h: (bi, 0, h)),
            pl.BlockSpec((1, dv, nk), lambda bi, h: (bi, h, 0)),
        ] + [pl.BlockSpec(a.shape, lambda bi, h: (0, 0)) for a in extra],
        out_specs=pl.BlockSpec((1, dv, nq), lambda bi, h: (bi, h, 0)),
        out_shape=jax.ShapeDtypeStruct((b, heads * dv, nq), jnp.float32),
        scratch_shapes=[pltpu.VMEM((2, n_ops, tk, tq), jnp.float32)],
        compiler_params=_cparams("parallel", "parallel"),
        name=name,
    )(qt, k, vt, *extra)


def _mla_attn(qt, ka, vat, nq):
    n_ops = 2 if nq % (2 * Q_TILE) == 0 else 1
    return _attn_call(_mla_attn_kernel, "mla_attn", MLA_HEADS, MLA_V, n_ops, qt, ka, vat, nq)


def _diff_attn_kernel(n_chunks, tk, tq, qt_ref, k_ref, vt_ref, lam_ref, g_ref, o_ref, s_scr):
    row = lax.broadcasted_iota(jnp.int32, (LANE, 1), 0)

    def q_of(t):
        qt = qt_ref[0, :, _q_cols(t, tq)]
        zero = jnp.zeros_like(qt)
        return [jnp.where(row < DIFF_QK, qt, zero), jnp.where(row >= DIFF_QK, qt, zero)]

    def finish(t, outs):
        o = outs[0] - lam_ref[0:1, 0:1] * outs[1]
        o = o * lax.rsqrt(jnp.mean(o * o, axis=0, keepdims=True) + RMS_EPS) * g_ref[...]
        o_ref[0, :, _q_cols(t, tq)] = o * (1.0 - LAMBDA_INIT_0)

    _attn_sweep(n_chunks, tk, tq, qt_ref.shape[2] // tq, q_of, k_ref, vt_ref, s_scr, finish)


def _diff_attn(dqt, dk, dvt, lam, subln_col, nq):
    return _attn_call(_diff_attn_kernel, "diff_attn", DIFF_HEADS, DIFF_V, 2, dqt, dk, dvt, nq,
                      extra=(lam, subln_col))


def _residual_ln(xin, y, gate_vec, ln_g, ln_b):
    r = DEEPNORM_ALPHA * xin + gate_vec * y
    mu = jnp.mean(r, axis=-1, keepdims=True)
    rc = r - mu
    var = jnp.mean(rc * rc, axis=-1, keepdims=True)
    return rc * lax.rsqrt(var + LN_EPS) * ln_g + ln_b


def _even_merge_kernel(nx, x_ref, c_ref, mod_ref, oax_ref, oac_ref, obx_ref, obc_ref, gate_ref,
                       wout_ref, lng_ref, lnb_ref, out_ref):
    i = pl.program_id(1)
    is_ctx = i >= nx
    xin = jnp.where(is_ctx, c_ref[0], x_ref[0])
    oa = jnp.where(is_ctx, oac_ref[0], oax_ref[0]).T
    ob = jnp.where(is_ctx, obc_ref[0], obx_ref[0]).T
    o = jnp.concatenate([oa, ob], axis=1)
    g = gate_ref[0]
    y = _bf16_dot(o * (g * _sigmoid(g)), wout_ref[...])
    out_ref[0] = _residual_ln(xin, y, mod_ref[0, 0][2:3], lng_ref[...], lnb_ref[...])


def _even_merge(x, ctx, modsel, oax, oac, obx, obc, gate, wout, ln_g, ln_b):
    b, n, d = x.shape
    c = ctx.shape[1]
    t = TOKEN_TILE
    nx, nc = n // t, c // t
    wa, wb = oax.shape[1], obx.shape[1]
    full = lambda a: pl.BlockSpec(a.shape, lambda bi, i: (0,) * a.ndim)
    xi = lambda bi, i: (bi, jnp.minimum(i, nx - 1), 0)
    ci = lambda bi, i: (bi, jnp.maximum(i - nx, 0), 0)
    xit = lambda bi, i: (bi, 0, jnp.minimum(i, nx - 1))
    cit = lambda bi, i: (bi, 0, jnp.maximum(i - nx, 0))
    return pl.pallas_call(
        functools.partial(_even_merge_kernel, nx),
        grid=(b, nx + nc),
        in_specs=[
            pl.BlockSpec((1, t, d), xi),
            pl.BlockSpec((1, t, d), ci),
            pl.BlockSpec((1, 1, 3, d), lambda bi, i: (bi, jnp.where(i >= nx, 1, 0), 0, 0)),
            pl.BlockSpec((1, wa, t), xit), pl.BlockSpec((1, wa, t), cit),
            pl.BlockSpec((1, wb, t), xit), pl.BlockSpec((1, wb, t), cit),
            pl.BlockSpec((1, t, wa + wb), lambda bi, i: (bi, i, 0)),
            full(wout), full(ln_g), full(ln_b),
        ],
        out_specs=pl.BlockSpec((1, t, d), lambda bi, i: (bi, i, 0)),
        out_shape=jax.ShapeDtypeStruct((b, n + c, d), jnp.float32),
        compiler_params=_cparams("parallel", "parallel"),
        name="even_merge",
    )(x, ctx, modsel, oax, oac, obx, obc, gate, wout, ln_g, ln_b)


def _odd_proj_kernel(xc_ref, mod_ref, w_ref, u_ref, gate_ref):
    mod = mod_ref[0, 0]
    xm = xc_ref[0] * (1.0 + mod[1:2]) + mod[0:1]
    z = _bf16_dot(xm, w_ref[...])
    w = u_ref.shape[2]
    u_ref[0] = z[:, :w]
    gate_ref[0] = z[:, w:]


def _odd_proj(xc, modsel, w_in, nx):
    b, nt, d = xc.shape
    t = TOKEN_TILE
    w = w_in.shape[1] // 2
    return pl.pallas_call(
        _odd_proj_kernel,
        grid=(b, nt // t),
        in_specs=[
            pl.BlockSpec((1, t, d), lambda bi, i: (bi, i, 0)),
            pl.BlockSpec((1, 1, 3, d), lambda bi, i: (bi, jnp.where(i >= nx, 1, 0), 0, 0)),
            pl.BlockSpec(w_in.shape, lambda bi, i: (0, 0)),
        ],
        out_specs=[pl.BlockSpec((1, t, w), lambda bi, i: (bi, i, 0))] * 2,
        out_shape=[jax.ShapeDtypeStruct((b, nt, w), jnp.float32)] * 2,
        compiler_params=_cparams("parallel", "parallel"),
        name="odd_proj",
    )(xc, modsel, w_in)


def _conv_tile(u, prev, nxt, cw, cb):
    t = u.shape[0]
    row = lax.broadcasted_iota(jnp.int32, (SUBLANE, 1), 0)
    r1, r2, rp = pltpu.roll(u, 1, 0), pltpu.roll(u, 2, 0), pltpu.roll(u, t - 1, 0)
    m1 = jnp.concatenate([jnp.where(row == 0, prev[7:8], r1[:SUBLANE]), r1[SUBLANE:]], 0)
    m2 = jnp.concatenate([jnp.where(row == 0, prev[6:7], jnp.where(row == 1, prev[7:8], r2[:SUBLANE])),
                          r2[SUBLANE:]], 0)
    p1 = jnp.concatenate([rp[:t - SUBLANE], jnp.where(row == SUBLANE - 1, nxt[0:1], rp[t - SUBLANE:])], 0)
    return cw[0:1] * m2 + cw[1:2] * m1 + cw[2:3] * u + cw[3:4] * p1 + cb


def _lru_coeffs(uc, wa_ref, ba, wx_ref, bx, lam, a_ref, b_ref):
    sp = jnp.log1p(jnp.exp(-lam))
    for k in range(LRU_BLOCKS):
        sl = slice(k * LANE, (k + 1) * LANE)
        ub = uc[:, sl]
        r = _sigmoid(_bf16_dot(ub, wa_ref[k]) + ba[:, sl])
        ig = _sigmoid(_bf16_dot(ub, wx_ref[k]) + bx[:, sl])
        log_a = (-LRU_C) * r * sp[:, sl]
        a = jnp.exp(log_a)
        a_ref[:, sl] = a
        b_ref[:, sl] = jnp.sqrt(-jnp.tanh(log_a) * (a * a + 1.0)) * (ig * ub)


def _scan_rows(a_ref, b_ref, h_out_ref, h0, t, reverse):
    def step(s, h):
        r = (t - 1 - s) if reverse else s
        h = a_ref[pl.ds(r, 1), :] * h + b_ref[pl.ds(r, 1), :]
        h_out_ref[pl.ds(r, 1), :] = h
        return h

    return lax.fori_loop(0, t, step, h0, unroll=8)


def _lru_ctx_kernel(u_ref, cw_ref, cb_ref, wa_ref, ba_ref, wx_ref, bx_ref, lam_ref, h_ref,
                    a_s, b_s, hs):
    u = u_ref[0]
    zero8 = jnp.zeros((SUBLANE, u.shape[1]), jnp.float32)
    uc = _conv_tile(u, zero8, zero8, cw_ref[...], cb_ref[...])
    t = u.shape[0]
    for d in range(2):
        _lru_coeffs(uc, wa_ref.at[d], ba_ref[d], wx_ref.at[d], bx_ref[d], lam_ref[d], a_s, b_s)
        h = _scan_rows(a_s, b_s, hs, jnp.zeros((1, u.shape[1]), jnp.float32), t, reverse=(d == 1))
        h_ref[0, d:d + 1, :] = h


def _lru_ctx(u_all, n, c, cw, cb, wa, ba, wx, bx, lam):
    b, _, w = u_all.shape
    full = lambda a: pl.BlockSpec(a.shape, lambda bi: (0,) * a.ndim)
    return pl.pallas_call(
        _lru_ctx_kernel,
        grid=(b,),
        in_specs=[pl.BlockSpec((1, c, w), lambda bi: (bi, n // c, 0)),
                  full(cw), full(cb), full(wa), full(ba), full(wx), full(bx), full(lam)],
        out_specs=pl.BlockSpec((1, 2, w), lambda bi: (bi, 0, 0)),
        out_shape=jax.ShapeDtypeStruct((b, 2, w), jnp.float32),
        scratch_shapes=[pltpu.VMEM((c, w), jnp.float32)] * 3,
        compiler_params=_cparams("parallel"),
        name="lru_ctx",
    )(u_all, cw, cb, wa, ba, wx, bx, lam)


def _lru_dir_kernel(direction, n_tiles, *refs):
    reverse = direction == 1
    if reverse:
        (u_ref, up_ref, un_ref, h0_ref, cw_ref, cb_ref, wa_ref, ba_ref, wx_ref, bx_ref, lam_ref,
         hf_ref, gate_ref, x_ref, mod_ref, wout_ref, lng_ref, lnb_ref, out_ref, a_s, b_s, hs, carry) = refs
    else:
        (u_ref, up_ref, un_ref, h0_ref, cw_ref, cb_ref, wa_ref, ba_ref, wx_ref, bx_ref, lam_ref,
         out_ref, a_s, b_s, carry) = refs
        hs = out_ref.at[0]
    step = pl.program_id(1)
    tile = (n_tiles - 1 - step) if reverse else step

    @pl.when(step == 0)
    def _():
        carry[...] = h0_ref[0, direction:direction + 1, :]

    u = u_ref[0]
    t = u.shape[0]
    prev = jnp.where(tile == 0, 0.0, up_ref[0])
    nxt = jnp.where(tile == n_tiles - 1, 0.0, un_ref[0])
    uc = _conv_tile(u, prev, nxt, cw_ref[...], cb_ref[...])
    _lru_coeffs(uc, wa_ref.at[direction], ba_ref[direction], wx_ref.at[direction], bx_ref[direction],
                lam_ref[direction], a_s, b_s)
    carry[...] = _scan_rows(a_s, b_s, hs, carry[...], t, reverse)

    if reverse:
        g = gate_ref[0]
        hx = hf_ref[0] + hs[...]
        y = _bf16_dot(hx * (g * _sigmoid(g)), wout_ref[...])
        out_ref[0] = _residual_ln(x_ref[0], y, mod_ref[0, 0][2:3], lng_ref[...], lnb_ref[...])


def _lru_dir(direction, u_all, n, h0, cw, cb, wa, ba, wx, bx, lam, extra=None):
    b, _, w = u_all.shape
    t = min(SCAN_TILE, n)
    n_tiles = n // t
    r = t // SUBLANE
    if direction == 1:
        tidx = lambda s: n_tiles - 1 - s
    else:
        tidx = lambda s: s
    full = lambda a: pl.BlockSpec(a.shape, lambda bi, s: (0,) * a.ndim)
    tile_spec = lambda width: pl.BlockSpec((1, t, width), lambda bi, s: (bi, tidx(s), 0))
    in_specs = [
        tile_spec(w),
        pl.BlockSpec((1, SUBLANE, w), lambda bi, s: (bi, jnp.maximum(tidx(s) * r - 1, 0), 0)),
        pl.BlockSpec((1, SUBLANE, w), lambda bi, s: (bi, (tidx(s) + 1) * r, 0)),
        pl.BlockSpec((1, 2, w), lambda bi, s: (bi, 0, 0)),
        full(cw), full(cb), full(wa), full(ba), full(wx), full(bx), full(lam),
    ]
    args = [u_all, u_all, u_all, h0, cw, cb, wa, ba, wx, bx, lam]
    scratch = [pltpu.VMEM((t, w), jnp.float32), pltpu.VMEM((t, w), jnp.float32)]
    if direction == 1:
        hf, gate, xc, modsel, wout, ln_g, ln_b = extra
        d = xc.shape[2]
        in_specs += [tile_spec(w), tile_spec(w), tile_spec(d),
                     pl.BlockSpec((1, 1, 3, d), lambda bi, s: (bi, 0, 0, 0)),
                     full(wout), full(ln_g), full(ln_b)]
        args += [hf, gate, xc, modsel, wout, ln_g, ln_b]
        scratch.append(pltpu.VMEM((t, w), jnp.float32))
        out_w = d
    else:
        out_w = w
    scratch.append(pltpu.VMEM((1, w), jnp.float32))
    return pl.pallas_call(
        functools.partial(_lru_dir_kernel, direction, n_tiles),
        grid=(b, n_tiles),
        in_specs=in_specs,
        out_specs=pl.BlockSpec((1, t, out_w), lambda bi, s: (bi, tidx(s), 0)),
        out_shape=jax.ShapeDtypeStruct((b, n, out_w), jnp.float32),
        scratch_shapes=scratch,
        compiler_params=_cparams("parallel", "arbitrary"),
        name="lru_bwd_out" if direction == 1 else "lru_fwd",
    )(*args)


def _rope_tables(n, c):
    t = jnp.arange(n)
    rows = (t // GRID_W).astype(jnp.float32)
    cols = (t % GRID_W).astype(jnp.float32)

    def ang(rot_dim):
        n_freq = rot_dim // 4
        freqs = ROPE_THETA ** (-jnp.arange(n_freq, dtype=jnp.float32) / n_freq)
        return jnp.concatenate([rows[:, None] * freqs, cols[:, None] * freqs], -1)

    aa, ab = ang(MLA_ROPE), ang(DIFF_QK)
    one = jnp.ones((n, MLA_NOPE), jnp.float32)
    zero = jnp.zeros((n, MLA_NOPE), jnp.float32)
    pad1 = jnp.ones((n, LANE - MLA_QK), jnp.float32)
    pad0 = jnp.zeros((n, LANE - MLA_QK), jnp.float32)
    ca = jnp.concatenate([one, jnp.cos(aa), jnp.cos(aa), pad1], 1)
    sa = jnp.concatenate([zero, -jnp.sin(aa), jnp.sin(aa), pad0], 1)
    cb = jnp.concatenate([jnp.cos(ab), jnp.cos(ab)] * 2, 1)
    sb = jnp.concatenate([-jnp.sin(ab), jnp.sin(ab)] * 2, 1)
    ident = lambda tab, v: jnp.concatenate([tab, jnp.full((c, LANE), v, jnp.float32)], 0)
    return ident(ca, 1.0), ident(sa, 0.0), ident(cb, 1.0), ident(sb, 0.0)


def _even_weights(w_in, w_uq, w_ukv):
    d = w_in.shape[0]
    bf = jnp.bfloat16
    o = [0, 384, 640, 672, 1184, 1696, 2208, 3232]
    cq, ckv, kr, dq, dk, dv, gate = (w_in[:, o[i]:o[i + 1]] for i in range(7))
    kr_group = jnp.concatenate([jnp.zeros((d, MLA_NOPE), w_in.dtype), kr,
                                jnp.zeros((d, LANE - MLA_QK), w_in.dtype)], 1)
    w1 = jnp.concatenate([cq, ckv, kr_group, dq, dk, dv, gate], 1).astype(bf)
    wuq = jnp.pad(w_uq.reshape(MLA_Q_LORA, MLA_HEADS, MLA_QK),
                  ((0, 0), (0, 0), (0, LANE - MLA_QK))).reshape(MLA_Q_LORA, MLA_HEADS * LANE).astype(bf)
    ukv = w_ukv.reshape(MLA_KV_LORA, MLA_HEADS, MLA_NOPE + MLA_V)
    wuk = jnp.pad(ukv[..., :MLA_NOPE], ((0, 0), (0, 0), (0, LANE - MLA_NOPE))).reshape(MLA_KV_LORA, MLA_HEADS * LANE)
    wuv = ukv[..., MLA_NOPE:].reshape(MLA_KV_LORA, MLA_HEADS * MLA_V)
    wkv = jnp.concatenate([wuk, wuv], 1).astype(bf)
    return w1, wuq, wkv


def _mod_select(mods_l, b):
    d = mods_l.shape[1] // 3
    mx = mods_l[:b].reshape(b, 3, d)
    mc = jnp.broadcast_to(mods_l[b].reshape(1, 3, d), (b, 3, d))
    return jnp.stack([mx, mc], axis=1)


def kernel(x, c, ctx, c_ctx, ada_w, ada_b, post_ln_g, post_ln_b, e_w_in, e_q_norm_g, e_w_uq, e_kv_norm_g, e_w_ukv, e_lam_q1, e_lam_k1, e_lam_q2, e_lam_k2, e_subln_g, e_w_out, o_w_in, o_conv_w, o_conv_b, o_gate_a_w, o_gate_a_b, o_gate_x_w, o_gate_x_b, o_lru_lambda, o_w_out):
    b, n, d = x.shape
    cl = ctx.shape[1]
    assert ada_w.shape[0] == DEPTH and b < SUBLANE
    assert n % Q_TILE == 0 and cl % TOKEN_TILE == 0 and n % cl == 0
    bf = jnp.bfloat16

    cond = jnp.concatenate([c, c_ctx[None], jnp.zeros((SUBLANE - b - 1, d), c.dtype)], 0)
    lamv = jnp.concatenate([e_lam_q1, e_lam_k1, e_lam_q2, e_lam_k2], 0)
    mods, lam = _adaln(cond, ada_w, ada_b, lamv)
    mod0, mod1 = _mod_select(mods[0], b), _mod_select(mods[1], b)

    w1, wuq, wkv = _even_weights(e_w_in[0], e_w_uq[0], e_w_ukv[0])
    tabs = _rope_tables(n, cl)
    qt, ka, vat, dqt, dk, dvt, gate = _even_proj(
        x, ctx, mod0, w1, e_q_norm_g[0][None], wuq, e_kv_norm_g[0][None], wkv, tabs)
    subln_col = e_subln_g[0][:, None]
    oax = _mla_attn(qt, ka, vat, n)
    obx = _diff_attn(dqt, dk, dvt, lam, subln_col, n)
    oac = _mla_attn(qt[:, :, n:], ka[:, n:], vat[:, :, n:], cl)
    obc = _diff_attn(dqt[:, :, n:], dk[:, n:], dvt[:, :, n:], lam, subln_col, cl)
    xc = _even_merge(x, ctx, mod0, oax, oac, obx, obc, gate, e_w_out[0].astype(bf),
                     post_ln_g[0][None], post_ln_b[0][None])

    u_all, gate1 = _odd_proj(xc, mod1, o_w_in[0].astype(bf), n // TOKEN_TILE)
    cw, cb = o_conv_w[0], o_conv_b[0][None]
    wa, wx = o_gate_a_w[0].astype(bf), o_gate_x_w[0].astype(bf)
    w = u_all.shape[2]
    ba = o_gate_a_b[0].reshape(2, 1, w)
    bx = o_gate_x_b[0].reshape(2, 1, w)
    lru_lam = o_lru_lambda[0].reshape(2, 1, w)
    h0 = _lru_ctx(u_all, n, cl, cw, cb, wa, ba, wx, bx, lru_lam)
    hf = _lru_dir(0, u_all, n, h0, cw, cb, wa, ba, wx, bx, lru_lam)
    return _lru_dir(1, u_all, n, h0, cw, cb, wa, ba, wx, bx, lru_lam,
                    extra=(hf, gate1, xc, mod1, o_w_out[0].astype(bf), post_ln_g[1][None], post_ln_b[1][None]))
```

```python
import functools
import math

import jax
import jax.numpy as jnp
from jax import lax
from jax.experimental import pallas as pl
from jax.experimental.pallas import tpu as pltpu

GRID_W = 64
ROPE_THETA = 10000.0
LN_EPS = 1e-6
RMS_EPS = 1e-6

MLA_HEADS = 8
MLA_Q_LORA = 384
MLA_KV_LORA = 256
MLA_NOPE = 64
MLA_ROPE = 32
MLA_V = 64
MLA_QK = MLA_NOPE + MLA_ROPE
MLA_SCALE = MLA_QK ** -0.5

DIFF_HEADS = 4
DIFF_QK = 64
DIFF_V = 2 * DIFF_QK
DIFF_SCALE = DIFF_QK ** -0.5

LRU_BLOCKS = 8
LRU_C = 8.0
CONV_W = 4

DEPTH = 2
DEEPNORM_ALPHA = (2 * DEPTH) ** 0.25
LAMBDA_INIT_0 = 0.8 - 0.6 * math.exp(-0.3 * 0)

LANE = 128
SUBLANE = 8
MXU_DEPTH = 256
LOG2E = 1.4426950408889634
NEG_BIG = -1e30

TOKEN_TILE = 256
Q_TILE = 512
KV_CHUNK = 1408
SCAN_TILE = 512
VMEM_LIMIT = 56 * 1024 * 1024

_G_CQ = (0, 384)
_G_CKV = (384, 640)
_G_KR = (640, 768)
_G_DQ = (768, 1280)
_G_DK = (1280, 1792)
_G_DV = (1792, 2304)
_G_GATE = (2304, 3328)
_EVEN_W = 3328


def _cparams(*sem):
    return pltpu.CompilerParams(dimension_semantics=sem, vmem_limit_bytes=VMEM_LIMIT)


def _bf16_dot(a, b):
    return jnp.dot(a.astype(jnp.bfloat16), b.astype(jnp.bfloat16), preferred_element_type=jnp.float32)


def _sigmoid(x):
    return 1.0 / (1.0 + jnp.exp(-x))


def _adaln_kernel(cond_ref, w_ref, b_ref, lamv_ref, mod_ref, lam_ref):
    cond = cond_ref[...]
    h = cond * _sigmoid(cond)
    mod_ref[0] = jnp.dot(h, w_ref[0], preferred_element_type=jnp.float32,
                         precision=lax.Precision.HIGHEST) + b_ref[0]
    lv = lamv_ref[...]
    d1 = jnp.sum(lv[0:1] * lv[1:2], axis=-1, keepdims=True)
    d2 = jnp.sum(lv[2:3] * lv[3:4], axis=-1, keepdims=True)
    lam = jnp.exp(d1) - jnp.exp(d2) + LAMBDA_INIT_0
    lam_ref[...] = jnp.broadcast_to(lam, lam_ref.shape)


def _adaln(cond, ada_w, ada_b, lamv):
    depth, d, d3 = ada_w.shape
    nj = d3 // d
    return pl.pallas_call(
        _adaln_kernel,
        grid=(depth, nj),
        in_specs=[
            pl.BlockSpec((SUBLANE, d), lambda l, j: (0, 0)),
            pl.BlockSpec((1, d, d), lambda l, j: (l, 0, j)),
            pl.BlockSpec((1, 1, d), lambda l, j: (l, 0, j)),
            pl.BlockSpec(lamv.shape, lambda l, j: (0, 0)),
        ],
        out_specs=[
            pl.BlockSpec((1, SUBLANE, d), lambda l, j: (l, 0, j)),
            pl.BlockSpec((SUBLANE, LANE), lambda l, j: (0, 0)),
        ],
        out_shape=[
            jax.ShapeDtypeStruct((depth, SUBLANE, d3), jnp.float32),
            jax.ShapeDtypeStruct((SUBLANE, LANE), jnp.float32),
        ],
        compiler_params=_cparams("arbitrary", "arbitrary"),
        name="adaln",
    )(cond, ada_w, ada_b.reshape(depth, 1, d3), lamv)


def _rms(x, g):
    return x * lax.rsqrt(jnp.mean(x * x, axis=-1, keepdims=True) + RMS_EPS) * g


def _rope_group(x, cos, sin, half, first_half_mask):
    partner = jnp.where(first_half_mask, pltpu.roll(x, LANE - half, 1), pltpu.roll(x, half, 1))
    return x * cos + partner * sin


def _even_proj_kernel(nx, x_ref, c_ref, mod_ref, w1_ref, qg_ref, wuq_ref, kvg_ref, wkv_ref,
                      ca_ref, sa_ref, cb_ref, sb_ref,
                      qt_ref, ka_ref, vat_ref, dqt_ref, dk_ref, dvt_ref, gate_ref):
    i = pl.program_id(1)
    xin = jnp.where(i >= nx, c_ref[0], x_ref[0])
    mod = mod_ref[0, 0]
    xm = (xin * (1.0 + mod[1:2]) + mod[0:1]).astype(w1_ref.dtype)

    def zcols(lo, hi):
        return jnp.dot(xm, w1_ref[:, lo:hi], preferred_element_type=jnp.float32)

    lane = lax.broadcasted_iota(jnp.int32, (1, LANE), 1)
    mla_first = jnp.logical_and(lane >= MLA_NOPE, lane < MLA_NOPE + MLA_ROPE // 2)
    diff_first = (lane % DIFF_QK) < DIFF_QK // 2
    ca, sa, cb, sb = ca_ref[...], sa_ref[...], cb_ref[...], sb_ref[...]

    z_lat = zcols(_G_CQ[0], _G_KR[1])
    z_dq = zcols(*_G_DQ)
    z_dk = zcols(*_G_DK)

    cqn = _rms(z_lat[:, _G_CQ[0]:_G_CQ[1]], qg_ref[...])
    q = _bf16_dot(cqn, wuq_ref[...])
    ckvn = _rms(z_lat[:, _G_CKV[0]:_G_CKV[1]], kvg_ref[...])
    kv = _bf16_dot(ckvn, wkv_ref[...])
    z_dv = zcols(*_G_DV)
    gate_ref[0] = zcols(*_G_GATE)

    q_heads = []
    for h in range(MLA_HEADS):
        qh = _rope_group(q[:, h * LANE:(h + 1) * LANE], ca, sa, MLA_ROPE // 2, mla_first)
        q_heads.append(qh * (MLA_SCALE * LOG2E))
    qt_ref[0] = jnp.concatenate(q_heads, axis=1).T.astype(qt_ref.dtype)

    kr = _rope_group(z_lat[:, _G_KR[0]:_G_KR[1]], ca, sa, MLA_ROPE // 2, mla_first)
    for h in range(MLA_HEADS):
        ka_ref[0, :, h * LANE:(h + 1) * LANE] = (kv[:, h * LANE:(h + 1) * LANE] + kr).astype(ka_ref.dtype)
    vat_ref[0] = kv[:, MLA_HEADS * LANE:].T.astype(vat_ref.dtype)

    dq_heads = []
    for h in range(DIFF_HEADS):
        sl = slice(h * LANE, (h + 1) * LANE)
        dq_heads.append(_rope_group(z_dq[:, sl], cb, sb, DIFF_QK // 2, diff_first) * (DIFF_SCALE * LOG2E))
        dk_ref[0, :, sl] = _rope_group(z_dk[:, sl], cb, sb, DIFF_QK // 2, diff_first).astype(dk_ref.dtype)
    dqt_ref[0] = jnp.concatenate(dq_heads, axis=1).T.astype(dqt_ref.dtype)
    dvt_ref[0] = z_dv.T.astype(dvt_ref.dtype)


def _even_proj(x, ctx, modsel, w1, qg, wuq, kvg, wkv, tabs):
    b, n, d = x.shape
    c = ctx.shape[1]
    t = TOKEN_TILE
    nx, nc = n // t, c // t
    nt = n + c
    full = lambda a: pl.BlockSpec(a.shape, lambda bi, i: (0,) * a.ndim)
    tab_spec = pl.BlockSpec((t, LANE), lambda bi, i: (i, 0))
    tok = lambda w: pl.BlockSpec((1, t, w), lambda bi, i: (bi, i, 0))
    tok_t = lambda w: pl.BlockSpec((1, w, t), lambda bi, i: (bi, 0, i))
    bf = jnp.bfloat16
    return pl.pallas_call(
        functools.partial(_even_proj_kernel, nx),
        grid=(b, nx + nc),
        in_specs=[
            pl.BlockSpec((1, t, d), lambda bi, i: (bi, jnp.minimum(i, nx - 1), 0)),
            pl.BlockSpec((1, t, d), lambda bi, i: (bi, jnp.maximum(i - nx, 0), 0)),
            pl.BlockSpec((1, 1, 3, d), lambda bi, i: (bi, jnp.where(i >= nx, 1, 0), 0, 0)),
            full(w1), full(qg), full(wuq), full(kvg), full(wkv),
            tab_spec, tab_spec, tab_spec, tab_spec,
        ],
        out_specs=[
            tok_t(MLA_HEADS * LANE), tok(MLA_HEADS * LANE), tok_t(MLA_HEADS * MLA_V),
            tok_t(DIFF_HEADS * LANE), tok(DIFF_HEADS * LANE), tok_t(DIFF_HEADS * DIFF_V),
            tok(MLA_HEADS * MLA_V + DIFF_HEADS * DIFF_V),
        ],
        out_shape=[
            jax.ShapeDtypeStruct((b, MLA_HEADS * LANE, nt), bf),
            jax.ShapeDtypeStruct((b, nt, MLA_HEADS * LANE), bf),
            jax.ShapeDtypeStruct((b, MLA_HEADS * MLA_V, nt), bf),
            jax.ShapeDtypeStruct((b, DIFF_HEADS * LANE, nt), bf),
            jax.ShapeDtypeStruct((b, nt, DIFF_HEADS * LANE), bf),
            jax.ShapeDtypeStruct((b, DIFF_HEADS * DIFF_V, nt), bf),
            jax.ShapeDtypeStruct((b, nt, MLA_HEADS * MLA_V + DIFF_HEADS * DIFF_V), jnp.float32),
        ],
        compiler_params=_cparams("parallel", "parallel"),
        name="even_proj",
    )(x, ctx, modsel, w1, qg, wuq, kvg, wkv, *tabs)


def _kv_chunk(nk):
    if nk <= KV_CHUNK:
        return nk
    for m in range(KV_CHUNK // LANE, 0, -1):
        if nk % (m * LANE) == 0 and (nk // (m * LANE)) % 2 == 0:
            return m * LANE
    raise ValueError(f"no even chunking of {nk} keys")


def _attn_sweep(n_chunks, tk, tq, n_qt, q_of, k_ref, vt_ref, s_scr, finish):
    dv = vt_ref.shape[1]

    def scores(t, j, slot):
        k = k_ref[0, pl.ds(pl.multiple_of(j * tk, LANE), tk), :]
        cms = []
        for si, qt in enumerate(q_of(t)):
            s = jnp.dot(k, qt, preferred_element_type=jnp.float32)
            s_scr[slot, si] = s
            cms.append(jnp.max(s, axis=0, keepdims=True))
        return tuple(cms)

    n_ops = s_scr.shape[1]
    ones = jnp.ones((2 * SUBLANE, MXU_DEPTH), vt_ref.dtype)
    init = (jnp.full((1, tq), NEG_BIG, jnp.float32), jnp.zeros((dv + 2 * SUBLANE, tq), jnp.float32)) * n_ops
    blocks = [(r0, min(MXU_DEPTH, tk - r0)) for r0 in range(0, tk, MXU_DEPTH)]

    def step(nxt, j_cur, slot_cur, cms, state):
        off_cur = pl.multiple_of(j_cur * tk, LANE)
        if nxt is not None:
            t_nxt, j_nxt, slot_nxt = nxt
            off_nxt = pl.multiple_of(j_nxt * tk, LANE)
            q_nxt = q_of(t_nxt)
        m_new = [jnp.maximum(state[2 * si], cms[si]) for si in range(n_ops)]
        cm_nxt, pv = [None] * n_ops, [None] * n_ops
        for r0, r in blocks:
            if nxt is not None:
                k = k_ref[0, pl.ds(off_nxt + r0, r), :]
            vt = vt_ref[0, :, pl.ds(off_cur + r0, r)]
            vt_ext = jnp.concatenate([vt, ones[:, :r]], axis=0)
            for si in range(n_ops):
                if nxt is not None:
                    s = jnp.dot(k, q_nxt[si], preferred_element_type=jnp.float32)
                    s_scr[slot_nxt, si, r0:r0 + r] = s
                    cmax = jnp.max(s, axis=0, keepdims=True)
                    cm_nxt[si] = cmax if cm_nxt[si] is None else jnp.maximum(cm_nxt[si], cmax)
                p = jnp.exp2(s_scr[slot_cur, si, r0:r0 + r] - m_new[si]).astype(vt.dtype)
                d = jnp.dot(vt_ext, p, preferred_element_type=jnp.float32)
                pv[si] = d if pv[si] is None else pv[si] + d
        out = []
        for si in range(n_ops):
            alpha = jnp.exp2(state[2 * si] - m_new[si])
            out += [m_new[si], alpha * state[2 * si + 1] + pv[si]]
        return tuple(cm_nxt), tuple(out)

    def result(st):
        return [st[2 * si + 1][:dv] / st[2 * si + 1][dv:dv + 1] for si in range(n_ops)]

    if n_chunks == 1:
        def tile(t, carry):
            finish(t, result(step(None, 0, 0, scores(t, 0, 0), init)[1]))
            return carry
        lax.fori_loop(0, n_qt, tile, 0)
        return

    assert n_chunks % 2 == 0

    def tile(t, cm):
        def pair(i, carry):
            cm_a, st = carry
            cm_b, st = step((t, 2 * i + 1, 1), 2 * i, 0, cm_a, st)
            cm_a, st = step((t, 2 * i + 2, 0), 2 * i + 1, 1, cm_b, st)
            return cm_a, st

        cm_a, st = lax.fori_loop(0, n_chunks // 2 - 1, pair, (cm, init), unroll=True)
        cm_b, st = step((t, n_chunks - 1, 1), n_chunks - 2, 0, cm_a, st)
        cm_next, st = step((jnp.minimum(t + 1, n_qt - 1), 0, 0), n_chunks - 1, 1, cm_b, st)
        finish(t, result(st))
        return cm_next

    lax.fori_loop(0, n_qt, tile, scores(0, 0, 0))


def _q_cols(t, tq):
    return pl.ds(pl.multiple_of(t * tq, LANE), tq)


def _mla_attn_kernel(n_chunks, tk, tq, qt_ref, k_ref, vt_ref, o_ref, s_scr):
    n_ops = s_scr.shape[1]

    def q_of(t):
        return [qt_ref[0, :, _q_cols(t * n_ops + i, tq)] for i in range(n_ops)]

    def finish(t, outs):
        for i in range(n_ops):
            o_ref[0, :, _q_cols(t * n_ops + i, tq)] = outs[i]

    _attn_sweep(n_chunks, tk, tq, qt_ref.shape[2] // (tq * n_ops), q_of, k_ref, vt_ref, s_scr, finish)


def _attn_call(kernel_fn, name, heads, dv, n_ops, qt, k, vt, nq, extra=()):
    b, _, nk = vt.shape
    tq = min(Q_TILE, nq)
    tk = _kv_chunk(nk)
    return pl.pallas_call(
        functools.partial(kernel_fn, nk // tk, tk, tq),
        grid=(b, heads),
        in_specs=[
            pl.BlockSpec((1, LANE, nq), lambda bi, h: (bi, h, 0)),
            pl.BlockSpec((1, nk, LANE), lambda bi, h: (bi, 0, h)),
            pl.BlockSpec((1, dv, nk), lambda bi, h: (bi, h, 0)),
        ] + [pl.BlockSpec(a.shape, lambda bi, h: (0, 0)) for a in extra],
        out_specs=pl.BlockSpec((1, dv, nq), lambda bi, h: (bi, h, 0)),
        out_shape=jax.ShapeDtypeStruct((b, heads * dv, nq), jnp.float32),
        scratch_shapes=[pltpu.VMEM((2, n_ops, tk, tq), jnp.float32)],
        compiler_params=_cparams("parallel", "parallel"),
        name=name,
    )(qt, k, vt, *extra)


def _mla_attn(qt, ka, vat, nq):
    n_ops = 2 if nq % (2 * Q_TILE) == 0 else 1
    return _attn_call(_mla_attn_kernel, "mla_attn", MLA_HEADS, MLA_V, n_ops, qt, ka, vat, nq)


def _diff_attn_kernel(n_chunks, tk, tq, qt_ref, k_ref, vt_ref, lam_ref, g_ref, o_ref, s_scr):
    row = lax.broadcasted_iota(jnp.int32, (LANE, 1), 0)

    def q_of(t):
        qt = qt_ref[0, :, _q_cols(t, tq)]
        zero = jnp.zeros_like(qt)
        return [jnp.where(row < DIFF_QK, qt, zero), jnp.where(row >= DIFF_QK, qt, zero)]

    def finish(t, outs):
        o = outs[0] - lam_ref[0:1, 0:1] * outs[1]
        o = o * lax.rsqrt(jnp.mean(o * o, axis=0, keepdims=True) + RMS_EPS) * g_ref[...]
        o_ref[0, :, _q_cols(t, tq)] = o * (1.0 - LAMBDA_INIT_0)

    _attn_sweep(n_chunks, tk, tq, qt_ref.shape[2] // tq, q_of, k_ref, vt_ref, s_scr, finish)


def _diff_attn(dqt, dk, dvt, lam, subln_col, nq):
    return _attn_call(_diff_attn_kernel, "diff_attn", DIFF_HEADS, DIFF_V, 2, dqt, dk, dvt, nq,
                      extra=(lam, subln_col))


def _residual_ln(xin, y, gate_vec, ln_g, ln_b):
    r = DEEPNORM_ALPHA * xin + gate_vec * y
    mu = jnp.mean(r, axis=-1, keepdims=True)
    rc = r - mu
    var = jnp.mean(rc * rc, axis=-1, keepdims=True)
    return rc * lax.rsqrt(var + LN_EPS) * ln_g + ln_b


def _even_merge_kernel(nx, x_ref, c_ref, mod_ref, oax_ref, oac_ref, obx_ref, obc_ref, gate_ref,
                       wout_ref, lng_ref, lnb_ref, out_ref):
    i = pl.program_id(1)
    is_ctx = i >= nx
    xin = jnp.where(is_ctx, c_ref[0], x_ref[0])
    oa = jnp.where(is_ctx, oac_ref[0], oax_ref[0]).T
    ob = jnp.where(is_ctx, obc_ref[0], obx_ref[0]).T
    o = jnp.concatenate([oa, ob], axis=1)
    g = gate_ref[0]
    y = _bf16_dot(o * (g * _sigmoid(g)), wout_ref[...])
    out_ref[0] = _residual_ln(xin, y, mod_ref[0, 0][2:3], lng_ref[...], lnb_ref[...])


def _even_merge(x, ctx, modsel, oax, oac, obx, obc, gate, wout, ln_g, ln_b):
    b, n, d = x.shape
    c = ctx.shape[1]
    t = TOKEN_TILE
    nx, nc = n // t, c // t
    wa, wb = oax.shape[1], obx.shape[1]
    full = lambda a: pl.BlockSpec(a.shape, lambda bi, i: (0,) * a.ndim)
    xi = lambda bi, i: (bi, jnp.minimum(i, nx - 1), 0)
    ci = lambda bi, i: (bi, jnp.maximum(i - nx, 0), 0)
    xit = lambda bi, i: (bi, 0, jnp.minimum(i, nx - 1))
    cit = lambda bi, i: (bi, 0, jnp.maximum(i - nx, 0))
    return pl.pallas_call(
        functools.partial(_even_merge_kernel, nx),
        grid=(b, nx + nc),
        in_specs=[
            pl.BlockSpec((1, t, d), xi),
            pl.BlockSpec((1, t, d), ci),
            pl.BlockSpec((1, 1, 3, d), lambda bi, i: (bi, jnp.where(i >= nx, 1, 0), 0, 0)),
            pl.BlockSpec((1, wa, t), xit), pl.BlockSpec((1, wa, t), cit),
            pl.BlockSpec((1, wb, t), xit), pl.BlockSpec((1, wb, t), cit),
            pl.BlockSpec((1, t, wa + wb), lambda bi, i: (bi, i, 0)),
            full(wout), full(ln_g), full(ln_b),
        ],
        out_specs=pl.BlockSpec((1, t, d), lambda bi, i: (bi, i, 0)),
        out_shape=jax.ShapeDtypeStruct((b, n + c, d), jnp.float32),
        compiler_params=_cparams("parallel", "parallel"),
        name="even_merge",
    )(x, ctx, modsel, oax, oac, obx, obc, gate, wout, ln_g, ln_b)


def _odd_proj_kernel(xc_ref, mod_ref, w_ref, u_ref, gate_ref):
    mod = mod_ref[0, 0]
    xm = xc_ref[0] * (1.0 + mod[1:2]) + mod[0:1]
    z = _bf16_dot(xm, w_ref[...])
    w = u_ref.shape[2]
    u_ref[0] = z[:, :w]
    gate_ref[0] = z[:, w:]


def _odd_proj(xc, modsel, w_in, nx):
    b, nt, d = xc.shape
    t = TOKEN_TILE
    w = w_in.shape[1] // 2
    return pl.pallas_call(
        _odd_proj_kernel,
        grid=(b, nt // t),
        in_specs=[
            pl.BlockSpec((1, t, d), lambda bi, i: (bi, i, 0)),
            pl.BlockSpec((1, 1, 3, d), lambda bi, i: (bi, jnp.where(i >= nx, 1, 0), 0, 0)),
            pl.BlockSpec(w_in.shape, lambda bi, i: (0, 0)),
        ],
        out_specs=[pl.BlockSpec((1, t, w), lambda bi, i: (bi, i, 0))] * 2,
        out_shape=[jax.ShapeDtypeStruct((b, nt, w), jnp.float32)] * 2,
        compiler_params=_cparams("parallel", "parallel"),
        name="odd_proj",
    )(xc, modsel, w_in)


def _conv_tile(u, prev, nxt, cw, cb):
    t = u.shape[0]
    row = lax.broadcasted_iota(jnp.int32, (SUBLANE, 1), 0)
    r1, r2, rp = pltpu.roll(u, 1, 0), pltpu.roll(u, 2, 0), pltpu.roll(u, t - 1, 0)
    m1 = jnp.concatenate([jnp.where(row == 0, prev[7:8], r1[:SUBLANE]), r1[SUBLANE:]], 0)
    m2 = jnp.concatenate([jnp.where(row == 0, prev[6:7], jnp.where(row == 1, prev[7:8], r2[:SUBLANE])),
                          r2[SUBLANE:]], 0)
    p1 = jnp.concatenate([rp[:t - SUBLANE], jnp.where(row == SUBLANE - 1, nxt[0:1], rp[t - SUBLANE:])], 0)
    return cw[0:1] * m2 + cw[1:2] * m1 + cw[2:3] * u + cw[3:4] * p1 + cb


def _lru_coeffs(uc, wa_ref, ba, wx_ref, bx, lam, a_ref, b_ref):
    sp = jnp.log1p(jnp.exp(-lam))
    for k in range(LRU_BLOCKS):
        sl = slice(k * LANE, (k + 1) * LANE)
        ub = uc[:, sl]
        r = _sigmoid(_bf16_dot(ub, wa_ref[k]) + ba[:, sl])
        ig = _sigmoid(_bf16_dot(ub, wx_ref[k]) + bx[:, sl])
        log_a = (-LRU_C) * r * sp[:, sl]
        a = jnp.exp(log_a)
        a_ref[:, sl] = a
        b_ref[:, sl] = jnp.sqrt(-jnp.tanh(log_a) * (a * a + 1.0)) * (ig * ub)


def _scan_rows(a_ref, b_ref, h_out_ref, h0, t, reverse):
    def step(s, h):
        r = (t - 1 - s) if reverse else s
        h = a_ref[pl.ds(r, 1), :] * h + b_ref[pl.ds(r, 1), :]
        h_out_ref[pl.ds(r, 1), :] = h
        return h

    return lax.fori_loop(0, t, step, h0, unroll=8)


def _lru_ctx_kernel(u_ref, cw_ref, cb_ref, wa_ref, ba_ref, wx_ref, bx_ref, lam_ref, h_ref,
                    a_s, b_s, hs):
    u = u_ref[0]
    zero8 = jnp.zeros((SUBLANE, u.shape[1]), jnp.float32)
    uc = _conv_tile(u, zero8, zero8, cw_ref[...], cb_ref[...])
    t = u.shape[0]
    for d in range(2):
        _lru_coeffs(uc, wa_ref.at[d], ba_ref[d], wx_ref.at[d], bx_ref[d], lam_ref[d], a_s, b_s)
        h = _scan_rows(a_s, b_s, hs, jnp.zeros((1, u.shape[1]), jnp.float32), t, reverse=(d == 1))
        h_ref[0, d:d + 1, :] = h


def _lru_ctx(u_all, n, c, cw, cb, wa, ba, wx, bx, lam):
    b, _, w = u_all.shape
    full = lambda a: pl.BlockSpec(a.shape, lambda bi: (0,) * a.ndim)
    return pl.pallas_call(
        _lru_ctx_kernel,
        grid=(b,),
        in_specs=[pl.BlockSpec((1, c, w), lambda bi: (bi, n // c, 0)),
                  full(cw), full(cb), full(wa), full(ba), full(wx), full(bx), full(lam)],
        out_specs=pl.BlockSpec((1, 2, w), lambda bi: (bi, 0, 0)),
        out_shape=jax.ShapeDtypeStruct((b, 2, w), jnp.float32),
        scratch_shapes=[pltpu.VMEM((c, w), jnp.float32)] * 3,
        compiler_params=_cparams("parallel"),
        name="lru_ctx",
    )(u_all, cw, cb, wa, ba, wx, bx, lam)


def _lru_dir_kernel(direction, n_tiles, *refs):
    reverse = direction == 1
    if reverse:
        (u_ref, up_ref, un_ref, h0_ref, cw_ref, cb_ref, wa_ref, ba_ref, wx_ref, bx_ref, lam_ref,
         hf_ref, gate_ref, x_ref, mod_ref, wout_ref, lng_ref, lnb_ref, out_ref, a_s, b_s, hs, carry) = refs
    else:
        (u_ref, up_ref, un_ref, h0_ref, cw_ref, cb_ref, wa_ref, ba_ref, wx_ref, bx_ref, lam_ref,
         out_ref, a_s, b_s, carry) = refs
        hs = out_ref.at[0]
    step = pl.program_id(1)
    tile = (n_tiles - 1 - step) if reverse else step

    @pl.when(step == 0)
    def _():
        carry[...] = h0_ref[0, direction:direction + 1, :]

    u = u_ref[0]
    t = u.shape[0]
    prev = jnp.where(tile == 0, 0.0, up_ref[0])
    nxt = jnp.where(tile == n_tiles - 1, 0.0, un_ref[0])
    uc = _conv_tile(u, prev, nxt, cw_ref[...], cb_ref[...])
    _lru_coeffs(uc, wa_ref.at[direction], ba_ref[direction], wx_ref.at[direction], bx_ref[direction],
                lam_ref[direction], a_s, b_s)
    carry[...] = _scan_rows(a_s, b_s, hs, carry[...], t, reverse)

    if reverse:
        g = gate_ref[0]
        hx = hf_ref[0] + hs[...]
        y = _bf16_dot(hx * (g * _sigmoid(g)), wout_ref[...])
        out_ref[0] = _residual_ln(x_ref[0], y, mod_ref[0, 0][2:3], lng_ref[...], lnb_ref[...])


def _lru_dir(direction, u_all, n, h0, cw, cb, wa, ba, wx, bx, lam, extra=None):
    b, _, w = u_all.shape
    t = min(SCAN_TILE, n)
    n_tiles = n // t
    r = t // SUBLANE
    if direction == 1:
        tidx = lambda s: n_tiles - 1 - s
    else:
        tidx = lambda s: s
    full = lambda a: pl.BlockSpec(a.shape, lambda bi, s: (0,) * a.ndim)
    tile_spec = lambda width: pl.BlockSpec((1, t, width), lambda bi, s: (bi, tidx(s), 0))
    in_specs = [
        tile_spec(w),
        pl.BlockSpec((1, SUBLANE, w), lambda bi, s: (bi, jnp.maximum(tidx(s) * r - 1, 0), 0)),
        pl.BlockSpec((1, SUBLANE, w), lambda bi, s: (bi, (tidx(s) + 1) * r, 0)),
        pl.BlockSpec((1, 2, w), lambda bi, s: (bi, 0, 0)),
        full(cw), full(cb), full(wa), full(ba), full(wx), full(bx), full(lam),
    ]
    args = [u_all, u_all, u_all, h0, cw, cb, wa, ba, wx, bx, lam]
    scratch = [pltpu.VMEM((t, w), jnp.float32), pltpu.VMEM((t, w), jnp.float32)]
    if direction == 1:
        hf, gate, xc, modsel, wout, ln_g, ln_b = extra
        d = xc.shape[2]
        in_specs += [tile_spec(w), tile_spec(w), tile_spec(d),
                     pl.BlockSpec((1, 1, 3, d), lambda bi, s: (bi, 0, 0, 0)),
                     full(wout), full(ln_g), full(ln_b)]
        args += [hf, gate, xc, modsel, wout, ln_g, ln_b]
        scratch.append(pltpu.VMEM((t, w), jnp.float32))
        out_w = d
    else:
        out_w = w
    scratch.append(pltpu.VMEM((1, w), jnp.float32))
    return pl.pallas_call(
        functools.partial(_lru_dir_kernel, direction, n_tiles),
        grid=(b, n_tiles),
        in_specs=in_specs,
        out_specs=pl.BlockSpec((1, t, out_w), lambda bi, s: (bi, tidx(s), 0)),
        out_shape=jax.ShapeDtypeStruct((b, n, out_w), jnp.float32),
        scratch_shapes=scratch,
        compiler_params=_cparams("parallel", "arbitrary"),
        name="lru_bwd_out" if direction == 1 else "lru_fwd",
    )(*args)


def _rope_tables(n, c):
    t = jnp.arange(n)
    rows = (t // GRID_W).astype(jnp.float32)
    cols = (t % GRID_W).astype(jnp.float32)

    def ang(rot_dim):
        n_freq = rot_dim // 4
        freqs = ROPE_THETA ** (-jnp.arange(n_freq, dtype=jnp.float32) / n_freq)
        return jnp.concatenate([rows[:, None] * freqs, cols[:, None] * freqs], -1)

    aa, ab = ang(MLA_ROPE), ang(DIFF_QK)
    one = jnp.ones((n, MLA_NOPE), jnp.float32)
    zero = jnp.zeros((n, MLA_NOPE), jnp.float32)
    pad1 = jnp.ones((n, LANE - MLA_QK), jnp.float32)
    pad0 = jnp.zeros((n, LANE - MLA_QK), jnp.float32)
    ca = jnp.concatenate([one, jnp.cos(aa), jnp.cos(aa), pad1], 1)
    sa = jnp.concatenate([zero, -jnp.sin(aa), jnp.sin(aa), pad0], 1)
    cb = jnp.concatenate([jnp.cos(ab), jnp.cos(ab)] * 2, 1)
    sb = jnp.concatenate([-jnp.sin(ab), jnp.sin(ab)] * 2, 1)
    ident = lambda tab, v: jnp.concatenate([tab, jnp.full((c, LANE), v, jnp.float32)], 0)
    return ident(ca, 1.0), ident(sa, 0.0), ident(cb, 1.0), ident(sb, 0.0)


def _even_weights(w_in, w_uq, w_ukv):
    d = w_in.shape[0]
    bf = jnp.bfloat16
    o = [0, 384, 640, 672, 1184, 1696, 2208, 3232]
    cq, ckv, kr, dq, dk, dv, gate = (w_in[:, o[i]:o[i + 1]] for i in range(7))
    kr_group = jnp.concatenate([jnp.zeros((d, MLA_NOPE), w_in.dtype), kr,
                                jnp.zeros((d, LANE - MLA_QK), w_in.dtype)], 1)
    w1 = jnp.concatenate([cq, ckv, kr_group, dq, dk, dv, gate], 1).astype(bf)
    wuq = jnp.pad(w_uq.reshape(MLA_Q_LORA, MLA_HEADS, MLA_QK),
                  ((0, 0), (0, 0), (0, LANE - MLA_QK))).reshape(MLA_Q_LORA, MLA_HEADS * LANE).astype(bf)
    ukv = w_ukv.reshape(MLA_KV_LORA, MLA_HEADS, MLA_NOPE + MLA_V)
    wuk = jnp.pad(ukv[..., :MLA_NOPE], ((0, 0), (0, 0), (0, LANE - MLA_NOPE))).reshape(MLA_KV_LORA, MLA_HEADS * LANE)
    wuv = ukv[..., MLA_NOPE:].reshape(MLA_KV_LORA, MLA_HEADS * MLA_V)
    wkv = jnp.concatenate([wuk, wuv], 1).astype(bf)
    return w1, wuq, wkv


def _mod_select(mods_l, b):
    d = mods_l.shape[1] // 3
    mx = mods_l[:b].reshape(b, 3, d)
    mc = jnp.broadcast_to(mods_l[b].reshape(1, 3, d), (b, 3, d))
    return jnp.stack([mx, mc], axis=1)


def kernel(x, c, ctx, c_ctx, ada_w, ada_b, post_ln_g, post_ln_b, e_w_in, e_q_norm_g, e_w_uq, e_kv_norm_g, e_w_ukv, e_lam_q1, e_lam_k1, e_lam_q2, e_lam_k2, e_subln_g, e_w_out, o_w_in, o_conv_w, o_conv_b, o_gate_a_w, o_gate_a_b, o_gate_x_w, o_gate_x_b, o_lru_lambda, o_w_out):
    b, n, d = x.shape
    cl = ctx.shape[1]
    assert ada_w.shape[0] == DEPTH and b < SUBLANE
    assert n % Q_TILE == 0 and cl % TOKEN_TILE == 0 and n % cl == 0
    bf = jnp.bfloat16

    cond = jnp.concatenate([c, c_ctx[None], jnp.zeros((SUBLANE - b - 1, d), c.dtype)], 0)
    lamv = jnp.concatenate([e_lam_q1, e_lam_k1, e_lam_q2, e_lam_k2], 0)
    mods, lam = _adaln(cond, ada_w, ada_b, lamv)
    mod0, mod1 = _mod_select(mods[0], b), _mod_select(mods[1], b)

    w1, wuq, wkv = _even_weights(e_w_in[0], e_w_uq[0], e_w_ukv[0])
    tabs = _rope_tables(n, cl)
    qt, ka, vat, dqt, dk, dvt, gate = _even_proj(
        x, ctx, mod0, w1, e_q_norm_g[0][None], wuq, e_kv_norm_g[0][None], wkv, tabs)
    subln_col = e_subln_g[0][:, None]
    oax = _mla_attn(qt, ka, vat, n)
    obx = _diff_attn(dqt, dk, dvt, lam, subln_col, n)
    oac = _mla_attn(qt[:, :, n:], ka[:, n:], vat[:, :, n:], cl)
    obc = _diff_attn(dqt[:, :, n:], dk[:, n:], dvt[:, :, n:], lam, subln_col, cl)
    xc = _even_merge(x, ctx, mod0, oax, oac, obx, obc, gate, e_w_out[0].astype(bf),
                     post_ln_g[0][None], post_ln_b[0][None])

    u_all, gate1 = _odd_proj(xc, mod1, o_w_in[0].astype(bf), n // TOKEN_TILE)
    cw, cb = o_conv_w[0], o_conv_b[0][None]
    wa, wx = o_gate_a_w[0].astype(bf), o_gate_x_w[0].astype(bf)
    w = u_all.shape[2]
    ba = o_gate_a_b[0].reshape(2, 1, w)
    bx = o_gate_x_b[0].reshape(2, 1, w)
    lru_lam = o_lru_lambda[0].reshape(2, 1, w)
    h0 = _lru_ctx(u_all, n, cl, cw, cb, wa, ba, wx, bx, lru_lam)
    hf = _lru_dir(0, u_all, n, h0, cw, cb, wa, ba, wx, bx, lru_lam)
    return _lru_dir(1, u_all, n, h0, cw, cb, wa, ba, wx, bx, lru_lam,
                    extra=(hf, gate1, xc, mod1, o_w_out[0].astype(bf), post_ln_g[1][None], post_ln_b[1][None]))
```

```python
import functools
import math

import jax
import jax.numpy as jnp
from jax import lax
from jax.experimental import pallas as pl
from jax.experimental.pallas import tpu as pltpu

GRID_W = 64
ROPE_THETA = 10000.0
LN_EPS = 1e-6
RMS_EPS = 1e-6

MLA_HEADS = 8
MLA_Q_LORA = 384
MLA_KV_LORA = 256
MLA_NOPE = 64
MLA_ROPE = 32
MLA_V = 64
MLA_QK = MLA_NOPE + MLA_ROPE
MLA_SCALE = MLA_QK ** -0.5

DIFF_HEADS = 4
DIFF_QK = 64
DIFF_V = 2 * DIFF_QK
DIFF_SCALE = DIFF_QK ** -0.5

LRU_BLOCKS = 8
LRU_C = 8.0
CONV_W = 4

DEPTH = 2
DEEPNORM_ALPHA = (2 * DEPTH) ** 0.25
LAMBDA_INIT_0 = 0.8 - 0.6 * math.exp(-0.3 * 0)

LANE = 128
SUBLANE = 8
MXU_DEPTH = 256
LOG2E = 1.4426950408889634
NEG_BIG = -1e30

TOKEN_TILE = 256
Q_TILE = 512
KV_CHUNK = 1408
SCAN_TILE = 512
VMEM_LIMIT = 56 * 1024 * 1024

_G_CQ = (0, 384)
_G_CKV = (384, 640)
_G_KR = (640, 768)
_G_DQ = (768, 1280)
_G_DK = (1280, 1792)
_G_DV = (1792, 2304)
_G_GATE = (2304, 3328)
_EVEN_W = 3328


def _cparams(*sem):
    return pltpu.CompilerParams(dimension_semantics=sem, vmem_limit_bytes=VMEM_LIMIT)


def _bf16_dot(a, b):
    return jnp.dot(a.astype(jnp.bfloat16), b.astype(jnp.bfloat16), preferred_element_type=jnp.float32)


def _sigmoid(x):
    return 1.0 / (1.0 + jnp.exp(-x))


def _adaln_kernel(cond_ref, w_ref, b_ref, lamv_ref, mod_ref, lam_ref):
    cond = cond_ref[...]
    h = cond * _sigmoid(cond)
    mod_ref[0] = jnp.dot(h, w_ref[0], preferred_element_type=jnp.float32,
                         precision=lax.Precision.HIGHEST) + b_ref[0]
    lv = lamv_ref[...]
    d1 = jnp.sum(lv[0:1] * lv[1:2], axis=-1, keepdims=True)
    d2 = jnp.sum(lv[2:3] * lv[3:4], axis=-1, keepdims=True)
    lam = jnp.exp(d1) - jnp.exp(d2) + LAMBDA_INIT_0
    lam_ref[...] = jnp.broadcast_to(lam, lam_ref.shape)


def _adaln(cond, ada_w, ada_b, lamv):
    depth, d, d3 = ada_w.shape
    nj = d3 // d
    return pl.pallas_call(
        _adaln_kernel,
        grid=(depth, nj),
        in_specs=[
            pl.BlockSpec((SUBLANE, d), lambda l, j: (0, 0)),
            pl.BlockSpec((1, d, d), lambda l, j: (l, 0, j)),
            pl.BlockSpec((1, 1, d), lambda l, j: (l, 0, j)),
            pl.BlockSpec(lamv.shape, lambda l, j: (0, 0)),
        ],
        out_specs=[
            pl.BlockSpec((1, SUBLANE, d), lambda l, j: (l, 0, j)),
            pl.BlockSpec((SUBLANE, LANE), lambda l, j: (0, 0)),
        ],
        out_shape=[
            jax.ShapeDtypeStruct((depth, SUBLANE, d3), jnp.float32),
            jax.ShapeDtypeStruct((SUBLANE, LANE), jnp.float32),
        ],
        compiler_params=_cparams("arbitrary", "arbitrary"),
        name="adaln",
    )(cond, ada_w, ada_b.reshape(depth, 1, d3), lamv)


def _rms(x, g):
    return x * lax.rsqrt(jnp.mean(x * x, axis=-1, keepdims=True) + RMS_EPS) * g


def _rope_group(x, cos, sin, half, first_half_mask):
    partner = jnp.where(first_half_mask, pltpu.roll(x, LANE - half, 1), pltpu.roll(x, half, 1))
    return x * cos + partner * sin


def _even_proj_kernel(nx, x_ref, c_ref, mod_ref, w1_ref, qg_ref, wuq_ref, kvg_ref, wkv_ref,
                      ca_ref, sa_ref, cb_ref, sb_ref,
                      qt_ref, ka_ref, vat_ref, dqt_ref, dk_ref, dvt_ref, gate_ref):
    i = pl.program_id(1)
    xin = jnp.where(i >= nx, c_ref[0], x_ref[0])
    mod = mod_ref[0, 0]
    xm = (xin * (1.0 + mod[1:2]) + mod[0:1]).astype(w1_ref.dtype)

    def zcols(lo, hi):
        return jnp.dot(xm, w1_ref[:, lo:hi], preferred_element_type=jnp.float32)

    lane = lax.broadcasted_iota(jnp.int32, (1, LANE), 1)
    mla_first = jnp.logical_and(lane >= MLA_NOPE, lane < MLA_NOPE + MLA_ROPE // 2)
    diff_first = (lane % DIFF_QK) < DIFF_QK // 2
    ca, sa, cb, sb = ca_ref[...], sa_ref[...], cb_ref[...], sb_ref[...]

    z_lat = zcols(_G_CQ[0], _G_KR[1])
    z_dq = zcols(*_G_DQ)
    z_dk = zcols(*_G_DK)

    cqn = _rms(z_lat[:, _G_CQ[0]:_G_CQ[1]], qg_ref[...])
    q = _bf16_dot(cqn, wuq_ref[...])
    ckvn = _rms(z_lat[:, _G_CKV[0]:_G_CKV[1]], kvg_ref[...])
    kv = _bf16_dot(ckvn, wkv_ref[...])
    z_dv = zcols(*_G_DV)
    gate_ref[0] = zcols(*_G_GATE)

    q_heads = []
    for h in range(MLA_HEADS):
        qh = _rope_group(q[:, h * LANE:(h + 1) * LANE], ca, sa, MLA_ROPE // 2, mla_first)
        q_heads.append(qh * (MLA_SCALE * LOG2E))
    qt_ref[0] = jnp.concatenate(q_heads, axis=1).T.astype(qt_ref.dtype)

    kr = _rope_group(z_lat[:, _G_KR[0]:_G_KR[1]], ca, sa, MLA_ROPE // 2, mla_first)
    for h in range(MLA_HEADS):
        ka_ref[0, :, h * LANE:(h + 1) * LANE] = (kv[:, h * LANE:(h + 1) * LANE] + kr).astype(ka_ref.dtype)
    vat_ref[0] = kv[:, MLA_HEADS * LANE:].T.astype(vat_ref.dtype)

    dq_heads = []
    for h in range(DIFF_HEADS):
        sl = slice(h * LANE, (h + 1) * LANE)
        dq_heads.append(_rope_group(z_dq[:, sl], cb, sb, DIFF_QK // 2, diff_first) * (DIFF_SCALE * LOG2E))
        dk_ref[0, :, sl] = _rope_group(z_dk[:, sl], cb, sb, DIFF_QK // 2, diff_first).astype(dk_ref.dtype)
    dqt_ref[0] = jnp.concatenate(dq_heads, axis=1).T.astype(dqt_ref.dtype)
    dvt_ref[0] = z_dv.T.astype(dvt_ref.dtype)


def _even_proj(x, ctx, modsel, w1, qg, wuq, kvg, wkv, tabs):
    b, n, d = x.shape
    c = ctx.shape[1]
    t = TOKEN_TILE
    nx, nc = n // t, c // t
    nt = n + c
    full = lambda a: pl.BlockSpec(a.shape, lambda bi, i: (0,) * a.ndim)
    tab_spec = pl.BlockSpec((t, LANE), lambda bi, i: (i, 0))
    tok = lambda w: pl.BlockSpec((1, t, w), lambda bi, i: (bi, i, 0))
    tok_t = lambda w: pl.BlockSpec((1, w, t), lambda bi, i: (bi, 0, i))
    bf = jnp.bfloat16
    return pl.pallas_call(
        functools.partial(_even_proj_kernel, nx),
        grid=(b, nx + nc),
        in_specs=[
            pl.BlockSpec((1, t, d), lambda bi, i: (bi, jnp.minimum(i, nx - 1), 0)),
            pl.BlockSpec((1, t, d), lambda bi, i: (bi, jnp.maximum(i - nx, 0), 0)),
            pl.BlockSpec((1, 1, 3, d), lambda bi, i: (bi, jnp.where(i >= nx, 1, 0), 0, 0)),
            full(w1), full(qg), full(wuq), full(kvg), full(wkv),
            tab_spec, tab_spec, tab_spec, tab_spec,
        ],
        out_specs=[
            tok_t(MLA_HEADS * LANE), tok(MLA_HEADS * LANE), tok_t(MLA_HEADS * MLA_V),
            tok_t(DIFF_HEADS * LANE), tok(DIFF_HEADS * LANE), tok_t(DIFF_HEADS * DIFF_V),
            tok(MLA_HEADS * MLA_V + DIFF_HEADS * DIFF_V),
        ],
        out_shape=[
            jax.ShapeDtypeStruct((b, MLA_HEADS * LANE, nt), bf),
            jax.ShapeDtypeStruct((b, nt, MLA_HEADS * LANE), bf),
            jax.ShapeDtypeStruct((b, MLA_HEADS * MLA_V, nt), bf),
            jax.ShapeDtypeStruct((b, DIFF_HEADS * LANE, nt), bf),
            jax.ShapeDtypeStruct((b, nt, DIFF_HEADS * LANE), bf),
            jax.ShapeDtypeStruct((b, DIFF_HEADS * DIFF_V, nt), bf),
            jax.ShapeDtypeStruct((b, nt, MLA_HEADS * MLA_V + DIFF_HEADS * DIFF_V), jnp.float32),
        ],
        compiler_params=_cparams("parallel", "parallel"),
        name="even_proj",
    )(x, ctx, modsel, w1, qg, wuq, kvg, wkv, *tabs)


def _kv_chunk(nk):
    if nk <= KV_CHUNK:
        return nk
    for m in range(KV_CHUNK // LANE, 0, -1):
        if nk % (m * LANE) == 0 and (nk // (m * LANE)) % 2 == 0:
            return m * LANE
    raise ValueError(f"no even chunking of {nk} keys")


def _attn_sweep(n_chunks, tk, tq, n_qt, q_of, k_ref, vt_ref, s_scr, finish):
    dv = vt_ref.shape[1]

    def scores(t, j, slot):
        k = k_ref[0, pl.ds(pl.multiple_of(j * tk, LANE), tk), :]
        cms = []
        for si, qt in enumerate(q_of(t)):
            s = jnp.dot(k, qt, preferred_element_type=jnp.float32)
            s_scr[slot, si] = s
            cms.append(jnp.max(s, axis=0, keepdims=True))
        return tuple(cms)

    n_ops = s_scr.shape[1]
    ones = jnp.ones((2 * SUBLANE, MXU_DEPTH), vt_ref.dtype)
    init = (jnp.full((1, tq), NEG_BIG, jnp.float32), jnp.zeros((dv + 2 * SUBLANE, tq), jnp.float32)) * n_ops
    blocks = [(r0, min(MXU_DEPTH, tk - r0)) for r0 in range(0, tk, MXU_DEPTH)]

    def step(nxt, j_cur, slot_cur, cms, state):
        off_cur = pl.multiple_of(j_cur * tk, LANE)
        if nxt is not None:
            t_nxt, j_nxt, slot_nxt = nxt
            off_nxt = pl.multiple_of(j_nxt * tk, LANE)
            q_nxt = q_of(t_nxt)
        m_new = [jnp.maximum(state[2 * si], cms[si]) for si in range(n_ops)]
        cm_nxt, pv = [None] * n_ops, [None] * n_ops
        for r0, r in blocks:
            if nxt is not None:
                k = k_ref[0, pl.ds(off_nxt + r0, r), :]
            vt = vt_ref[0, :, pl.ds(off_cur + r0, r)]
            vt_ext = jnp.concatenate([vt, ones[:, :r]], axis=0)
            for si in range(n_ops):
                if nxt is not None:
                    s = jnp.dot(k, q_nxt[si], preferred_element_type=jnp.float32)
                    s_scr[slot_nxt, si, r0:r0 + r] = s
                    cmax = jnp.max(s, axis=0, keepdims=True)
                    cm_nxt[si] = cmax if cm_nxt[si] is None else jnp.maximum(cm_nxt[si], cmax)
                p = jnp.exp2(s_scr[slot_cur, si, r0:r0 + r] - m_new[si]).astype(vt.dtype)
                d = jnp.dot(vt_ext, p, preferred_element_type=jnp.float32)
                pv[si] = d if pv[si] is None else pv[si] + d
        out = []
        for si in range(n_ops):
            alpha = jnp.exp2(state[2 * si] - m_new[si])
            out += [m_new[si], alpha * state[2 * si + 1] + pv[si]]
        return tuple(cm_nxt), tuple(out)

    def result(st):
        return [st[2 * si + 1][:dv] / st[2 * si + 1][dv:dv + 1] for si in range(n_ops)]

    if n_chunks == 1:
        def tile(t, carry):
            finish(t, result(step(None, 0, 0, scores(t, 0, 0), init)[1]))
            return carry
        lax.fori_loop(0, n_qt, tile, 0)
        return

    assert n_chunks % 2 == 0

    def tile(t, cm):
        def pair(i, carry):
            cm_a, st = carry
            cm_b, st = step((t, 2 * i + 1, 1), 2 * i, 0, cm_a, st)
            cm_a, st = step((t, 2 * i + 2, 0), 2 * i + 1, 1, cm_b, st)
            return cm_a, st

        cm_a, st = lax.fori_loop(0, n_chunks // 2 - 1, pair, (cm, init), unroll=True)
        cm_b, st = step((t, n_chunks - 1, 1), n_chunks - 2, 0, cm_a, st)
        cm_next, st = step((jnp.minimum(t + 1, n_qt - 1), 0, 0), n_chunks - 1, 1, cm_b, st)
        finish(t, result(st))
        return cm_next

    lax.fori_loop(0, n_qt, tile, scores(0, 0, 0), unroll=2)


def _q_cols(t, tq):
    return pl.ds(pl.multiple_of(t * tq, LANE), tq)


def _mla_attn_kernel(n_chunks, tk, tq, qt_ref, k_ref, vt_ref, o_ref, s_scr):
    n_ops = s_scr.shape[1]

    def q_of(t):
        return [qt_ref[0, :, _q_cols(t * n_ops + i, tq)] for i in range(n_ops)]

    def finish(t, outs):
        for i in range(n_ops):
            o_ref[0, :, _q_cols(t * n_ops + i, tq)] = outs[i]

    _attn_sweep(n_chunks, tk, tq, qt_ref.shape[2] // (tq * n_ops), q_of, k_ref, vt_ref, s_scr, finish)


def _attn_call(kernel_fn, name, heads, dv, n_ops, qt, k, vt, nq, extra=()):
    b, _, nk = vt.shape
    tq = min(Q_TILE, nq)
    tk = _kv_chunk(nk)
    return pl.pallas_call(
        functools.partial(kernel_fn, nk // tk, tk, tq),
        grid=(b, heads),
        in_specs=[
            pl.BlockSpec((1, LANE, nq), lambda bi, h: (bi, h, 0)),
            pl.BlockSpec((1, nk, LANE), lambda bi, h: (bi, 0, h)),
            pl.BlockSpec((1, dv, nk), lambda bi, h: (bi, h, 0)),
        ] + [pl.BlockSpec(a.shape, lambda bi, h: (0, 0)) for a in extra],
        out_specs=pl.BlockSpec((1, dv, nq), lambda bi, h: (bi, h, 0)),
        out_shape=jax.ShapeDtypeStruct((b, heads * dv, nq), jnp.float32),
        scratch_shapes=[pltpu.VMEM((2, n_ops, tk, tq), jnp.float32)],
        compiler_params=_cparams("parallel", "parallel"),
        name=name,
    )(qt, k, vt, *extra)


def _mla_attn(qt, ka, vat, nq):
    n_ops = 2 if nq % (2 * Q_TILE) == 0 else 1
    return _attn_call(_mla_attn_kernel, "mla_attn", MLA_HEADS, MLA_V, n_ops, qt, ka, vat, nq)


def _diff_attn_kernel(n_chunks, tk, tq, qt_ref, k_ref, vt_ref, lam_ref, g_ref, o_ref, s_scr):
    row = lax.broadcasted_iota(jnp.int32, (LANE, 1), 0)

    def q_of(t):
        qt = qt_ref[0, :, _q_cols(t, tq)]
        zero = jnp.zeros_like(qt)
        return [jnp.where(row < DIFF_QK, qt, zero), jnp.where(row >= DIFF_QK, qt, zero)]

    def finish(t, outs):
        o = outs[0] - lam_ref[0:1, 0:1] * outs[1]
        o = o * lax.rsqrt(jnp.mean(o * o, axis=0, keepdims=True) + RMS_EPS) * g_ref[...]
        o_ref[0, :, _q_cols(t, tq)] = o * (1.0 - LAMBDA_INIT_0)

    _attn_sweep(n_chunks, tk, tq, qt_ref.shape[2] // tq, q_of, k_ref, vt_ref, s_scr, finish)


def _diff_attn(dqt, dk, dvt, lam, subln_col, nq):
    return _attn_call(_diff_attn_kernel, "diff_attn", DIFF_HEADS, DIFF_V, 2, dqt, dk, dvt, nq,
                      extra=(lam, subln_col))


def _residual_ln(xin, y, gate_vec, ln_g, ln_b):
    r = DEEPNORM_ALPHA * xin + gate_vec * y
    mu = jnp.mean(r, axis=-1, keepdims=True)
    rc = r - mu
    var = jnp.mean(rc * rc, axis=-1, keepdims=True)
    return rc * lax.rsqrt(var + LN_EPS) * ln_g + ln_b


def _even_merge_kernel(nx, x_ref, c_ref, mod_ref, oax_ref, oac_ref, obx_ref, obc_ref, gate_ref,
                       wout_ref, lng_ref, lnb_ref, out_ref):
    i = pl.program_id(1)
    is_ctx = i >= nx
    xin = jnp.where(is_ctx, c_ref[0], x_ref[0])
    oa = jnp.where(is_ctx, oac_ref[0], oax_ref[0]).T
    ob = jnp.where(is_ctx, obc_ref[0], obx_ref[0]).T
    o = jnp.concatenate([oa, ob], axis=1)
    g = gate_ref[0]
    y = _bf16_dot(o * (g * _sigmoid(g)), wout_ref[...])
    out_ref[0] = _residual_ln(xin, y, mod_ref[0, 0][2:3], lng_ref[...], lnb_ref[...])


def _even_merge(x, ctx, modsel, oax, oac, obx, obc, gate, wout, ln_g, ln_b):
    b, n, d = x.shape
    c = ctx.shape[1]
    t = TOKEN_TILE
    nx, nc = n // t, c // t
    wa, wb = oax.shape[1], obx.shape[1]
    full = lambda a: pl.BlockSpec(a.shape, lambda bi, i: (0,) * a.ndim)
    xi = lambda bi, i: (bi, jnp.minimum(i, nx - 1), 0)
    ci = lambda bi, i: (bi, jnp.maximum(i - nx, 0), 0)
    xit = lambda bi, i: (bi, 0, jnp.minimum(i, nx - 1))
    cit = lambda bi, i: (bi, 0, jnp.maximum(i - nx, 0))
    return pl.pallas_call(
        functools.partial(_even_merge_kernel, nx),
        grid=(b, nx + nc),
        in_specs=[
            pl.BlockSpec((1, t, d), xi),
            pl.BlockSpec((1, t, d), ci),
            pl.BlockSpec((1, 1, 3, d), lambda bi, i: (bi, jnp.where(i >= nx, 1, 0), 0, 0)),
            pl.BlockSpec((1, wa, t), xit), pl.BlockSpec((1, wa, t), cit),
            pl.BlockSpec((1, wb, t), xit), pl.BlockSpec((1, wb, t), cit),
            pl.BlockSpec((1, t, wa + wb), lambda bi, i: (bi, i, 0)),
            full(wout), full(ln_g), full(ln_b),
        ],
        out_specs=pl.BlockSpec((1, t, d), lambda bi, i: (bi, i, 0)),
        out_shape=jax.ShapeDtypeStruct((b, n + c, d), jnp.float32),
        compiler_params=_cparams("parallel", "parallel"),
        name="even_merge",
    )(x, ctx, modsel, oax, oac, obx, obc, gate, wout, ln_g, ln_b)


def _odd_proj_kernel(xc_ref, mod_ref, w_ref, u_ref, gate_ref):
    mod = mod_ref[0, 0]
    xm = xc_ref[0] * (1.0 + mod[1:2]) + mod[0:1]
    z = _bf16_dot(xm, w_ref[...])
    w = u_ref.shape[2]
    u_ref[0] = z[:, :w]
    gate_ref[0] = z[:, w:]


def _odd_proj(xc, modsel, w_in, nx):
    b, nt, d = xc.shape
    t = TOKEN_TILE
    w = w_in.shape[1] // 2
    return pl.pallas_call(
        _odd_proj_kernel,
        grid=(b, nt // t),
        in_specs=[
            pl.BlockSpec((1, t, d), lambda bi, i: (bi, i, 0)),
            pl.BlockSpec((1, 1, 3, d), lambda bi, i: (bi, jnp.where(i >= nx, 1, 0), 0, 0)),
            pl.BlockSpec(w_in.shape, lambda bi, i: (0, 0)),
        ],
        out_specs=[pl.BlockSpec((1, t, w), lambda bi, i: (bi, i, 0))] * 2,
        out_shape=[jax.ShapeDtypeStruct((b, nt, w), jnp.float32)] * 2,
        compiler_params=_cparams("parallel", "parallel"),
        name="odd_proj",
    )(xc, modsel, w_in)


def _conv_tile(u, prev, nxt, cw, cb):
    t = u.shape[0]
    row = lax.broadcasted_iota(jnp.int32, (SUBLANE, 1), 0)
    r1, r2, rp = pltpu.roll(u, 1, 0), pltpu.roll(u, 2, 0), pltpu.roll(u, t - 1, 0)
    m1 = jnp.concatenate([jnp.where(row == 0, prev[7:8], r1[:SUBLANE]), r1[SUBLANE:]], 0)
    m2 = jnp.concatenate([jnp.where(row == 0, prev[6:7], jnp.where(row == 1, prev[7:8], r2[:SUBLANE])),
                          r2[SUBLANE:]], 0)
    p1 = jnp.concatenate([rp[:t - SUBLANE], jnp.where(row == SUBLANE - 1, nxt[0:1], rp[t - SUBLANE:])], 0)
    return cw[0:1] * m2 + cw[1:2] * m1 + cw[2:3] * u + cw[3:4] * p1 + cb


def _lru_coeffs(uc, wa_ref, ba, wx_ref, bx, lam, a_ref, b_ref):
    neg_log_a1 = LRU_C * jnp.log1p(jnp.exp(-lam))
    for k in range(LRU_BLOCKS):
        sl = slice(k * LANE, (k + 1) * LANE)
        ub = uc[:, sl]
        r = _sigmoid(_bf16_dot(ub, wa_ref[k]) + ba[:, sl])
        ig = _sigmoid(_bf16_dot(ub, wx_ref[k]) + bx[:, sl])
        a = jnp.exp2(r * (-LOG2E * neg_log_a1[:, sl]))
        a_ref[:, sl] = a
        b_ref[:, sl] = jnp.sqrt(jnp.tanh(r * neg_log_a1[:, sl]) * (a * a + 1.0)) * (ig * ub)


def _scan_rows(a_ref, b_ref, h_out_ref, h0, t, reverse):
    def step(s, h):
        r = (t - 1 - s) if reverse else s
        h = a_ref[pl.ds(r, 1), :] * h + b_ref[pl.ds(r, 1), :]
        h_out_ref[pl.ds(r, 1), :] = h
        return h

    return lax.fori_loop(0, t, step, h0, unroll=8)


def _lru_ctx_kernel(u_ref, cw_ref, cb_ref, wa_ref, ba_ref, wx_ref, bx_ref, lam_ref, h_ref,
                    a_s, b_s, hs):
    u = u_ref[0]
    zero8 = jnp.zeros((SUBLANE, u.shape[1]), jnp.float32)
    uc = _conv_tile(u, zero8, zero8, cw_ref[...], cb_ref[...])
    t = u.shape[0]
    for d in range(2):
        _lru_coeffs(uc, wa_ref.at[d], ba_ref[d], wx_ref.at[d], bx_ref[d], lam_ref[d], a_s, b_s)
        h = _scan_rows(a_s, b_s, hs, jnp.zeros((1, u.shape[1]), jnp.float32), t, reverse=(d == 1))
        h_ref[0, d:d + 1, :] = h


def _lru_ctx(u_all, n, c, cw, cb, wa, ba, wx, bx, lam):
    b, _, w = u_all.shape
    full = lambda a: pl.BlockSpec(a.shape, lambda bi: (0,) * a.ndim)
    return pl.pallas_call(
        _lru_ctx_kernel,
        grid=(b,),
        in_specs=[pl.BlockSpec((1, c, w), lambda bi: (bi, n // c, 0)),
                  full(cw), full(cb), full(wa), full(ba), full(wx), full(bx), full(lam)],
        out_specs=pl.BlockSpec((1, 2, w), lambda bi: (bi, 0, 0)),
        out_shape=jax.ShapeDtypeStruct((b, 2, w), jnp.float32),
        scratch_shapes=[pltpu.VMEM((c, w), jnp.float32)] * 3,
        compiler_params=_cparams("parallel"),
        name="lru_ctx",
    )(u_all, cw, cb, wa, ba, wx, bx, lam)


def _lru_fwd_kernel(n_tiles, u_ref, up_ref, un_ref, h0_ref, cw_ref, cb_ref, wa_ref, ba_ref, wx_ref, bx_ref,
                    lam_ref, hf_ref, uc_ref, a_s, b_s, carry):
    tile = pl.program_id(1)

    @pl.when(tile == 0)
    def _():
        carry[...] = h0_ref[0, 0:1, :]

    u = u_ref[0]
    prev = jnp.where(tile == 0, 0.0, up_ref[0])
    nxt = jnp.where(tile == n_tiles - 1, 0.0, un_ref[0])
    uc = _conv_tile(u, prev, nxt, cw_ref[...], cb_ref[...])
    uc_ref[0] = uc
    _lru_coeffs(uc, wa_ref.at[0], ba_ref[0], wx_ref.at[0], bx_ref[0], lam_ref[0], a_s, b_s)
    carry[...] = _scan_rows(a_s, b_s, hf_ref.at[0], carry[...], u.shape[0], reverse=False)


def _lru_bwd_kernel(uc_ref, h0_ref, wa_ref, ba_ref, wx_ref, bx_ref, lam_ref, hf_ref, gate_ref, x_ref, mod_ref,
                    wout_ref, lng_ref, lnb_ref, out_ref, a_s, b_s, hs, carry):
    @pl.when(pl.program_id(1) == 0)
    def _():
        carry[...] = h0_ref[0, 1:2, :]

    uc = uc_ref[0]
    _lru_coeffs(uc, wa_ref.at[1], ba_ref[1], wx_ref.at[1], bx_ref[1], lam_ref[1], a_s, b_s)
    carry[...] = _scan_rows(a_s, b_s, hs, carry[...], uc.shape[0], reverse=True)
    g = gate_ref[0]
    hx = hf_ref[0] + hs[...]
    y = _bf16_dot(hx * (g * _sigmoid(g)), wout_ref[...])
    out_ref[0] = _residual_ln(x_ref[0], y, mod_ref[0, 0][2:3], lng_ref[...], lnb_ref[...])


def _lru_fwd(u_all, n, h0, cw, cb, wa, ba, wx, bx, lam):
    b, _, w = u_all.shape
    t = min(SCAN_TILE, n)
    n_tiles = n // t
    r = t // SUBLANE
    full = lambda a: pl.BlockSpec(a.shape, lambda bi, s: (0,) * a.ndim)
    tile_spec = pl.BlockSpec((1, t, w), lambda bi, s: (bi, s, 0))
    return pl.pallas_call(
        functools.partial(_lru_fwd_kernel, n_tiles),
        grid=(b, n_tiles),
        in_specs=[
            tile_spec,
            pl.BlockSpec((1, SUBLANE, w), lambda bi, s: (bi, jnp.maximum(s * r - 1, 0), 0)),
            pl.BlockSpec((1, SUBLANE, w), lambda bi, s: (bi, (s + 1) * r, 0)),
            pl.BlockSpec((1, 2, w), lambda bi, s: (bi, 0, 0)),
            full(cw), full(cb), full(wa), full(ba), full(wx), full(bx), full(lam),
        ],
        out_specs=[tile_spec, tile_spec],
        out_shape=[jax.ShapeDtypeStruct((b, n, w), jnp.float32)] * 2,
        scratch_shapes=[pltpu.VMEM((t, w), jnp.float32), pltpu.VMEM((t, w), jnp.float32),
                        pltpu.VMEM((1, w), jnp.float32)],
        compiler_params=_cparams("parallel", "arbitrary"),
        name="lru_fwd",
    )(u_all, u_all, u_all, h0, cw, cb, wa, ba, wx, bx, lam)


def _lru_bwd_out(uc, h0, wa, ba, wx, bx, lam, hf, gate, xc, modsel, wout, ln_g, ln_b):
    b, n, w = uc.shape
    d = xc.shape[2]
    t = min(SCAN_TILE, n)
    n_tiles = n // t
    full = lambda a: pl.BlockSpec(a.shape, lambda bi, s: (0,) * a.ndim)
    tile_spec = lambda width: pl.BlockSpec((1, t, width), lambda bi, s: (bi, n_tiles - 1 - s, 0))
    return pl.pallas_call(
        _lru_bwd_kernel,
        grid=(b, n_tiles),
        in_specs=[
            tile_spec(w),
            pl.BlockSpec((1, 2, w), lambda bi, s: (bi, 0, 0)),
            full(wa), full(ba), full(wx), full(bx), full(lam),
            tile_spec(w), tile_spec(w), tile_spec(d),
            pl.BlockSpec((1, 1, 3, d), lambda bi, s: (bi, 0, 0, 0)),
            full(wout), full(ln_g), full(ln_b),
        ],
        out_specs=tile_spec(d),
        out_shape=jax.ShapeDtypeStruct((b, n, d), jnp.float32),
        scratch_shapes=[pltpu.VMEM((t, w), jnp.float32)] * 3 + [pltpu.VMEM((1, w), jnp.float32)],
        compiler_params=_cparams("parallel", "arbitrary"),
        name="lru_bwd_out",
    )(uc, h0, wa, ba, wx, bx, lam, hf, gate, xc, modsel, wout, ln_g, ln_b)


def _rope_tables(n, c):
    t = jnp.arange(n)
    rows = (t // GRID_W).astype(jnp.float32)
    cols = (t % GRID_W).astype(jnp.float32)

    def ang(rot_dim):
        n_freq = rot_dim // 4
        freqs = ROPE_THETA ** (-jnp.arange(n_freq, dtype=jnp.float32) / n_freq)
        return jnp.concatenate([rows[:, None] * freqs, cols[:, None] * freqs], -1)

    aa, ab = ang(MLA_ROPE), ang(DIFF_QK)
    one = jnp.ones((n, MLA_NOPE), jnp.float32)
    zero = jnp.zeros((n, MLA_NOPE), jnp.float32)
    pad1 = jnp.ones((n, LANE - MLA_QK), jnp.float32)
    pad0 = jnp.zeros((n, LANE - MLA_QK), jnp.float32)
    ca = jnp.concatenate([one, jnp.cos(aa), jnp.cos(aa), pad1], 1)
    sa = jnp.concatenate([zero, -jnp.sin(aa), jnp.sin(aa), pad0], 1)
    cb = jnp.concatenate([jnp.cos(ab), jnp.cos(ab)] * 2, 1)
    sb = jnp.concatenate([-jnp.sin(ab), jnp.sin(ab)] * 2, 1)
    ident = lambda tab, v: jnp.concatenate([tab, jnp.full((c, LANE), v, jnp.float32)], 0)
    return ident(ca, 1.0), ident(sa, 0.0), ident(cb, 1.0), ident(sb, 0.0)


def _even_weights(w_in, w_uq, w_ukv):
    d = w_in.shape[0]
    bf = jnp.bfloat16
    o = [0, 384, 640, 672, 1184, 1696, 2208, 3232]
    cq, ckv, kr, dq, dk, dv, gate = (w_in[:, o[i]:o[i + 1]] for i in range(7))
    kr_group = jnp.concatenate([jnp.zeros((d, MLA_NOPE), w_in.dtype), kr,
                                jnp.zeros((d, LANE - MLA_QK), w_in.dtype)], 1)
    w1 = jnp.concatenate([cq, ckv, kr_group, dq, dk, dv, gate], 1).astype(bf)
    wuq = jnp.pad(w_uq.reshape(MLA_Q_LORA, MLA_HEADS, MLA_QK),
                  ((0, 0), (0, 0), (0, LANE - MLA_QK))).reshape(MLA_Q_LORA, MLA_HEADS * LANE).astype(bf)
    ukv = w_ukv.reshape(MLA_KV_LORA, MLA_HEADS, MLA_NOPE + MLA_V)
    wuk = jnp.pad(ukv[..., :MLA_NOPE], ((0, 0), (0, 0), (0, LANE - MLA_NOPE))).reshape(MLA_KV_LORA, MLA_HEADS * LANE)
    wuv = ukv[..., MLA_NOPE:].reshape(MLA_KV_LORA, MLA_HEADS * MLA_V)
    wkv = jnp.concatenate([wuk, wuv], 1).astype(bf)
    return w1, wuq, wkv


def _mod_select(mods_l, b):
    d = mods_l.shape[1] // 3
    mx = mods_l[:b].reshape(b, 3, d)
    mc = jnp.broadcast_to(mods_l[b].reshape(1, 3, d), (b, 3, d))
    return jnp.stack([mx, mc], axis=1)


def kernel(x, c, ctx, c_ctx, ada_w, ada_b, post_ln_g, post_ln_b, e_w_in, e_q_norm_g, e_w_uq, e_kv_norm_g, e_w_ukv, e_lam_q1, e_lam_k1, e_lam_q2, e_lam_k2, e_subln_g, e_w_out, o_w_in, o_conv_w, o_conv_b, o_gate_a_w, o_gate_a_b, o_gate_x_w, o_gate_x_b, o_lru_lambda, o_w_out):
    b, n, d = x.shape
    cl = ctx.shape[1]
    assert ada_w.shape[0] == DEPTH and b < SUBLANE
    assert n % Q_TILE == 0 and cl % TOKEN_TILE == 0 and n % cl == 0
    bf = jnp.bfloat16

    cond = jnp.concatenate([c, c_ctx[None], jnp.zeros((SUBLANE - b - 1, d), c.dtype)], 0)
    lamv = jnp.concatenate([e_lam_q1, e_lam_k1, e_lam_q2, e_lam_k2], 0)
    mods, lam = _adaln(cond, ada_w, ada_b, lamv)
    mod0, mod1 = _mod_select(mods[0], b), _mod_select(mods[1], b)

    w1, wuq, wkv = _even_weights(e_w_in[0], e_w_uq[0], e_w_ukv[0])
    tabs = _rope_tables(n, cl)
    qt, ka, vat, dqt, dk, dvt, gate = _even_proj(
        x, ctx, mod0, w1, e_q_norm_g[0][None], wuq, e_kv_norm_g[0][None], wkv, tabs)
    subln_col = e_subln_g[0][:, None]
    oax = _mla_attn(qt, ka, vat, n)
    obx = _diff_attn(dqt, dk, dvt, lam, subln_col, n)
    oac = _mla_attn(qt[:, :, n:], ka[:, n:], vat[:, :, n:], cl)
    obc = _diff_attn(dqt[:, :, n:], dk[:, n:], dvt[:, :, n:], lam, subln_col, cl)
    xc = _even_merge(x, ctx, mod0, oax, oac, obx, obc, gate, e_w_out[0].astype(bf),
                     post_ln_g[0][None], post_ln_b[0][None])

    u_all, gate1 = _odd_proj(xc, mod1, o_w_in[0].astype(bf), n // TOKEN_TILE)
    cw, cb = o_conv_w[0], o_conv_b[0][None]
    wa, wx = o_gate_a_w[0].astype(bf), o_gate_x_w[0].astype(bf)
    w = u_all.shape[2]
    ba = o_gate_a_b[0].reshape(2, 1, w)
    bx = o_gate_x_b[0].reshape(2, 1, w)
    lru_lam = o_lru_lambda[0].reshape(2, 1, w)
    h0 = _lru_ctx(u_all, n, cl, cw, cb, wa, ba, wx, bx, lru_lam)
    hf, uc = _lru_fwd(u_all, n, h0, cw, cb, wa, ba, wx, bx, lru_lam)
    return _lru_bwd_out(uc, h0, wa, ba, wx, bx, lru_lam, hf, gate1, xc, mod1, o_w_out[0].astype(bf),
                        post_ln_g[1][None], post_ln_b[1][None])
```

```python
import functools
import math

import jax
import jax.numpy as jnp
import numpy as np
from jax import lax
from jax.experimental import pallas as pl
from jax.experimental.pallas import tpu as pltpu

GRID_W = 64
ROPE_THETA = 10000.0
LN_EPS = 1e-6
RMS_EPS = 1e-6

MLA_HEADS = 8
MLA_Q_LORA = 384
MLA_KV_LORA = 256
MLA_NOPE = 64
MLA_ROPE = 32
MLA_V = 64
MLA_QK = MLA_NOPE + MLA_ROPE
MLA_SCALE = MLA_QK ** -0.5

DIFF_HEADS = 4
DIFF_QK = 64
DIFF_V = 2 * DIFF_QK
DIFF_SCALE = DIFF_QK ** -0.5

LRU_BLOCKS = 8
LRU_C = 8.0
CONV_W = 4

DEPTH = 2
DEEPNORM_ALPHA = (2 * DEPTH) ** 0.25
LAMBDA_INIT_0 = 0.8 - 0.6 * math.exp(-0.3 * 0)

LANE = 128
SUBLANE = 8
MXU_DEPTH = 256
LOG2E = 1.4426950408889634
NEG_BIG = -1e30

TOKEN_TILE = 256
Q_TILE = 512
KV_CHUNK = 1408
SCAN_TILE = 512
VMEM_LIMIT = 56 * 1024 * 1024

_G_CQ = (0, 384)
_G_CKV = (384, 640)
_G_KR = (640, 768)
_G_DQ = (768, 1280)
_G_DK = (1280, 1792)
_G_DV = (1792, 2304)
_G_GATE = (2304, 3328)
_EVEN_W = 3328


def _cparams(*sem):
    return pltpu.CompilerParams(dimension_semantics=sem, vmem_limit_bytes=VMEM_LIMIT)


def _bf16_dot(a, b):
    return jnp.dot(a.astype(jnp.bfloat16), b.astype(jnp.bfloat16), preferred_element_type=jnp.float32)


def _sigmoid(x):
    return 1.0 / (1.0 + jnp.exp(-x))


def _adaln_kernel(cond_ref, w_ref, b_ref, lamv_ref, mod_ref, lam_ref):
    cond = cond_ref[...]
    h = cond * _sigmoid(cond)
    mod_ref[0] = jnp.dot(h, w_ref[0], preferred_element_type=jnp.float32,
                         precision=lax.Precision.HIGHEST) + b_ref[0]
    lv = lamv_ref[...]
    d1 = jnp.sum(lv[0:1] * lv[1:2], axis=-1, keepdims=True)
    d2 = jnp.sum(lv[2:3] * lv[3:4], axis=-1, keepdims=True)
    lam = jnp.exp(d1) - jnp.exp(d2) + LAMBDA_INIT_0
    lam_ref[...] = jnp.broadcast_to(lam, lam_ref.shape)


def _adaln(cond, ada_w, ada_b, lamv):
    depth, d, d3 = ada_w.shape
    nj = d3 // d
    return pl.pallas_call(
        _adaln_kernel,
        grid=(depth, nj),
        in_specs=[
            pl.BlockSpec((SUBLANE, d), lambda l, j: (0, 0)),
            pl.BlockSpec((1, d, d), lambda l, j: (l, 0, j)),
            pl.BlockSpec((1, 1, d), lambda l, j: (l, 0, j)),
            pl.BlockSpec(lamv.shape, lambda l, j: (0, 0)),
        ],
        out_specs=[
            pl.BlockSpec((1, SUBLANE, d), lambda l, j: (l, 0, j)),
            pl.BlockSpec((SUBLANE, LANE), lambda l, j: (0, 0)),
        ],
        out_shape=[
            jax.ShapeDtypeStruct((depth, SUBLANE, d3), jnp.float32),
            jax.ShapeDtypeStruct((SUBLANE, LANE), jnp.float32),
        ],
        compiler_params=_cparams("arbitrary", "arbitrary"),
        name="adaln",
    )(cond, ada_w, ada_b.reshape(depth, 1, d3), lamv)


def _rms(x, g):
    return x * lax.rsqrt(jnp.mean(x * x, axis=-1, keepdims=True) + RMS_EPS) * g


def _rope_group(x, cos, sin, half, first_half_mask):
    partner = jnp.where(first_half_mask, pltpu.roll(x, LANE - half, 1), pltpu.roll(x, half, 1))
    return x * cos + partner * sin


def _even_proj_kernel(nx, x_ref, c_ref, mod_ref, w1_ref, qg_ref, wuq_ref, kvg_ref, wkv_ref,
                      ca_ref, sa_ref, cb_ref, sb_ref,
                      qt_ref, ka_ref, vat_ref, dqt_ref, dk_ref, dvt_ref, gate_ref):
    i = pl.program_id(1)
    xin = jnp.where(i >= nx, c_ref[0], x_ref[0])
    mod = mod_ref[0, 0]
    xm = (xin * (1.0 + mod[1:2]) + mod[0:1]).astype(w1_ref.dtype)

    def zcols(lo, hi):
        return jnp.dot(xm, w1_ref[:, lo:hi], preferred_element_type=jnp.float32)

    lane = lax.broadcasted_iota(jnp.int32, (1, LANE), 1)
    mla_first = jnp.logical_and(lane >= MLA_NOPE, lane < MLA_NOPE + MLA_ROPE // 2)
    diff_first = (lane % DIFF_QK) < DIFF_QK // 2
    ca, sa, cb, sb = ca_ref[...], sa_ref[...], cb_ref[...], sb_ref[...]

    z_lat = zcols(_G_CQ[0], _G_KR[1])
    z_dq = zcols(*_G_DQ)
    z_dk = zcols(*_G_DK)

    cqn = _rms(z_lat[:, _G_CQ[0]:_G_CQ[1]], qg_ref[...])
    q = _bf16_dot(cqn, wuq_ref[...])
    ckvn = _rms(z_lat[:, _G_CKV[0]:_G_CKV[1]], kvg_ref[...])
    kv = _bf16_dot(ckvn, wkv_ref[...])
    z_dv = zcols(*_G_DV)
    gate_ref[0] = zcols(*_G_GATE)

    q_heads = []
    for h in range(MLA_HEADS):
        qh = _rope_group(q[:, h * LANE:(h + 1) * LANE], ca, sa, MLA_ROPE // 2, mla_first)
        q_heads.append(qh * (MLA_SCALE * LOG2E))
    qt_ref[0] = jnp.concatenate(q_heads, axis=1).T.astype(qt_ref.dtype)

    kr = _rope_group(z_lat[:, _G_KR[0]:_G_KR[1]], ca, sa, MLA_ROPE // 2, mla_first)
    for h in range(MLA_HEADS):
        ka_ref[0, :, h * LANE:(h + 1) * LANE] = (kv[:, h * LANE:(h + 1) * LANE] + kr).astype(ka_ref.dtype)
    vat_ref[0] = kv[:, MLA_HEADS * LANE:].T.astype(vat_ref.dtype)

    dq_heads = []
    for h in range(DIFF_HEADS):
        sl = slice(h * LANE, (h + 1) * LANE)
        dq_heads.append(_rope_group(z_dq[:, sl], cb, sb, DIFF_QK // 2, diff_first) * (DIFF_SCALE * LOG2E))
        dk_ref[0, :, sl] = _rope_group(z_dk[:, sl], cb, sb, DIFF_QK // 2, diff_first).astype(dk_ref.dtype)
    dqt_ref[0] = jnp.concatenate(dq_heads, axis=1).T.astype(dqt_ref.dtype)
    dvt_ref[0] = z_dv.T.astype(dvt_ref.dtype)


def _even_proj(x, ctx, modsel, w1, qg, wuq, kvg, wkv, tabs):
    b, n, d = x.shape
    c = ctx.shape[1]
    t = TOKEN_TILE
    nx, nc = n // t, c // t
    nt = n + c
    full = lambda a: pl.BlockSpec(a.shape, lambda bi, i: (0,) * a.ndim)
    tab_spec = pl.BlockSpec((t, LANE), lambda bi, i: (i, 0))
    tok = lambda w: pl.BlockSpec((1, t, w), lambda bi, i: (bi, i, 0))
    tok_t = lambda w: pl.BlockSpec((1, w, t), lambda bi, i: (bi, 0, i))
    bf = jnp.bfloat16
    return pl.pallas_call(
        functools.partial(_even_proj_kernel, nx),
        grid=(b, nx + nc),
        in_specs=[
            pl.BlockSpec((1, t, d), lambda bi, i: (bi, jnp.minimum(i, nx - 1), 0)),
            pl.BlockSpec((1, t, d), lambda bi, i: (bi, jnp.maximum(i - nx, 0), 0)),
            pl.BlockSpec((1, 1, 3, d), lambda bi, i: (bi, jnp.where(i >= nx, 1, 0), 0, 0)),
            full(w1), full(qg), full(wuq), full(kvg), full(wkv),
            tab_spec, tab_spec, tab_spec, tab_spec,
        ],
        out_specs=[
            tok_t(MLA_HEADS * LANE), tok(MLA_HEADS * LANE), tok_t(MLA_HEADS * MLA_V),
            tok_t(DIFF_HEADS * LANE), tok(DIFF_HEADS * LANE), tok_t(DIFF_HEADS * DIFF_V),
            tok(MLA_HEADS * MLA_V + DIFF_HEADS * DIFF_V),
        ],
        out_shape=[
            jax.ShapeDtypeStruct((b, MLA_HEADS * LANE, nt), bf),
            jax.ShapeDtypeStruct((b, nt, MLA_HEADS * LANE), bf),
            jax.ShapeDtypeStruct((b, MLA_HEADS * MLA_V, nt), bf),
            jax.ShapeDtypeStruct((b, DIFF_HEADS * LANE, nt), bf),
            jax.ShapeDtypeStruct((b, nt, DIFF_HEADS * LANE), bf),
            jax.ShapeDtypeStruct((b, DIFF_HEADS * DIFF_V, nt), bf),
            jax.ShapeDtypeStruct((b, nt, MLA_HEADS * MLA_V + DIFF_HEADS * DIFF_V), jnp.float32),
        ],
        compiler_params=_cparams("parallel", "parallel"),
        name="even_proj",
    )(x, ctx, modsel, w1, qg, wuq, kvg, wkv, *tabs)


def _kv_chunk(nk):
    if nk <= KV_CHUNK:
        return nk
    for m in range(KV_CHUNK // LANE, 0, -1):
        if nk % (m * LANE) == 0 and (nk // (m * LANE)) % 2 == 0:
            return m * LANE
    raise ValueError(f"no even chunking of {nk} keys")


def _attn_sweep(n_chunks, tk, tq, n_qt, q_of, k_ref, vt_ref, s_scr, finish):
    dv = vt_ref.shape[1]

    def scores(t, j, slot):
        k = k_ref[0, pl.ds(pl.multiple_of(j * tk, LANE), tk), :]
        cms = []
        for si, qt in enumerate(q_of(t)):
            s = jnp.dot(k, qt, preferred_element_type=jnp.float32)
            s_scr[slot, si] = s
            cms.append(jnp.max(s, axis=0, keepdims=True))
        return tuple(cms)

    n_ops = s_scr.shape[1]
    ones = jnp.ones((2 * SUBLANE, MXU_DEPTH), vt_ref.dtype)
    init = (jnp.full((1, tq), NEG_BIG, jnp.float32), jnp.zeros((dv + 2 * SUBLANE, tq), jnp.float32)) * n_ops
    blocks = [(r0, min(MXU_DEPTH, tk - r0)) for r0 in range(0, tk, MXU_DEPTH)]

    def step(nxt, j_cur, slot_cur, cms, state):
        off_cur = pl.multiple_of(j_cur * tk, LANE)
        if nxt is not None:
            t_nxt, j_nxt, slot_nxt = nxt
            off_nxt = pl.multiple_of(j_nxt * tk, LANE)
            q_nxt = q_of(t_nxt)
        m_new = [jnp.maximum(state[2 * si], cms[si]) for si in range(n_ops)]
        cm_nxt, pv = [None] * n_ops, [None] * n_ops
        for r0, r in blocks:
            if nxt is not None:
                k = k_ref[0, pl.ds(off_nxt + r0, r), :]
            vt = vt_ref[0, :, pl.ds(off_cur + r0, r)]
            vt_ext = jnp.concatenate([vt, ones[:, :r]], axis=0)
            for si in range(n_ops):
                if nxt is not None:
                    s = jnp.dot(k, q_nxt[si], preferred_element_type=jnp.float32)
                    s_scr[slot_nxt, si, r0:r0 + r] = s
                    cmax = jnp.max(s, axis=0, keepdims=True)
                    cm_nxt[si] = cmax if cm_nxt[si] is None else jnp.maximum(cm_nxt[si], cmax)
                p = jnp.exp2(s_scr[slot_cur, si, r0:r0 + r] - m_new[si]).astype(vt.dtype)
                d = jnp.dot(vt_ext, p, preferred_element_type=jnp.float32)
                pv[si] = d if pv[si] is None else pv[si] + d
        out = []
        for si in range(n_ops):
            alpha = jnp.exp2(state[2 * si] - m_new[si])
            out += [m_new[si], alpha * state[2 * si + 1] + pv[si]]
        return tuple(cm_nxt), tuple(out)

    def result(st):
        return [st[2 * si + 1][:dv] / st[2 * si + 1][dv:dv + 1] for si in range(n_ops)]

    if n_chunks == 1:
        def tile(t, carry):
            finish(t, result(step(None, 0, 0, scores(t, 0, 0), init)[1]))
            return carry
        lax.fori_loop(0, n_qt, tile, 0)
        return

    assert n_chunks % 2 == 0

    def tile(t, cm):
        def pair(i, carry):
            cm_a, st = carry
            cm_b, st = step((t, 2 * i + 1, 1), 2 * i, 0, cm_a, st)
            cm_a, st = step((t, 2 * i + 2, 0), 2 * i + 1, 1, cm_b, st)
            return cm_a, st

        cm_a, st = lax.fori_loop(0, n_chunks // 2 - 1, pair, (cm, init), unroll=True)
        cm_b, st = step((t, n_chunks - 1, 1), n_chunks - 2, 0, cm_a, st)
        cm_next, st = step((jnp.minimum(t + 1, n_qt - 1), 0, 0), n_chunks - 1, 1, cm_b, st)
        finish(t, result(st))
        return cm_next

    lax.fori_loop(0, n_qt, tile, scores(0, 0, 0), unroll=2)


def _q_cols(t, tq):
    return pl.ds(pl.multiple_of(t * tq, LANE), tq)


def _mla_attn_kernel(n_chunks, tk, tq, qt_ref, k_ref, vt_ref, o_ref, s_scr):
    n_ops = s_scr.shape[1]

    def q_of(t):
        return [qt_ref[0, :, _q_cols(t * n_ops + i, tq)] for i in range(n_ops)]

    def finish(t, outs):
        for i in range(n_ops):
            o_ref[0, :, _q_cols(t * n_ops + i, tq)] = outs[i]

    _attn_sweep(n_chunks, tk, tq, qt_ref.shape[2] // (tq * n_ops), q_of, k_ref, vt_ref, s_scr, finish)


def _attn_call(kernel_fn, name, heads, dv, n_ops, qt, k, vt, nq, nk, tail, extra=()):
    b, _, nt = vt.shape
    assert not tail or ((nt - nq) % nq == 0 and (nt - nk) % nk == 0)
    qb = (nt - nq) // nq if tail else 0
    kb = (nt - nk) // nk if tail else 0
    tq = min(Q_TILE, nq)
    tk = _kv_chunk(nk)
    return pl.pallas_call(
        functools.partial(kernel_fn, nk // tk, tk, tq),
        grid=(b, heads),
        in_specs=[
            pl.BlockSpec((1, LANE, nq), lambda bi, h: (bi, h, qb)),
            pl.BlockSpec((1, nk, LANE), lambda bi, h: (bi, kb, h)),
            pl.BlockSpec((1, dv, nk), lambda bi, h: (bi, h, kb)),
        ] + [pl.BlockSpec(a.shape, lambda bi, h: (0, 0)) for a in extra],
        out_specs=pl.BlockSpec((1, dv, nq), lambda bi, h: (bi, h, 0)),
        out_shape=jax.ShapeDtypeStruct((b, heads * dv, nq), jnp.float32),
        scratch_shapes=[pltpu.VMEM((2, n_ops, tk, tq), jnp.float32)],
        compiler_params=_cparams("parallel", "parallel"),
        name=name,
    )(qt, k, vt, *extra)


def _mla_attn(qt, ka, vat, nq, nk, tail=False):
    n_ops = 2 if nq % (2 * Q_TILE) == 0 else 1
    return _attn_call(_mla_attn_kernel, "mla_attn", MLA_HEADS, MLA_V, n_ops, qt, ka, vat, nq, nk, tail)


def _diff_attn_kernel(n_chunks, tk, tq, qt_ref, k_ref, vt_ref, lam_ref, g_ref, o_ref, s_scr):
    row = lax.broadcasted_iota(jnp.int32, (LANE, 1), 0)

    def q_of(t):
        qt = qt_ref[0, :, _q_cols(t, tq)]
        zero = jnp.zeros_like(qt)
        return [jnp.where(row < DIFF_QK, qt, zero), jnp.where(row >= DIFF_QK, qt, zero)]

    def finish(t, outs):
        o = outs[0] - lam_ref[0:1, 0:1] * outs[1]
        o = o * lax.rsqrt(jnp.mean(o * o, axis=0, keepdims=True) + RMS_EPS) * g_ref[...]
        o_ref[0, :, _q_cols(t, tq)] = o * (1.0 - LAMBDA_INIT_0)

    _attn_sweep(n_chunks, tk, tq, qt_ref.shape[2] // tq, q_of, k_ref, vt_ref, s_scr, finish)


def _diff_attn(dqt, dk, dvt, lam, subln_col, nq, nk, tail=False):
    return _attn_call(_diff_attn_kernel, "diff_attn", DIFF_HEADS, DIFF_V, 2, dqt, dk, dvt, nq, nk, tail,
                      extra=(lam, subln_col))


def _residual_ln(xin, y, gate_vec, ln_g, ln_b):
    r = DEEPNORM_ALPHA * xin + gate_vec * y
    mu = jnp.mean(r, axis=-1, keepdims=True)
    rc = r - mu
    var = jnp.mean(rc * rc, axis=-1, keepdims=True)
    return rc * lax.rsqrt(var + LN_EPS) * ln_g + ln_b


def _even_merge_kernel(nx, x_ref, c_ref, mod_ref, oax_ref, oac_ref, obx_ref, obc_ref, gate_ref,
                       wout_ref, lng_ref, lnb_ref, out_ref):
    i = pl.program_id(1)
    is_ctx = i >= nx
    xin = jnp.where(is_ctx, c_ref[0], x_ref[0])
    oa = jnp.where(is_ctx, oac_ref[0], oax_ref[0]).T
    ob = jnp.where(is_ctx, obc_ref[0], obx_ref[0]).T
    o = jnp.concatenate([oa, ob], axis=1)
    g = gate_ref[0]
    y = _bf16_dot(o * (g * _sigmoid(g)), wout_ref[...])
    out_ref[0] = _residual_ln(xin, y, mod_ref[0, 0][2:3], lng_ref[...], lnb_ref[...])


def _even_merge(x, ctx, modsel, oax, oac, obx, obc, gate, wout, ln_g, ln_b):
    b, n, d = x.shape
    c = ctx.shape[1]
    t = TOKEN_TILE
    nx, nc = n // t, c // t
    wa, wb = oax.shape[1], obx.shape[1]
    full = lambda a: pl.BlockSpec(a.shape, lambda bi, i: (0,) * a.ndim)
    xi = lambda bi, i: (bi, jnp.minimum(i, nx - 1), 0)
    ci = lambda bi, i: (bi, jnp.maximum(i - nx, 0), 0)
    xit = lambda bi, i: (bi, 0, jnp.minimum(i, nx - 1))
    cit = lambda bi, i: (bi, 0, jnp.maximum(i - nx, 0))
    return pl.pallas_call(
        functools.partial(_even_merge_kernel, nx),
        grid=(b, nx + nc),
        in_specs=[
            pl.BlockSpec((1, t, d), xi),
            pl.BlockSpec((1, t, d), ci),
            pl.BlockSpec((1, 1, 3, d), lambda bi, i: (bi, jnp.where(i >= nx, 1, 0), 0, 0)),
            pl.BlockSpec((1, wa, t), xit), pl.BlockSpec((1, wa, t), cit),
            pl.BlockSpec((1, wb, t), xit), pl.BlockSpec((1, wb, t), cit),
            pl.BlockSpec((1, t, wa + wb), lambda bi, i: (bi, i, 0)),
            full(wout), full(ln_g), full(ln_b),
        ],
        out_specs=pl.BlockSpec((1, t, d), lambda bi, i: (bi, i, 0)),
        out_shape=jax.ShapeDtypeStruct((b, n + c, d), jnp.float32),
        compiler_params=_cparams("parallel", "parallel"),
        name="even_merge",
    )(x, ctx, modsel, oax, oac, obx, obc, gate, wout, ln_g, ln_b)


def _odd_proj_kernel(xc_ref, mod_ref, w_ref, u_ref, gate_ref):
    mod = mod_ref[0, 0]
    xm = xc_ref[0] * (1.0 + mod[1:2]) + mod[0:1]
    z = _bf16_dot(xm, w_ref[...])
    w = u_ref.shape[2]
    u_ref[0] = z[:, :w]
    gate_ref[0] = z[:, w:]


def _odd_proj(xc, modsel, w_in, nx):
    b, nt, d = xc.shape
    t = TOKEN_TILE
    w = w_in.shape[1] // 2
    return pl.pallas_call(
        _odd_proj_kernel,
        grid=(b, nt // t),
        in_specs=[
            pl.BlockSpec((1, t, d), lambda bi, i: (bi, i, 0)),
            pl.BlockSpec((1, 1, 3, d), lambda bi, i: (bi, jnp.where(i >= nx, 1, 0), 0, 0)),
            pl.BlockSpec(w_in.shape, lambda bi, i: (0, 0)),
        ],
        out_specs=[pl.BlockSpec((1, t, w), lambda bi, i: (bi, i, 0))] * 2,
        out_shape=[jax.ShapeDtypeStruct((b, nt, w), jnp.float32)] * 2,
        compiler_params=_cparams("parallel", "parallel"),
        name="odd_proj",
    )(xc, modsel, w_in)


def _conv_tile(u, prev, nxt, cw, cb):
    t = u.shape[0]
    row = lax.broadcasted_iota(jnp.int32, (SUBLANE, 1), 0)
    r1, r2, rp = pltpu.roll(u, 1, 0), pltpu.roll(u, 2, 0), pltpu.roll(u, t - 1, 0)
    m1 = jnp.concatenate([jnp.where(row == 0, prev[7:8], r1[:SUBLANE]), r1[SUBLANE:]], 0)
    m2 = jnp.concatenate([jnp.where(row == 0, prev[6:7], jnp.where(row == 1, prev[7:8], r2[:SUBLANE])),
                          r2[SUBLANE:]], 0)
    p1 = jnp.concatenate([rp[:t - SUBLANE], jnp.where(row == SUBLANE - 1, nxt[0:1], rp[t - SUBLANE:])], 0)
    return cw[0:1] * m2 + cw[1:2] * m1 + cw[2:3] * u + cw[3:4] * p1 + cb


def _lru_coeffs(uc, wa_ref, ba, wx_ref, bx, lam, a_ref, b_ref):
    neg_log_a1 = LRU_C * jnp.log1p(jnp.exp(-lam))
    for k in range(LRU_BLOCKS):
        sl = slice(k * LANE, (k + 1) * LANE)
        ub = uc[:, sl]
        r = _sigmoid(_bf16_dot(ub, wa_ref[k]) + ba[:, sl])
        ig = _sigmoid(_bf16_dot(ub, wx_ref[k]) + bx[:, sl])
        a = jnp.exp2(r * (-LOG2E * neg_log_a1[:, sl]))
        a_ref[:, sl] = a
        y = jnp.tanh(r * neg_log_a1[:, sl]) * (a * a + 1.0)
        root = jnp.where(y > 0.0, y * lax.rsqrt(y), 0.0)
        b_ref[:, sl] = root * (ig * ub)


def _scan_rows(a_ref, b_ref, h_out_ref, h0, t, reverse):
    def step(s, h):
        r = (t - 1 - s) if reverse else s
        h = a_ref[pl.ds(r, 1), :] * h + b_ref[pl.ds(r, 1), :]
        h_out_ref[pl.ds(r, 1), :] = h
        return h

    return lax.fori_loop(0, t, step, h0, unroll=8)


def _lru_ctx_kernel(u_ref, cw_ref, cb_ref, wa_ref, ba_ref, wx_ref, bx_ref, lam_ref, h_ref,
                    a_s, b_s, hs):
    u = u_ref[0]
    zero8 = jnp.zeros((SUBLANE, u.shape[1]), jnp.float32)
    uc = _conv_tile(u, zero8, zero8, cw_ref[...], cb_ref[...])
    t = u.shape[0]
    for d in range(2):
        _lru_coeffs(uc, wa_ref.at[d], ba_ref[d], wx_ref.at[d], bx_ref[d], lam_ref[d], a_s, b_s)
        h = _scan_rows(a_s, b_s, hs, jnp.zeros((1, u.shape[1]), jnp.float32), t, reverse=(d == 1))
        h_ref[0, d:d + 1, :] = h


def _lru_ctx(u_all, n, c, cw, cb, wa, ba, wx, bx, lam):
    b, _, w = u_all.shape
    full = lambda a: pl.BlockSpec(a.shape, lambda bi: (0,) * a.ndim)
    return pl.pallas_call(
        _lru_ctx_kernel,
        grid=(b,),
        in_specs=[pl.BlockSpec((1, c, w), lambda bi: (bi, n // c, 0)),
                  full(cw), full(cb), full(wa), full(ba), full(wx), full(bx), full(lam)],
        out_specs=pl.BlockSpec((1, 2, w), lambda bi: (bi, 0, 0)),
        out_shape=jax.ShapeDtypeStruct((b, 2, w), jnp.float32),
        scratch_shapes=[pltpu.VMEM((c, w), jnp.float32)] * 3,
        compiler_params=_cparams("parallel"),
        name="lru_ctx",
    )(u_all, cw, cb, wa, ba, wx, bx, lam)


def _lru_fwd_kernel(n_tiles, u_ref, up_ref, un_ref, h0_ref, cw_ref, cb_ref, wa_ref, ba_ref, wx_ref, bx_ref,
                    lam_ref, hf_ref, uc_ref, a_s, b_s, carry):
    tile = pl.program_id(1)

    @pl.when(tile == 0)
    def _():
        carry[...] = h0_ref[0, 0:1, :]

    u = u_ref[0]
    prev = jnp.where(tile == 0, 0.0, up_ref[0])
    nxt = jnp.where(tile == n_tiles - 1, 0.0, un_ref[0])
    uc = _conv_tile(u, prev, nxt, cw_ref[...], cb_ref[...])
    uc_ref[0] = uc
    _lru_coeffs(uc, wa_ref.at[0], ba_ref[0], wx_ref.at[0], bx_ref[0], lam_ref[0], a_s, b_s)
    carry[...] = _scan_rows(a_s, b_s, hf_ref.at[0], carry[...], u.shape[0], reverse=False)


def _lru_bwd_kernel(uc_ref, h0_ref, wa_ref, ba_ref, wx_ref, bx_ref, lam_ref, hf_ref, gate_ref, x_ref, mod_ref,
                    wout_ref, lng_ref, lnb_ref, out_ref, a_s, b_s, hs, carry):
    @pl.when(pl.program_id(1) == 0)
    def _():
        carry[...] = h0_ref[0, 1:2, :]

    uc = uc_ref[0]
    _lru_coeffs(uc, wa_ref.at[1], ba_ref[1], wx_ref.at[1], bx_ref[1], lam_ref[1], a_s, b_s)
    carry[...] = _scan_rows(a_s, b_s, hs, carry[...], uc.shape[0], reverse=True)
    g = gate_ref[0]
    hx = hf_ref[0] + hs[...]
    y = _bf16_dot(hx * (g * _sigmoid(g)), wout_ref[...])
    out_ref[0] = _residual_ln(x_ref[0], y, mod_ref[0, 0][2:3], lng_ref[...], lnb_ref[...])


def _lru_fwd(u_all, n, h0, cw, cb, wa, ba, wx, bx, lam):
    b, _, w = u_all.shape
    t = min(SCAN_TILE, n)
    n_tiles = n // t
    r = t // SUBLANE
    full = lambda a: pl.BlockSpec(a.shape, lambda bi, s: (0,) * a.ndim)
    tile_spec = pl.BlockSpec((1, t, w), lambda bi, s: (bi, s, 0))
    return pl.pallas_call(
        functools.partial(_lru_fwd_kernel, n_tiles),
        grid=(b, n_tiles),
        in_specs=[
            tile_spec,
            pl.BlockSpec((1, SUBLANE, w), lambda bi, s: (bi, jnp.maximum(s * r - 1, 0), 0)),
            pl.BlockSpec((1, SUBLANE, w), lambda bi, s: (bi, (s + 1) * r, 0)),
            pl.BlockSpec((1, 2, w), lambda bi, s: (bi, 0, 0)),
            full(cw), full(cb), full(wa), full(ba), full(wx), full(bx), full(lam),
        ],
        out_specs=[tile_spec, tile_spec],
        out_shape=[jax.ShapeDtypeStruct((b, n, w), jnp.float32)] * 2,
        scratch_shapes=[pltpu.VMEM((t, w), jnp.float32), pltpu.VMEM((t, w), jnp.float32),
                        pltpu.VMEM((1, w), jnp.float32)],
        compiler_params=_cparams("parallel", "arbitrary"),
        name="lru_fwd",
    )(u_all, u_all, u_all, h0, cw, cb, wa, ba, wx, bx, lam)


def _lru_bwd_out(uc, h0, wa, ba, wx, bx, lam, hf, gate, xc, modsel, wout, ln_g, ln_b):
    b, n, w = uc.shape
    d = xc.shape[2]
    t = min(SCAN_TILE, n)
    n_tiles = n // t
    full = lambda a: pl.BlockSpec(a.shape, lambda bi, s: (0,) * a.ndim)
    tile_spec = lambda width: pl.BlockSpec((1, t, width), lambda bi, s: (bi, n_tiles - 1 - s, 0))
    return pl.pallas_call(
        _lru_bwd_kernel,
        grid=(b, n_tiles),
        in_specs=[
            tile_spec(w),
            pl.BlockSpec((1, 2, w), lambda bi, s: (bi, 0, 0)),
            full(wa), full(ba), full(wx), full(bx), full(lam),
            tile_spec(w), tile_spec(w), tile_spec(d),
            pl.BlockSpec((1, 1, 3, d), lambda bi, s: (bi, 0, 0, 0)),
            full(wout), full(ln_g), full(ln_b),
        ],
        out_specs=tile_spec(d),
        out_shape=jax.ShapeDtypeStruct((b, n, d), jnp.float32),
        scratch_shapes=[pltpu.VMEM((t, w), jnp.float32)] * 3 + [pltpu.VMEM((1, w), jnp.float32)],
        compiler_params=_cparams("parallel", "arbitrary"),
        name="lru_bwd_out",
    )(uc, h0, wa, ba, wx, bx, lam, hf, gate, xc, modsel, wout, ln_g, ln_b)


def _rope_tables(n, c):
    t = np.arange(n)
    rows = (t // GRID_W).astype(np.float32)
    cols = (t % GRID_W).astype(np.float32)

    def ang(rot_dim):
        n_freq = rot_dim // 4
        freqs = np.float32(ROPE_THETA) ** (-np.arange(n_freq, dtype=np.float32) / np.float32(n_freq))
        a = np.concatenate([rows[:, None] * freqs, cols[:, None] * freqs], -1)
        return np.cos(a.astype(np.float64)), np.sin(a.astype(np.float64))

    (cos_a, sin_a), (cos_b, sin_b) = ang(MLA_ROPE), ang(DIFF_QK)
    one = np.ones((n, MLA_NOPE))
    zero = np.zeros((n, MLA_NOPE))
    pad1 = np.ones((n, LANE - MLA_QK))
    pad0 = np.zeros((n, LANE - MLA_QK))
    ca = np.concatenate([one, cos_a, cos_a, pad1], 1)
    sa = np.concatenate([zero, -sin_a, sin_a, pad0], 1)
    cb = np.concatenate([cos_b, cos_b] * 2, 1)
    sb = np.concatenate([-sin_b, sin_b] * 2, 1)
    ident = lambda tab, v: jnp.asarray(np.concatenate([tab, np.full((c, LANE), v)], 0), jnp.float32)
    return ident(ca, 1.0), ident(sa, 0.0), ident(cb, 1.0), ident(sb, 0.0)


def _even_weights(w_in, w_uq, w_ukv):
    d = w_in.shape[0]
    bf = jnp.bfloat16
    o = [0, 384, 640, 672, 1184, 1696, 2208, 3232]
    cq, ckv, kr, dq, dk, dv, gate = (w_in[:, o[i]:o[i + 1]] for i in range(7))
    kr_group = jnp.concatenate([jnp.zeros((d, MLA_NOPE), w_in.dtype), kr,
                                jnp.zeros((d, LANE - MLA_QK), w_in.dtype)], 1)
    w1 = jnp.concatenate([cq, ckv, kr_group, dq, dk, dv, gate], 1).astype(bf)
    wuq = jnp.pad(w_uq.reshape(MLA_Q_LORA, MLA_HEADS, MLA_QK),
                  ((0, 0), (0, 0), (0, LANE - MLA_QK))).reshape(MLA_Q_LORA, MLA_HEADS * LANE).astype(bf)
    ukv = w_ukv.reshape(MLA_KV_LORA, MLA_HEADS, MLA_NOPE + MLA_V)
    wuk = jnp.pad(ukv[..., :MLA_NOPE], ((0, 0), (0, 0), (0, LANE - MLA_NOPE))).reshape(MLA_KV_LORA, MLA_HEADS * LANE)
    wuv = ukv[..., MLA_NOPE:].reshape(MLA_KV_LORA, MLA_HEADS * MLA_V)
    wkv = jnp.concatenate([wuk, wuv], 1).astype(bf)
    return w1, wuq, wkv


def _mod_select(mods_l, b):
    d = mods_l.shape[1] // 3
    mx = mods_l[:b].reshape(b, 3, d)
    mc = jnp.broadcast_to(mods_l[b].reshape(1, 3, d), (b, 3, d))
    return jnp.stack([mx, mc], axis=1)


def kernel(x, c, ctx, c_ctx, ada_w, ada_b, post_ln_g, post_ln_b, e_w_in, e_q_norm_g, e_w_uq, e_kv_norm_g, e_w_ukv, e_lam_q1, e_lam_k1, e_lam_q2, e_lam_k2, e_subln_g, e_w_out, o_w_in, o_conv_w, o_conv_b, o_gate_a_w, o_gate_a_b, o_gate_x_w, o_gate_x_b, o_lru_lambda, o_w_out):
    b, n, d = x.shape
    cl = ctx.shape[1]
    assert ada_w.shape[0] == DEPTH and b < SUBLANE
    assert n % Q_TILE == 0 and cl % TOKEN_TILE == 0 and n % cl == 0
    bf = jnp.bfloat16

    cond = jnp.concatenate([c, c_ctx[None], jnp.zeros((SUBLANE - b - 1, d), c.dtype)], 0)
    lamv = jnp.concatenate([e_lam_q1, e_lam_k1, e_lam_q2, e_lam_k2], 0)
    mods, lam = _adaln(cond, ada_w, ada_b, lamv)
    mod0, mod1 = _mod_select(mods[0], b), _mod_select(mods[1], b)

    w1, wuq, wkv = _even_weights(e_w_in[0], e_w_uq[0], e_w_ukv[0])
    tabs = _rope_tables(n, cl)
    qt, ka, vat, dqt, dk, dvt, gate = _even_proj(
        x, ctx, mod0, w1, e_q_norm_g[0][None], wuq, e_kv_norm_g[0][None], wkv, tabs)
    subln_col = e_subln_g[0][:, None]
    oax = _mla_attn(qt, ka, vat, n, n + cl)
    obx = _diff_attn(dqt, dk, dvt, lam, subln_col, n, n + cl)
    oac = _mla_attn(qt, ka, vat, cl, cl, tail=True)
    obc = _diff_attn(dqt, dk, dvt, lam, subln_col, cl, cl, tail=True)
    xc = _even_merge(x, ctx, mod0, oax, oac, obx, obc, gate, e_w_out[0].astype(bf),
                     post_ln_g[0][None], post_ln_b[0][None])

    u_all, gate1 = _odd_proj(xc, mod1, o_w_in[0].astype(bf), n // TOKEN_TILE)
    cw, cb = o_conv_w[0], o_conv_b[0][None]
    wa, wx = o_gate_a_w[0].astype(bf), o_gate_x_w[0].astype(bf)
    w = u_all.shape[2]
    ba = o_gate_a_b[0].reshape(2, 1, w)
    bx = o_gate_x_b[0].reshape(2, 1, w)
    lru_lam = o_lru_lambda[0].reshape(2, 1, w)
    h0 = _lru_ctx(u_all, n, cl, cw, cb, wa, ba, wx, bx, lru_lam)
    hf, uc = _lru_fwd(u_all, n, h0, cw, cb, wa, ba, wx, bx, lru_lam)
    return _lru_bwd_out(uc, h0, wa, ba, wx, bx, lru_lam, hf, gate1, xc, mod1, o_w_out[0].astype(bf),
                        post_ln_g[1][None], post_ln_b[1][None])
```

```python
import functools
import math

import jax
import jax.numpy as jnp
import numpy as np
from jax import lax
from jax.experimental import pallas as pl
from jax.experimental.pallas import tpu as pltpu

GRID_W = 64
ROPE_THETA = 10000.0
LN_EPS = 1e-6
RMS_EPS = 1e-6

MLA_HEADS = 8
MLA_Q_LORA = 384
MLA_KV_LORA = 256
MLA_NOPE = 64
MLA_ROPE = 32
MLA_V = 64
MLA_QK = MLA_NOPE + MLA_ROPE
MLA_SCALE = MLA_QK ** -0.5

DIFF_HEADS = 4
DIFF_QK = 64
DIFF_V = 2 * DIFF_QK
DIFF_SCALE = DIFF_QK ** -0.5

LRU_BLOCKS = 8
LRU_C = 8.0
CONV_W = 4

DEPTH = 2
DEEPNORM_ALPHA = (2 * DEPTH) ** 0.25
LAMBDA_INIT_0 = 0.8 - 0.6 * math.exp(-0.3 * 0)

LANE = 128
SUBLANE = 8
MXU_DEPTH = 256
LOG2E = 1.4426950408889634
NEG_BIG = -1e30

TOKEN_TILE = 256
Q_TILE = 512
KV_CHUNK = 1408
SCAN_TILE = 512
VMEM_LIMIT = 56 * 1024 * 1024

_G_CQ = (0, 384)
_G_CKV = (384, 640)
_G_KR = (640, 768)
_G_DQ = (768, 1280)
_G_DK = (1280, 1792)
_G_DV = (1792, 2304)
_G_GATE = (2304, 3328)
_EVEN_W = 3328


def _cparams(*sem):
    return pltpu.CompilerParams(dimension_semantics=sem, vmem_limit_bytes=VMEM_LIMIT)


def _bf16_dot(a, b):
    return jnp.dot(a.astype(jnp.bfloat16), b.astype(jnp.bfloat16), preferred_element_type=jnp.float32)


def _sigmoid(x):
    return 1.0 / (1.0 + jnp.exp(-x))


def _adaln_kernel(cond_ref, w_ref, b_ref, lamv_ref, mod_ref, lam_ref):
    cond = cond_ref[...]
    h = cond * _sigmoid(cond)
    mod_ref[0] = jnp.dot(h, w_ref[0], preferred_element_type=jnp.float32,
                         precision=lax.Precision.HIGHEST) + b_ref[0]
    lv = lamv_ref[...]
    d1 = jnp.sum(lv[0:1] * lv[1:2], axis=-1, keepdims=True)
    d2 = jnp.sum(lv[2:3] * lv[3:4], axis=-1, keepdims=True)
    lam = jnp.exp(d1) - jnp.exp(d2) + LAMBDA_INIT_0
    lam_ref[...] = jnp.broadcast_to(lam, lam_ref.shape)


def _adaln(cond, ada_w, ada_b, lamv):
    depth, d, d3 = ada_w.shape
    nj = d3 // d
    return pl.pallas_call(
        _adaln_kernel,
        grid=(depth, nj),
        in_specs=[
            pl.BlockSpec((SUBLANE, d), lambda l, j: (0, 0)),
            pl.BlockSpec((1, d, d), lambda l, j: (l, 0, j)),
            pl.BlockSpec((1, 1, d), lambda l, j: (l, 0, j)),
            pl.BlockSpec(lamv.shape, lambda l, j: (0, 0)),
        ],
        out_specs=[
            pl.BlockSpec((1, SUBLANE, d), lambda l, j: (l, 0, j)),
            pl.BlockSpec((SUBLANE, LANE), lambda l, j: (0, 0)),
        ],
        out_shape=[
            jax.ShapeDtypeStruct((depth, SUBLANE, d3), jnp.float32),
            jax.ShapeDtypeStruct((SUBLANE, LANE), jnp.float32),
        ],
        compiler_params=_cparams("arbitrary", "arbitrary"),
        name="adaln",
    )(cond, ada_w, ada_b.reshape(depth, 1, d3), lamv)


def _rms(x, g):
    return x * lax.rsqrt(jnp.mean(x * x, axis=-1, keepdims=True) + RMS_EPS) * g


def _rope_group(x, cos, sin, half, first_half_mask):
    partner = jnp.where(first_half_mask, pltpu.roll(x, LANE - half, 1), pltpu.roll(x, half, 1))
    return x * cos + partner * sin


def _even_proj_kernel(nx, x_ref, c_ref, mod_ref, w1_ref, qg_ref, wuq_ref, kvg_ref, wkv_ref,
                      ca_ref, sa_ref, cb_ref, sb_ref,
                      qt_ref, ka_ref, vat_ref, dqt_ref, dk_ref, dvt_ref, gate_ref):
    i = pl.program_id(1)
    xin = jnp.where(i >= nx, c_ref[0], x_ref[0])
    mod = mod_ref[0, 0]
    xm = (xin * (1.0 + mod[1:2]) + mod[0:1]).astype(w1_ref.dtype)

    def zcols(lo, hi):
        return jnp.dot(xm, w1_ref[:, lo:hi], preferred_element_type=jnp.float32)

    lane = lax.broadcasted_iota(jnp.int32, (1, LANE), 1)
    mla_first = jnp.logical_and(lane >= MLA_NOPE, lane < MLA_NOPE + MLA_ROPE // 2)
    diff_first = (lane % DIFF_QK) < DIFF_QK // 2
    ca, sa, cb, sb = ca_ref[...], sa_ref[...], cb_ref[...], sb_ref[...]

    z_lat = zcols(_G_CQ[0], _G_KR[1])
    z_dq = zcols(*_G_DQ)
    z_dk = zcols(*_G_DK)

    cqn = _rms(z_lat[:, _G_CQ[0]:_G_CQ[1]], qg_ref[...])
    q = _bf16_dot(cqn, wuq_ref[...])
    ckvn = _rms(z_lat[:, _G_CKV[0]:_G_CKV[1]], kvg_ref[...])
    kv = _bf16_dot(ckvn, wkv_ref[...])
    z_dv = zcols(*_G_DV)
    gate_ref[0] = zcols(*_G_GATE)

    q_heads = []
    for h in range(MLA_HEADS):
        qh = _rope_group(q[:, h * LANE:(h + 1) * LANE], ca, sa, MLA_ROPE // 2, mla_first)
        q_heads.append(qh * (MLA_SCALE * LOG2E))
    qt_ref[0] = jnp.concatenate(q_heads, axis=1).T.astype(qt_ref.dtype)

    kr = _rope_group(z_lat[:, _G_KR[0]:_G_KR[1]], ca, sa, MLA_ROPE // 2, mla_first)
    for h in range(MLA_HEADS):
        ka_ref[0, :, h * LANE:(h + 1) * LANE] = (kv[:, h * LANE:(h + 1) * LANE] + kr).astype(ka_ref.dtype)
    vat_ref[0] = kv[:, MLA_HEADS * LANE:].T.astype(vat_ref.dtype)

    dq_heads = []
    for h in range(DIFF_HEADS):
        sl = slice(h * LANE, (h + 1) * LANE)
        dq_heads.append(_rope_group(z_dq[:, sl], cb, sb, DIFF_QK // 2, diff_first) * (DIFF_SCALE * LOG2E))
        dk_ref[0, :, sl] = _rope_group(z_dk[:, sl], cb, sb, DIFF_QK // 2, diff_first).astype(dk_ref.dtype)
    dqt_ref[0] = jnp.concatenate(dq_heads, axis=1).T.astype(dqt_ref.dtype)
    dvt_ref[0] = z_dv.T.astype(dvt_ref.dtype)


def _even_proj(x, ctx, modsel, w1, qg, wuq, kvg, wkv, tabs):
    b, n, d = x.shape
    c = ctx.shape[1]
    t = TOKEN_TILE
    nx, nc = n // t, c // t
    nt = n + c
    full = lambda a: pl.BlockSpec(a.shape, lambda bi, i: (0,) * a.ndim)
    tab_spec = pl.BlockSpec((t, LANE), lambda bi, i: (i, 0))
    tok = lambda w: pl.BlockSpec((1, t, w), lambda bi, i: (bi, i, 0))
    tok_t = lambda w: pl.BlockSpec((1, w, t), lambda bi, i: (bi, 0, i))
    bf = jnp.bfloat16
    return pl.pallas_call(
        functools.partial(_even_proj_kernel, nx),
        grid=(b, nx + nc),
        in_specs=[
            pl.BlockSpec((1, t, d), lambda bi, i: (bi, jnp.minimum(i, nx - 1), 0)),
            pl.BlockSpec((1, t, d), lambda bi, i: (bi, jnp.maximum(i - nx, 0), 0)),
            pl.BlockSpec((1, 1, 3, d), lambda bi, i: (bi, jnp.where(i >= nx, 1, 0), 0, 0)),
            full(w1), full(qg), full(wuq), full(kvg), full(wkv),
            tab_spec, tab_spec, tab_spec, tab_spec,
        ],
        out_specs=[
            tok_t(MLA_HEADS * LANE), tok(MLA_HEADS * LANE), tok_t(MLA_HEADS * MLA_V),
            tok_t(DIFF_HEADS * LANE), tok(DIFF_HEADS * LANE), tok_t(DIFF_HEADS * DIFF_V),
            tok(MLA_HEADS * MLA_V + DIFF_HEADS * DIFF_V),
        ],
        out_shape=[
            jax.ShapeDtypeStruct((b, MLA_HEADS * LANE, nt), bf),
            jax.ShapeDtypeStruct((b, nt, MLA_HEADS * LANE), bf),
            jax.ShapeDtypeStruct((b, MLA_HEADS * MLA_V, nt), bf),
            jax.ShapeDtypeStruct((b, DIFF_HEADS * LANE, nt), bf),
            jax.ShapeDtypeStruct((b, nt, DIFF_HEADS * LANE), bf),
            jax.ShapeDtypeStruct((b, DIFF_HEADS * DIFF_V, nt), bf),
            jax.ShapeDtypeStruct((b, nt, MLA_HEADS * MLA_V + DIFF_HEADS * DIFF_V), jnp.float32),
        ],
        compiler_params=_cparams("parallel", "parallel"),
        name="even_proj",
    )(x, ctx, modsel, w1, qg, wuq, kvg, wkv, *tabs)


def _kv_chunk(nk):
    if nk <= KV_CHUNK:
        return nk
    for m in range(KV_CHUNK // LANE, 0, -1):
        if nk % (m * LANE) == 0 and (nk // (m * LANE)) % 2 == 0:
            return m * LANE
    raise ValueError(f"no even chunking of {nk} keys")


def _attn_sweep(n_chunks, tk, tq, n_qt, q_of, k_ref, vt_ref, s_scr, finish):
    dv = vt_ref.shape[1]

    def scores(t, j, slot):
        k = k_ref[0, pl.ds(pl.multiple_of(j * tk, LANE), tk), :]
        cms = []
        for si, qt in enumerate(q_of(t)):
            s = jnp.dot(k, qt, preferred_element_type=jnp.float32)
            s_scr[slot, si] = s
            cms.append(jnp.max(s, axis=0, keepdims=True))
        return tuple(cms)

    n_ops = s_scr.shape[1]
    ones = jnp.ones((2 * SUBLANE, MXU_DEPTH), vt_ref.dtype)
    init = (jnp.full((1, tq), NEG_BIG, jnp.float32), jnp.zeros((dv + 2 * SUBLANE, tq), jnp.float32)) * n_ops
    blocks = [(r0, min(MXU_DEPTH, tk - r0)) for r0 in range(0, tk, MXU_DEPTH)]

    def step(nxt, j_cur, slot_cur, cms, state):
        off_cur = pl.multiple_of(j_cur * tk, LANE)
        if nxt is not None:
            t_nxt, j_nxt, slot_nxt = nxt
            off_nxt = pl.multiple_of(j_nxt * tk, LANE)
            q_nxt = q_of(t_nxt)
        m_new = [jnp.maximum(state[2 * si], cms[si]) for si in range(n_ops)]
        cm_nxt, pv = [None] * n_ops, [None] * n_ops
        for r0, r in blocks:
            if nxt is not None:
                k = k_ref[0, pl.ds(off_nxt + r0, r), :]
            vt = vt_ref[0, :, pl.ds(off_cur + r0, r)]
            vt_ext = jnp.concatenate([vt, ones[:, :r]], axis=0)
            for si in range(n_ops):
                if nxt is not None:
                    s = jnp.dot(k, q_nxt[si], preferred_element_type=jnp.float32)
                    s_scr[slot_nxt, si, r0:r0 + r] = s
                    cmax = jnp.max(s, axis=0, keepdims=True)
                    cm_nxt[si] = cmax if cm_nxt[si] is None else jnp.maximum(cm_nxt[si], cmax)
                p = jnp.exp2(s_scr[slot_cur, si, r0:r0 + r] - m_new[si]).astype(vt.dtype)
                d = jnp.dot(vt_ext, p, preferred_element_type=jnp.float32)
                pv[si] = d if pv[si] is None else pv[si] + d
        out = []
        for si in range(n_ops):
            alpha = jnp.exp2(state[2 * si] - m_new[si])
            out += [m_new[si], alpha * state[2 * si + 1] + pv[si]]
        return tuple(cm_nxt), tuple(out)

    def result(st):
        return [st[2 * si + 1][:dv] / st[2 * si + 1][dv:dv + 1] for si in range(n_ops)]

    if n_chunks == 1:
        def tile(t, carry):
            finish(t, result(step(None, 0, 0, scores(t, 0, 0), init)[1]))
            return carry
        lax.fori_loop(0, n_qt, tile, 0)
        return

    assert n_chunks % 2 == 0

    def tile(t, cm):
        def pair(i, carry):
            cm_a, st = carry
            cm_b, st = step((t, 2 * i + 1, 1), 2 * i, 0, cm_a, st)
            cm_a, st = step((t, 2 * i + 2, 0), 2 * i + 1, 1, cm_b, st)
            return cm_a, st

        cm_a, st = lax.fori_loop(0, n_chunks // 2 - 1, pair, (cm, init), unroll=True)
        cm_b, st = step((t, n_chunks - 1, 1), n_chunks - 2, 0, cm_a, st)
        cm_next, st = step((jnp.minimum(t + 1, n_qt - 1), 0, 0), n_chunks - 1, 1, cm_b, st)
        finish(t, result(st))
        return cm_next

    lax.fori_loop(0, n_qt, tile, scores(0, 0, 0), unroll=4)


def _q_cols(t, tq):
    return pl.ds(pl.multiple_of(t * tq, LANE), tq)


def _mla_attn_kernel(n_chunks, tk, tq, qt_ref, k_ref, vt_ref, o_ref, s_scr):
    n_ops = s_scr.shape[1]

    def q_of(t):
        return [qt_ref[0, :, _q_cols(t * n_ops + i, tq)] for i in range(n_ops)]

    def finish(t, outs):
        for i in range(n_ops):
            o_ref[0, :, _q_cols(t * n_ops + i, tq)] = outs[i]

    _attn_sweep(n_chunks, tk, tq, qt_ref.shape[2] // (tq * n_ops), q_of, k_ref, vt_ref, s_scr, finish)


def _attn_call(kernel_fn, name, heads, dv, n_ops, qt, k, vt, nq, nk, tail, extra=()):
    b, _, nt = vt.shape
    assert not tail or ((nt - nq) % nq == 0 and (nt - nk) % nk == 0)
    qb = (nt - nq) // nq if tail else 0
    kb = (nt - nk) // nk if tail else 0
    tq = min(Q_TILE, nq)
    tk = _kv_chunk(nk)
    return pl.pallas_call(
        functools.partial(kernel_fn, nk // tk, tk, tq),
        grid=(b, heads),
        in_specs=[
            pl.BlockSpec((1, LANE, nq), lambda bi, h: (bi, h, qb)),
            pl.BlockSpec((1, nk, LANE), lambda bi, h: (bi, kb, h)),
            pl.BlockSpec((1, dv, nk), lambda bi, h: (bi, h, kb)),
        ] + [pl.BlockSpec(a.shape, lambda bi, h: (0, 0)) for a in extra],
        out_specs=pl.BlockSpec((1, dv, nq), lambda bi, h: (bi, h, 0)),
        out_shape=jax.ShapeDtypeStruct((b, heads * dv, nq), jnp.float32),
        scratch_shapes=[pltpu.VMEM((2, n_ops, tk, tq), jnp.float32)],
        compiler_params=_cparams("parallel", "parallel"),
        name=name,
    )(qt, k, vt, *extra)


def _mla_attn(qt, ka, vat, nq, nk, tail=False):
    n_ops = 2 if nq % (2 * Q_TILE) == 0 else 1
    return _attn_call(_mla_attn_kernel, "mla_attn", MLA_HEADS, MLA_V, n_ops, qt, ka, vat, nq, nk, tail)


def _diff_attn_kernel(n_chunks, tk, tq, qt_ref, k_ref, vt_ref, lam_ref, g_ref, o_ref, s_scr):
    row = lax.broadcasted_iota(jnp.int32, (LANE, 1), 0)

    def q_of(t):
        qt = qt_ref[0, :, _q_cols(t, tq)]
        zero = jnp.zeros_like(qt)
        return [jnp.where(row < DIFF_QK, qt, zero), jnp.where(row >= DIFF_QK, qt, zero)]

    def finish(t, outs):
        o = outs[0] - lam_ref[0:1, 0:1] * outs[1]
        o = o * lax.rsqrt(jnp.mean(o * o, axis=0, keepdims=True) + RMS_EPS) * g_ref[...]
        o_ref[0, :, _q_cols(t, tq)] = o * (1.0 - LAMBDA_INIT_0)

    _attn_sweep(n_chunks, tk, tq, qt_ref.shape[2] // tq, q_of, k_ref, vt_ref, s_scr, finish)


def _diff_attn(dqt, dk, dvt, lam, subln_col, nq, nk, tail=False):
    return _attn_call(_diff_attn_kernel, "diff_attn", DIFF_HEADS, DIFF_V, 2, dqt, dk, dvt, nq, nk, tail,
                      extra=(lam, subln_col))


def _residual_ln(xin, y, gate_vec, ln_g, ln_b):
    r = DEEPNORM_ALPHA * xin + gate_vec * y
    mu = jnp.mean(r, axis=-1, keepdims=True)
    rc = r - mu
    var = jnp.mean(rc * rc, axis=-1, keepdims=True)
    return rc * lax.rsqrt(var + LN_EPS) * ln_g + ln_b


def _even_merge_kernel(nx, x_ref, c_ref, mod_ref, oax_ref, oac_ref, obx_ref, obc_ref, gate_ref,
                       wout_ref, lng_ref, lnb_ref, out_ref):
    i = pl.program_id(1)
    is_ctx = i >= nx
    xin = jnp.where(is_ctx, c_ref[0], x_ref[0])
    oa = jnp.where(is_ctx, oac_ref[0], oax_ref[0]).T
    ob = jnp.where(is_ctx, obc_ref[0], obx_ref[0]).T
    o = jnp.concatenate([oa, ob], axis=1)
    g = gate_ref[0]
    y = _bf16_dot(o * (g * _sigmoid(g)), wout_ref[...])
    out_ref[0] = _residual_ln(xin, y, mod_ref[0, 0][2:3], lng_ref[...], lnb_ref[...])


def _even_merge(x, ctx, modsel, oax, oac, obx, obc, gate, wout, ln_g, ln_b):
    b, n, d = x.shape
    c = ctx.shape[1]
    t = TOKEN_TILE
    nx, nc = n // t, c // t
    wa, wb = oax.shape[1], obx.shape[1]
    full = lambda a: pl.BlockSpec(a.shape, lambda bi, i: (0,) * a.ndim)
    xi = lambda bi, i: (bi, jnp.minimum(i, nx - 1), 0)
    ci = lambda bi, i: (bi, jnp.maximum(i - nx, 0), 0)
    xit = lambda bi, i: (bi, 0, jnp.minimum(i, nx - 1))
    cit = lambda bi, i: (bi, 0, jnp.maximum(i - nx, 0))
    return pl.pallas_call(
        functools.partial(_even_merge_kernel, nx),
        grid=(b, nx + nc),
        in_specs=[
            pl.BlockSpec((1, t, d), xi),
            pl.BlockSpec((1, t, d), ci),
            pl.BlockSpec((1, 1, 3, d), lambda bi, i: (bi, jnp.where(i >= nx, 1, 0), 0, 0)),
            pl.BlockSpec((1, wa, t), xit), pl.BlockSpec((1, wa, t), cit),
            pl.BlockSpec((1, wb, t), xit), pl.BlockSpec((1, wb, t), cit),
            pl.BlockSpec((1, t, wa + wb), lambda bi, i: (bi, i, 0)),
            full(wout), full(ln_g), full(ln_b),
        ],
        out_specs=pl.BlockSpec((1, t, d), lambda bi, i: (bi, i, 0)),
        out_shape=jax.ShapeDtypeStruct((b, n + c, d), jnp.float32),
        compiler_params=_cparams("parallel", "parallel"),
        name="even_merge",
    )(x, ctx, modsel, oax, oac, obx, obc, gate, wout, ln_g, ln_b)


def _odd_proj_kernel(xc_ref, mod_ref, w_ref, u_ref, gate_ref):
    mod = mod_ref[0, 0]
    xm = xc_ref[0] * (1.0 + mod[1:2]) + mod[0:1]
    z = _bf16_dot(xm, w_ref[...])
    w = u_ref.shape[2]
    u_ref[0] = z[:, :w]
    gate_ref[0] = z[:, w:]


def _odd_proj(xc, modsel, w_in, nx):
    b, nt, d = xc.shape
    t = TOKEN_TILE
    w = w_in.shape[1] // 2
    return pl.pallas_call(
        _odd_proj_kernel,
        grid=(b, nt // t),
        in_specs=[
            pl.BlockSpec((1, t, d), lambda bi, i: (bi, i, 0)),
            pl.BlockSpec((1, 1, 3, d), lambda bi, i: (bi, jnp.where(i >= nx, 1, 0), 0, 0)),
            pl.BlockSpec(w_in.shape, lambda bi, i: (0, 0)),
        ],
        out_specs=[pl.BlockSpec((1, t, w), lambda bi, i: (bi, i, 0))] * 2,
        out_shape=[jax.ShapeDtypeStruct((b, nt, w), jnp.float32)] * 2,
        compiler_params=_cparams("parallel", "parallel"),
        name="odd_proj",
    )(xc, modsel, w_in)


def _conv_tile(u, prev, nxt, cw, cb):
    t = u.shape[0]
    row = lax.broadcasted_iota(jnp.int32, (SUBLANE, 1), 0)
    r1, r2, rp = pltpu.roll(u, 1, 0), pltpu.roll(u, 2, 0), pltpu.roll(u, t - 1, 0)
    m1 = jnp.concatenate([jnp.where(row == 0, prev[7:8], r1[:SUBLANE]), r1[SUBLANE:]], 0)
    m2 = jnp.concatenate([jnp.where(row == 0, prev[6:7], jnp.where(row == 1, prev[7:8], r2[:SUBLANE])),
                          r2[SUBLANE:]], 0)
    p1 = jnp.concatenate([rp[:t - SUBLANE], jnp.where(row == SUBLANE - 1, nxt[0:1], rp[t - SUBLANE:])], 0)
    return cw[0:1] * m2 + cw[1:2] * m1 + cw[2:3] * u + cw[3:4] * p1 + cb


def _lru_coeffs(uc, wa_ref, ba, wx_ref, bx, lam, a_ref, b_ref):
    neg_log_a1 = LRU_C * jnp.log1p(jnp.exp(-lam))
    for k in range(LRU_BLOCKS):
        sl = slice(k * LANE, (k + 1) * LANE)
        ub = uc[:, sl]
        r = _sigmoid(_bf16_dot(ub, wa_ref[k]) + ba[:, sl])
        ig = _sigmoid(_bf16_dot(ub, wx_ref[k]) + bx[:, sl])
        a = jnp.exp2(r * (-LOG2E * neg_log_a1[:, sl]))
        a_ref[:, sl] = a
        y = jnp.tanh(r * neg_log_a1[:, sl]) * (a * a + 1.0)
        root = jnp.where(y > 0.0, y * lax.rsqrt(y), 0.0)
        b_ref[:, sl] = root * (ig * ub)


def _scan_rows(a_ref, b_ref, h_out_ref, h0, t, reverse):
    def step(s, h):
        r = (t - 1 - s) if reverse else s
        h = a_ref[pl.ds(r, 1), :] * h + b_ref[pl.ds(r, 1), :]
        h_out_ref[pl.ds(r, 1), :] = h
        return h

    return lax.fori_loop(0, t, step, h0, unroll=8)


def _lru_ctx_kernel(u_ref, cw_ref, cb_ref, wa_ref, ba_ref, wx_ref, bx_ref, lam_ref, h_ref,
                    a_s, b_s, hs):
    u = u_ref[0]
    zero8 = jnp.zeros((SUBLANE, u.shape[1]), jnp.float32)
    uc = _conv_tile(u, zero8, zero8, cw_ref[...], cb_ref[...])
    t = u.shape[0]
    for d in range(2):
        _lru_coeffs(uc, wa_ref.at[d], ba_ref[d], wx_ref.at[d], bx_ref[d], lam_ref[d], a_s, b_s)
        h = _scan_rows(a_s, b_s, hs, jnp.zeros((1, u.shape[1]), jnp.float32), t, reverse=(d == 1))
        h_ref[0, d:d + 1, :] = h


def _lru_ctx(u_all, n, c, cw, cb, wa, ba, wx, bx, lam):
    b, _, w = u_all.shape
    full = lambda a: pl.BlockSpec(a.shape, lambda bi: (0,) * a.ndim)
    return pl.pallas_call(
        _lru_ctx_kernel,
        grid=(b,),
        in_specs=[pl.BlockSpec((1, c, w), lambda bi: (bi, n // c, 0)),
                  full(cw), full(cb), full(wa), full(ba), full(wx), full(bx), full(lam)],
        out_specs=pl.BlockSpec((1, 2, w), lambda bi: (bi, 0, 0)),
        out_shape=jax.ShapeDtypeStruct((b, 2, w), jnp.float32),
        scratch_shapes=[pltpu.VMEM((c, w), jnp.float32)] * 3,
        compiler_params=_cparams("parallel"),
        name="lru_ctx",
    )(u_all, cw, cb, wa, ba, wx, bx, lam)


def _lru_fwd_kernel(n_tiles, u_ref, up_ref, un_ref, h0_ref, cw_ref, cb_ref, wa_ref, ba_ref, wx_ref, bx_ref,
                    lam_ref, hf_ref, uc_ref, a_s, b_s, carry):
    tile = pl.program_id(1)

    @pl.when(tile == 0)
    def _():
        carry[...] = h0_ref[0, 0:1, :]

    u = u_ref[0]
    prev = jnp.where(tile == 0, 0.0, up_ref[0])
    nxt = jnp.where(tile == n_tiles - 1, 0.0, un_ref[0])
    uc = _conv_tile(u, prev, nxt, cw_ref[...], cb_ref[...])
    uc_ref[0] = uc
    _lru_coeffs(uc, wa_ref.at[0], ba_ref[0], wx_ref.at[0], bx_ref[0], lam_ref[0], a_s, b_s)
    carry[...] = _scan_rows(a_s, b_s, hf_ref.at[0], carry[...], u.shape[0], reverse=False)


def _lru_bwd_kernel(uc_ref, h0_ref, wa_ref, ba_ref, wx_ref, bx_ref, lam_ref, hf_ref, gate_ref, x_ref, mod_ref,
                    wout_ref, lng_ref, lnb_ref, out_ref, a_s, b_s, hs, carry):
    @pl.when(pl.program_id(1) == 0)
    def _():
        carry[...] = h0_ref[0, 1:2, :]

    uc = uc_ref[0]
    _lru_coeffs(uc, wa_ref.at[1], ba_ref[1], wx_ref.at[1], bx_ref[1], lam_ref[1], a_s, b_s)
    carry[...] = _scan_rows(a_s, b_s, hs, carry[...], uc.shape[0], reverse=True)
    g = gate_ref[0]
    hx = hf_ref[0] + hs[...]
    y = _bf16_dot(hx * (g * _sigmoid(g)), wout_ref[...])
    out_ref[0] = _residual_ln(x_ref[0], y, mod_ref[0, 0][2:3], lng_ref[...], lnb_ref[...])


def _lru_fwd(u_all, n, h0, cw, cb, wa, ba, wx, bx, lam):
    b, _, w = u_all.shape
    t = min(SCAN_TILE, n)
    n_tiles = n // t
    r = t // SUBLANE
    full = lambda a: pl.BlockSpec(a.shape, lambda bi, s: (0,) * a.ndim)
    tile_spec = pl.BlockSpec((1, t, w), lambda bi, s: (bi, s, 0))
    return pl.pallas_call(
        functools.partial(_lru_fwd_kernel, n_tiles),
        grid=(b, n_tiles),
        in_specs=[
            tile_spec,
            pl.BlockSpec((1, SUBLANE, w), lambda bi, s: (bi, jnp.maximum(s * r - 1, 0), 0)),
            pl.BlockSpec((1, SUBLANE, w), lambda bi, s: (bi, (s + 1) * r, 0)),
            pl.BlockSpec((1, 2, w), lambda bi, s: (bi, 0, 0)),
            full(cw), full(cb), full(wa), full(ba), full(wx), full(bx), full(lam),
        ],
        out_specs=[tile_spec, tile_spec],
        out_shape=[jax.ShapeDtypeStruct((b, n, w), jnp.float32)] * 2,
        scratch_shapes=[pltpu.VMEM((t, w), jnp.float32), pltpu.VMEM((t, w), jnp.float32),
                        pltpu.VMEM((1, w), jnp.float32)],
        compiler_params=_cparams("parallel", "arbitrary"),
        name="lru_fwd",
    )(u_all, u_all, u_all, h0, cw, cb, wa, ba, wx, bx, lam)


def _lru_bwd_out(uc, h0, wa, ba, wx, bx, lam, hf, gate, xc, modsel, wout, ln_g, ln_b):
    b, n, w = uc.shape
    d = xc.shape[2]
    t = min(SCAN_TILE, n)
    n_tiles = n // t
    full = lambda a: pl.BlockSpec(a.shape, lambda bi, s: (0,) * a.ndim)
    tile_spec = lambda width: pl.BlockSpec((1, t, width), lambda bi, s: (bi, n_tiles - 1 - s, 0))
    return pl.pallas_call(
        _lru_bwd_kernel,
        grid=(b, n_tiles),
        in_specs=[
            tile_spec(w),
            pl.BlockSpec((1, 2, w), lambda bi, s: (bi, 0, 0)),
            full(wa), full(ba), full(wx), full(bx), full(lam),
            tile_spec(w), tile_spec(w), tile_spec(d),
            pl.BlockSpec((1, 1, 3, d), lambda bi, s: (bi, 0, 0, 0)),
            full(wout), full(ln_g), full(ln_b),
        ],
        out_specs=tile_spec(d),
        out_shape=jax.ShapeDtypeStruct((b, n, d), jnp.float32),
        scratch_shapes=[pltpu.VMEM((t, w), jnp.float32)] * 3 + [pltpu.VMEM((1, w), jnp.float32)],
        compiler_params=_cparams("parallel", "arbitrary"),
        name="lru_bwd_out",
    )(uc, h0, wa, ba, wx, bx, lam, hf, gate, xc, modsel, wout, ln_g, ln_b)


def _rope_tables(n, c):
    t = np.arange(n)
    rows = (t // GRID_W).astype(np.float32)
    cols = (t % GRID_W).astype(np.float32)

    def ang(rot_dim):
        n_freq = rot_dim // 4
        freqs = np.float32(ROPE_THETA) ** (-np.arange(n_freq, dtype=np.float32) / np.float32(n_freq))
        a = np.concatenate([rows[:, None] * freqs, cols[:, None] * freqs], -1)
        return np.cos(a.astype(np.float64)), np.sin(a.astype(np.float64))

    (cos_a, sin_a), (cos_b, sin_b) = ang(MLA_ROPE), ang(DIFF_QK)
    one = np.ones((n, MLA_NOPE))
    zero = np.zeros((n, MLA_NOPE))
    pad1 = np.ones((n, LANE - MLA_QK))
    pad0 = np.zeros((n, LANE - MLA_QK))
    ca = np.concatenate([one, cos_a, cos_a, pad1], 1)
    sa = np.concatenate([zero, -sin_a, sin_a, pad0], 1)
    cb = np.concatenate([cos_b, cos_b] * 2, 1)
    sb = np.concatenate([-sin_b, sin_b] * 2, 1)
    ident = lambda tab, v: jnp.asarray(np.concatenate([tab, np.full((c, LANE), v)], 0), jnp.float32)
    return ident(ca, 1.0), ident(sa, 0.0), ident(cb, 1.0), ident(sb, 0.0)


def _even_weights(w_in, w_uq, w_ukv):
    d = w_in.shape[0]
    bf = jnp.bfloat16
    o = [0, 384, 640, 672, 1184, 1696, 2208, 3232]
    cq, ckv, kr, dq, dk, dv, gate = (w_in[:, o[i]:o[i + 1]] for i in range(7))
    kr_group = jnp.concatenate([jnp.zeros((d, MLA_NOPE), w_in.dtype), kr,
                                jnp.zeros((d, LANE - MLA_QK), w_in.dtype)], 1)
    w1 = jnp.concatenate([cq, ckv, kr_group, dq, dk, dv, gate], 1).astype(bf)
    wuq = jnp.pad(w_uq.reshape(MLA_Q_LORA, MLA_HEADS, MLA_QK),
                  ((0, 0), (0, 0), (0, LANE - MLA_QK))).reshape(MLA_Q_LORA, MLA_HEADS * LANE).astype(bf)
    ukv = w_ukv.reshape(MLA_KV_LORA, MLA_HEADS, MLA_NOPE + MLA_V)
    wuk = jnp.pad(ukv[..., :MLA_NOPE], ((0, 0), (0, 0), (0, LANE - MLA_NOPE))).reshape(MLA_KV_LORA, MLA_HEADS * LANE)
    wuv = ukv[..., MLA_NOPE:].reshape(MLA_KV_LORA, MLA_HEADS * MLA_V)
    wkv = jnp.concatenate([wuk, wuv], 1).astype(bf)
    return w1, wuq, wkv


def _mod_select(mods_l, b):
    d = mods_l.shape[1] // 3
    mx = mods_l[:b].reshape(b, 3, d)
    mc = jnp.broadcast_to(mods_l[b].reshape(1, 3, d), (b, 3, d))
    return jnp.stack([mx, mc], axis=1)


def kernel(x, c, ctx, c_ctx, ada_w, ada_b, post_ln_g, post_ln_b, e_w_in, e_q_norm_g, e_w_uq, e_kv_norm_g, e_w_ukv, e_lam_q1, e_lam_k1, e_lam_q2, e_lam_k2, e_subln_g, e_w_out, o_w_in, o_conv_w, o_conv_b, o_gate_a_w, o_gate_a_b, o_gate_x_w, o_gate_x_b, o_lru_lambda, o_w_out):
    b, n, d = x.shape
    cl = ctx.shape[1]
    assert ada_w.shape[0] == DEPTH and b < SUBLANE
    assert n % Q_TILE == 0 and cl % TOKEN_TILE == 0 and n % cl == 0
    bf = jnp.bfloat16

    cond = jnp.concatenate([c, c_ctx[None], jnp.zeros((SUBLANE - b - 1, d), c.dtype)], 0)
    lamv = jnp.concatenate([e_lam_q1, e_lam_k1, e_lam_q2, e_lam_k2], 0)
    mods, lam = _adaln(cond, ada_w, ada_b, lamv)
    mod0, mod1 = _mod_select(mods[0], b), _mod_select(mods[1], b)

    w1, wuq, wkv = _even_weights(e_w_in[0], e_w_uq[0], e_w_ukv[0])
    tabs = _rope_tables(n, cl)
    qt, ka, vat, dqt, dk, dvt, gate = _even_proj(
        x, ctx, mod0, w1, e_q_norm_g[0][None], wuq, e_kv_norm_g[0][None], wkv, tabs)
    subln_col = e_subln_g[0][:, None]
    oax = _mla_attn(qt, ka, vat, n, n + cl)
    obx = _diff_attn(dqt, dk, dvt, lam, subln_col, n, n + cl)
    oac = _mla_attn(qt, ka, vat, cl, cl, tail=True)
    obc = _diff_attn(dqt, dk, dvt, lam, subln_col, cl, cl, tail=True)
    xc = _even_merge(x, ctx, mod0, oax, oac, obx, obc, gate, e_w_out[0].astype(bf),
                     post_ln_g[0][None], post_ln_b[0][None])

    u_all, gate1 = _odd_proj(xc, mod1, o_w_in[0].astype(bf), n // TOKEN_TILE)
    cw, cb = o_conv_w[0], o_conv_b[0][None]
    wa, wx = o_gate_a_w[0].astype(bf), o_gate_x_w[0].astype(bf)
    w = u_all.shape[2]
    ba = o_gate_a_b[0].reshape(2, 1, w)
    bx = o_gate_x_b[0].reshape(2, 1, w)
    lru_lam = o_lru_lambda[0].reshape(2, 1, w)
    h0 = _lru_ctx(u_all, n, cl, cw, cb, wa, ba, wx, bx, lru_lam)
    hf, uc = _lru_fwd(u_all, n, h0, cw, cb, wa, ba, wx, bx, lru_lam)
    return _lru_bwd_out(uc, h0, wa, ba, wx, bx, lru_lam, hf, gate1, xc, mod1, o_w_out[0].astype(bf),
                        post_ln_g[1][None], post_ln_b[1][None])
```

```python
import functools
import math

import jax
import jax.numpy as jnp
import numpy as np
from jax import lax
from jax.experimental import pallas as pl
from jax.experimental.pallas import tpu as pltpu

GRID_W = 64
ROPE_THETA = 10000.0
LN_EPS = 1e-6
RMS_EPS = 1e-6

MLA_HEADS = 8
MLA_Q_LORA = 384
MLA_KV_LORA = 256
MLA_NOPE = 64
MLA_ROPE = 32
MLA_V = 64
MLA_QK = MLA_NOPE + MLA_ROPE
MLA_SCALE = MLA_QK ** -0.5

DIFF_HEADS = 4
DIFF_QK = 64
DIFF_V = 2 * DIFF_QK
DIFF_SCALE = DIFF_QK ** -0.5

LRU_BLOCKS = 8
LRU_C = 8.0
CONV_W = 4

DEPTH = 2
DEEPNORM_ALPHA = (2 * DEPTH) ** 0.25
LAMBDA_INIT_0 = 0.8 - 0.6 * math.exp(-0.3 * 0)

LANE = 128
SUBLANE = 8
MXU_DEPTH = 256
LOG2E = 1.4426950408889634
NEG_BIG = -1e30

TOKEN_TILE = 256
Q_TILE = 512
KV_CHUNK = 1408
SCAN_TILE = 512
VMEM_LIMIT = 56 * 1024 * 1024

_G_CQ = (0, 384)
_G_CKV = (384, 640)
_G_KR = (640, 768)
_G_DQ = (768, 1280)
_G_DK = (1280, 1792)
_G_DV = (1792, 2304)
_G_GATE = (2304, 3328)
_EVEN_W = 3328


def _cparams(*sem):
    return pltpu.CompilerParams(dimension_semantics=sem, vmem_limit_bytes=VMEM_LIMIT)


def _bf16_dot(a, b):
    return jnp.dot(a.astype(jnp.bfloat16), b.astype(jnp.bfloat16), preferred_element_type=jnp.float32)


def _sigmoid(x):
    return 1.0 / (1.0 + jnp.exp(-x))


def _adaln_kernel(cond_ref, w_ref, b_ref, lamv_ref, mod_ref, lam_ref):
    cond = cond_ref[...]
    h = cond * _sigmoid(cond)
    mod_ref[0] = jnp.dot(h, w_ref[0], preferred_element_type=jnp.float32,
                         precision=lax.Precision.HIGHEST) + b_ref[0]
    lv = lamv_ref[...]
    d1 = jnp.sum(lv[0:1] * lv[1:2], axis=-1, keepdims=True)
    d2 = jnp.sum(lv[2:3] * lv[3:4], axis=-1, keepdims=True)
    lam = jnp.exp(d1) - jnp.exp(d2) + LAMBDA_INIT_0
    lam_ref[...] = jnp.broadcast_to(lam, lam_ref.shape)


def _adaln(cond, ada_w, ada_b, lamv):
    depth, d, d3 = ada_w.shape
    nj = d3 // d
    return pl.pallas_call(
        _adaln_kernel,
        grid=(depth, nj),
        in_specs=[
            pl.BlockSpec((SUBLANE, d), lambda l, j: (0, 0)),
            pl.BlockSpec((1, d, d), lambda l, j: (l, 0, j)),
            pl.BlockSpec((1, 1, d), lambda l, j: (l, 0, j)),
            pl.BlockSpec(lamv.shape, lambda l, j: (0, 0)),
        ],
        out_specs=[
            pl.BlockSpec((1, SUBLANE, d), lambda l, j: (l, 0, j)),
            pl.BlockSpec((SUBLANE, LANE), lambda l, j: (0, 0)),
        ],
        out_shape=[
            jax.ShapeDtypeStruct((depth, SUBLANE, d3), jnp.float32),
            jax.ShapeDtypeStruct((SUBLANE, LANE), jnp.float32),
        ],
        compiler_params=_cparams("arbitrary", "arbitrary"),
        name="adaln",
    )(cond, ada_w, ada_b.reshape(depth, 1, d3), lamv)


def _rms(x, g):
    return x * lax.rsqrt(jnp.mean(x * x, axis=-1, keepdims=True) + RMS_EPS) * g


def _rope_group(x, cos, sin, half, first_half_mask):
    partner = jnp.where(first_half_mask, pltpu.roll(x, LANE - half, 1), pltpu.roll(x, half, 1))
    return x * cos + partner * sin


def _even_proj_kernel(nx, x_ref, c_ref, mod_ref, w1_ref, qg_ref, wuq_ref, kvg_ref, wkv_ref,
                      ca_ref, sa_ref, cb_ref, sb_ref,
                      qt_ref, ka_ref, vat_ref, dqt_ref, dk_ref, dvt_ref, gate_ref):
    i = pl.program_id(1)
    xin = jnp.where(i >= nx, c_ref[0], x_ref[0])
    mod = mod_ref[0, 0]
    xm = (xin * (1.0 + mod[1:2]) + mod[0:1]).astype(w1_ref.dtype)

    def zcols(lo, hi):
        return jnp.dot(xm, w1_ref[:, lo:hi], preferred_element_type=jnp.float32)

    lane = lax.broadcasted_iota(jnp.int32, (1, LANE), 1)
    mla_first = jnp.logical_and(lane >= MLA_NOPE, lane < MLA_NOPE + MLA_ROPE // 2)
    diff_first = (lane % DIFF_QK) < DIFF_QK // 2
    ca, sa, cb, sb = ca_ref[...], sa_ref[...], cb_ref[...], sb_ref[...]

    z_lat = zcols(_G_CQ[0], _G_KR[1])
    z_dq = zcols(*_G_DQ)
    z_dk = zcols(*_G_DK)

    cqn = _rms(z_lat[:, _G_CQ[0]:_G_CQ[1]], qg_ref[...])
    q = _bf16_dot(cqn, wuq_ref[...])
    ckvn = _rms(z_lat[:, _G_CKV[0]:_G_CKV[1]], kvg_ref[...])
    kv = _bf16_dot(ckvn, wkv_ref[...])
    z_dv = zcols(*_G_DV)
    gate_ref[0] = zcols(*_G_GATE).astype(gate_ref.dtype)

    q_heads = []
    for h in range(MLA_HEADS):
        qh = _rope_group(q[:, h * LANE:(h + 1) * LANE], ca, sa, MLA_ROPE // 2, mla_first)
        q_heads.append(qh * (MLA_SCALE * LOG2E))
    qt_ref[0] = jnp.concatenate(q_heads, axis=1).T.astype(qt_ref.dtype)

    kr = _rope_group(z_lat[:, _G_KR[0]:_G_KR[1]], ca, sa, MLA_ROPE // 2, mla_first)
    for h in range(MLA_HEADS):
        ka_ref[0, :, h * LANE:(h + 1) * LANE] = (kv[:, h * LANE:(h + 1) * LANE] + kr).astype(ka_ref.dtype)
    vat_ref[0] = kv[:, MLA_HEADS * LANE:].T.astype(vat_ref.dtype)

    dq_heads = []
    for h in range(DIFF_HEADS):
        sl = slice(h * LANE, (h + 1) * LANE)
        dq_heads.append(_rope_group(z_dq[:, sl], cb, sb, DIFF_QK // 2, diff_first) * (DIFF_SCALE * LOG2E))
        dk_ref[0, :, sl] = _rope_group(z_dk[:, sl], cb, sb, DIFF_QK // 2, diff_first).astype(dk_ref.dtype)
    dqt_ref[0] = jnp.concatenate(dq_heads, axis=1).T.astype(dqt_ref.dtype)
    dvt_ref[0] = z_dv.T.astype(dvt_ref.dtype)


def _even_proj(x, ctx, modsel, w1, qg, wuq, kvg, wkv, tabs):
    b, n, d = x.shape
    c = ctx.shape[1]
    t = TOKEN_TILE
    nx, nc = n // t, c // t
    nt = n + c
    full = lambda a: pl.BlockSpec(a.shape, lambda bi, i: (0,) * a.ndim)
    tab_spec = pl.BlockSpec((t, LANE), lambda bi, i: (i, 0))
    tok = lambda w: pl.BlockSpec((1, t, w), lambda bi, i: (bi, i, 0))
    tok_t = lambda w: pl.BlockSpec((1, w, t), lambda bi, i: (bi, 0, i))
    bf = jnp.bfloat16
    return pl.pallas_call(
        functools.partial(_even_proj_kernel, nx),
        grid=(b, nx + nc),
        in_specs=[
            pl.BlockSpec((1, t, d), lambda bi, i: (bi, jnp.minimum(i, nx - 1), 0)),
            pl.BlockSpec((1, t, d), lambda bi, i: (bi, jnp.maximum(i - nx, 0), 0)),
            pl.BlockSpec((1, 1, 3, d), lambda bi, i: (bi, jnp.where(i >= nx, 1, 0), 0, 0)),
            full(w1), full(qg), full(wuq), full(kvg), full(wkv),
            tab_spec, tab_spec, tab_spec, tab_spec,
        ],
        out_specs=[
            tok_t(MLA_HEADS * LANE), tok(MLA_HEADS * LANE), tok_t(MLA_HEADS * MLA_V),
            tok_t(DIFF_HEADS * LANE), tok(DIFF_HEADS * LANE), tok_t(DIFF_HEADS * DIFF_V),
            tok(MLA_HEADS * MLA_V + DIFF_HEADS * DIFF_V),
        ],
        out_shape=[
            jax.ShapeDtypeStruct((b, MLA_HEADS * LANE, nt), bf),
            jax.ShapeDtypeStruct((b, nt, MLA_HEADS * LANE), bf),
            jax.ShapeDtypeStruct((b, MLA_HEADS * MLA_V, nt), bf),
            jax.ShapeDtypeStruct((b, DIFF_HEADS * LANE, nt), bf),
            jax.ShapeDtypeStruct((b, nt, DIFF_HEADS * LANE), bf),
            jax.ShapeDtypeStruct((b, DIFF_HEADS * DIFF_V, nt), bf),
            jax.ShapeDtypeStruct((b, nt, MLA_HEADS * MLA_V + DIFF_HEADS * DIFF_V), bf),
        ],
        compiler_params=_cparams("parallel", "parallel"),
        name="even_proj",
    )(x, ctx, modsel, w1, qg, wuq, kvg, wkv, *tabs)


def _kv_chunk(nk):
    if nk <= KV_CHUNK:
        return nk
    for m in range(KV_CHUNK // LANE, 0, -1):
        if nk % (m * LANE) == 0 and (nk // (m * LANE)) % 2 == 0:
            return m * LANE
    raise ValueError(f"no even chunking of {nk} keys")


def _attn_sweep(n_chunks, tk, tq, n_qt, q_of, k_ref, vt_ref, s_scr, finish):
    dv = vt_ref.shape[1]

    def scores(t, j, slot):
        k = k_ref[0, pl.ds(pl.multiple_of(j * tk, LANE), tk), :]
        cms = []
        for si, qt in enumerate(q_of(t)):
            s = jnp.dot(k, qt, preferred_element_type=jnp.float32)
            s_scr[slot, si] = s
            cms.append(jnp.max(s, axis=0, keepdims=True))
        return tuple(cms)

    n_ops = s_scr.shape[1]
    ones = jnp.ones((2 * SUBLANE, MXU_DEPTH), vt_ref.dtype)
    init = (jnp.full((1, tq), NEG_BIG, jnp.float32), jnp.zeros((dv + 2 * SUBLANE, tq), jnp.float32)) * n_ops
    blocks = [(r0, min(MXU_DEPTH, tk - r0)) for r0 in range(0, tk, MXU_DEPTH)]

    def step(nxt, j_cur, slot_cur, cms, state):
        off_cur = pl.multiple_of(j_cur * tk, LANE)
        if nxt is not None:
            t_nxt, j_nxt, slot_nxt = nxt
            off_nxt = pl.multiple_of(j_nxt * tk, LANE)
            q_nxt = q_of(t_nxt)
        m_new = [jnp.maximum(state[2 * si], cms[si]) for si in range(n_ops)]
        cm_nxt, pv = [None] * n_ops, [None] * n_ops
        for r0, r in blocks:
            if nxt is not None:
                k = k_ref[0, pl.ds(off_nxt + r0, r), :]
            vt = vt_ref[0, :, pl.ds(off_cur + r0, r)]
            vt_ext = jnp.concatenate([vt, ones[:, :r]], axis=0)
            for si in range(n_ops):
                if nxt is not None:
                    s = jnp.dot(k, q_nxt[si], preferred_element_type=jnp.float32)
                    s_scr[slot_nxt, si, r0:r0 + r] = s
                    cmax = jnp.max(s, axis=0, keepdims=True)
                    cm_nxt[si] = cmax if cm_nxt[si] is None else jnp.maximum(cm_nxt[si], cmax)
                p = jnp.exp2(s_scr[slot_cur, si, r0:r0 + r] - m_new[si]).astype(vt.dtype)
                d = jnp.dot(vt_ext, p, preferred_element_type=jnp.float32)
                pv[si] = d if pv[si] is None else pv[si] + d
        out = []
        for si in range(n_ops):
            alpha = jnp.exp2(state[2 * si] - m_new[si])
            out += [m_new[si], alpha * state[2 * si + 1] + pv[si]]
        return tuple(cm_nxt), tuple(out)

    def result(st):
        return [st[2 * si + 1][:dv] / st[2 * si + 1][dv:dv + 1] for si in range(n_ops)]

    if n_chunks == 1:
        def tile(t, carry):
            finish(t, result(step(None, 0, 0, scores(t, 0, 0), init)[1]))
            return carry
        lax.fori_loop(0, n_qt, tile, 0)
        return

    assert n_chunks % 2 == 0

    def tile(t, cm):
        def pair(i, carry):
            cm_a, st = carry
            cm_b, st = step((t, 2 * i + 1, 1), 2 * i, 0, cm_a, st)
            cm_a, st = step((t, 2 * i + 2, 0), 2 * i + 1, 1, cm_b, st)
            return cm_a, st

        cm_a, st = lax.fori_loop(0, n_chunks // 2 - 1, pair, (cm, init), unroll=True)
        cm_b, st = step((t, n_chunks - 1, 1), n_chunks - 2, 0, cm_a, st)
        cm_next, st = step((jnp.minimum(t + 1, n_qt - 1), 0, 0), n_chunks - 1, 1, cm_b, st)
        finish(t, result(st))
        return cm_next

    lax.fori_loop(0, n_qt, tile, scores(0, 0, 0), unroll=2)


def _q_cols(t, tq):
    return pl.ds(pl.multiple_of(t * tq, LANE), tq)


def _mla_attn_kernel(n_chunks, tk, tq, qt_ref, k_ref, vt_ref, o_ref, s_scr):
    n_ops = s_scr.shape[1]

    def q_of(t):
        return [qt_ref[0, :, _q_cols(t * n_ops + i, tq)] for i in range(n_ops)]

    def finish(t, outs):
        for i in range(n_ops):
            o_ref[0, :, _q_cols(t * n_ops + i, tq)] = outs[i].astype(o_ref.dtype)

    _attn_sweep(n_chunks, tk, tq, qt_ref.shape[2] // (tq * n_ops), q_of, k_ref, vt_ref, s_scr, finish)


def _attn_call(kernel_fn, name, heads, dv, n_ops, qt, k, vt, nq, nk, tail, extra=()):
    b, _, nt = vt.shape
    assert not tail or ((nt - nq) % nq == 0 and (nt - nk) % nk == 0)
    qb = (nt - nq) // nq if tail else 0
    kb = (nt - nk) // nk if tail else 0
    tq = min(Q_TILE, nq)
    tk = _kv_chunk(nk)
    return pl.pallas_call(
        functools.partial(kernel_fn, nk // tk, tk, tq),
        grid=(b, heads),
        in_specs=[
            pl.BlockSpec((1, LANE, nq), lambda bi, h: (bi, h, qb)),
            pl.BlockSpec((1, nk, LANE), lambda bi, h: (bi, kb, h)),
            pl.BlockSpec((1, dv, nk), lambda bi, h: (bi, h, kb)),
        ] + [pl.BlockSpec(a.shape, lambda bi, h: (0, 0)) for a in extra],
        out_specs=pl.BlockSpec((1, dv, nq), lambda bi, h: (bi, h, 0)),
        out_shape=jax.ShapeDtypeStruct((b, heads * dv, nq), jnp.bfloat16),
        scratch_shapes=[pltpu.VMEM((2, n_ops, tk, tq), jnp.float32)],
        compiler_params=_cparams("parallel", "parallel"),
        name=name,
    )(qt, k, vt, *extra)


def _mla_attn(qt, ka, vat, nq, nk, tail=False):
    n_ops = 2 if nq % (2 * Q_TILE) == 0 else 1
    return _attn_call(_mla_attn_kernel, "mla_attn", MLA_HEADS, MLA_V, n_ops, qt, ka, vat, nq, nk, tail)


def _diff_attn_kernel(n_chunks, tk, tq, qt_ref, k_ref, vt_ref, lam_ref, g_ref, o_ref, s_scr):
    row = lax.broadcasted_iota(jnp.int32, (LANE, 1), 0)

    def q_of(t):
        qt = qt_ref[0, :, _q_cols(t, tq)]
        zero = jnp.zeros_like(qt)
        return [jnp.where(row < DIFF_QK, qt, zero), jnp.where(row >= DIFF_QK, qt, zero)]

    def finish(t, outs):
        o = outs[0] - lam_ref[0:1, 0:1] * outs[1]
        o = o * lax.rsqrt(jnp.mean(o * o, axis=0, keepdims=True) + RMS_EPS) * g_ref[...]
        o_ref[0, :, _q_cols(t, tq)] = (o * (1.0 - LAMBDA_INIT_0)).astype(o_ref.dtype)

    _attn_sweep(n_chunks, tk, tq, qt_ref.shape[2] // tq, q_of, k_ref, vt_ref, s_scr, finish)


def _diff_attn(dqt, dk, dvt, lam, subln_col, nq, nk, tail=False):
    return _attn_call(_diff_attn_kernel, "diff_attn", DIFF_HEADS, DIFF_V, 2, dqt, dk, dvt, nq, nk, tail,
                      extra=(lam, subln_col))


def _residual_ln(xin, y, gate_vec, ln_g, ln_b):
    r = DEEPNORM_ALPHA * xin + gate_vec * y
    mu = jnp.mean(r, axis=-1, keepdims=True)
    rc = r - mu
    var = jnp.mean(rc * rc, axis=-1, keepdims=True)
    return rc * lax.rsqrt(var + LN_EPS) * ln_g + ln_b


def _even_merge_kernel(nx, x_ref, c_ref, mod_ref, oax_ref, oac_ref, obx_ref, obc_ref, gate_ref,
                       wout_ref, lng_ref, lnb_ref, out_ref):
    i = pl.program_id(1)
    is_ctx = i >= nx
    xin = jnp.where(is_ctx, c_ref[0], x_ref[0])
    oa = jnp.where(is_ctx, oac_ref[0], oax_ref[0]).astype(jnp.float32).T
    ob = jnp.where(is_ctx, obc_ref[0], obx_ref[0]).astype(jnp.float32).T
    o = jnp.concatenate([oa, ob], axis=1)
    g = gate_ref[0].astype(jnp.float32)
    y = _bf16_dot(o * (g * _sigmoid(g)), wout_ref[...])
    out_ref[0] = _residual_ln(xin, y, mod_ref[0, 0][2:3], lng_ref[...], lnb_ref[...])


def _even_merge(x, ctx, modsel, oax, oac, obx, obc, gate, wout, ln_g, ln_b):
    b, n, d = x.shape
    c = ctx.shape[1]
    t = TOKEN_TILE
    nx, nc = n // t, c // t
    wa, wb = oax.shape[1], obx.shape[1]
    full = lambda a: pl.BlockSpec(a.shape, lambda bi, i: (0,) * a.ndim)
    xi = lambda bi, i: (bi, jnp.minimum(i, nx - 1), 0)
    ci = lambda bi, i: (bi, jnp.maximum(i - nx, 0), 0)
    xit = lambda bi, i: (bi, 0, jnp.minimum(i, nx - 1))
    cit = lambda bi, i: (bi, 0, jnp.maximum(i - nx, 0))
    return pl.pallas_call(
        functools.partial(_even_merge_kernel, nx),
        grid=(b, nx + nc),
        in_specs=[
            pl.BlockSpec((1, t, d), xi),
            pl.BlockSpec((1, t, d), ci),
            pl.BlockSpec((1, 1, 3, d), lambda bi, i: (bi, jnp.where(i >= nx, 1, 0), 0, 0)),
            pl.BlockSpec((1, wa, t), xit), pl.BlockSpec((1, wa, t), cit),
            pl.BlockSpec((1, wb, t), xit), pl.BlockSpec((1, wb, t), cit),
            pl.BlockSpec((1, t, wa + wb), lambda bi, i: (bi, i, 0)),
            full(wout), full(ln_g), full(ln_b),
        ],
        out_specs=pl.BlockSpec((1, t, d), lambda bi, i: (bi, i, 0)),
        out_shape=jax.ShapeDtypeStruct((b, n + c, d), jnp.float32),
        compiler_params=_cparams("parallel", "parallel"),
        name="even_merge",
    )(x, ctx, modsel, oax, oac, obx, obc, gate, wout, ln_g, ln_b)


def _odd_proj_kernel(xc_ref, mod_ref, w_ref, u_ref, gate_ref):
    mod = mod_ref[0, 0]
    xm = xc_ref[0] * (1.0 + mod[1:2]) + mod[0:1]
    z = _bf16_dot(xm, w_ref[...])
    w = u_ref.shape[2]
    u_ref[0] = z[:, :w]
    gate_ref[0] = z[:, w:].astype(gate_ref.dtype)


def _odd_proj(xc, modsel, w_in, nx):
    b, nt, d = xc.shape
    t = TOKEN_TILE
    w = w_in.shape[1] // 2
    return pl.pallas_call(
        _odd_proj_kernel,
        grid=(b, nt // t),
        in_specs=[
            pl.BlockSpec((1, t, d), lambda bi, i: (bi, i, 0)),
            pl.BlockSpec((1, 1, 3, d), lambda bi, i: (bi, jnp.where(i >= nx, 1, 0), 0, 0)),
            pl.BlockSpec(w_in.shape, lambda bi, i: (0, 0)),
        ],
        out_specs=[pl.BlockSpec((1, t, w), lambda bi, i: (bi, i, 0))] * 2,
        out_shape=[jax.ShapeDtypeStruct((b, nt, w), jnp.float32), jax.ShapeDtypeStruct((b, nt, w), jnp.bfloat16)],
        compiler_params=_cparams("parallel", "parallel"),
        name="odd_proj",
    )(xc, modsel, w_in)


def _conv_tile(u, prev, nxt, cw, cb):
    t = u.shape[0]
    row = lax.broadcasted_iota(jnp.int32, (SUBLANE, 1), 0)
    r1, r2, rp = pltpu.roll(u, 1, 0), pltpu.roll(u, 2, 0), pltpu.roll(u, t - 1, 0)
    m1 = jnp.concatenate([jnp.where(row == 0, prev[7:8], r1[:SUBLANE]), r1[SUBLANE:]], 0)
    m2 = jnp.concatenate([jnp.where(row == 0, prev[6:7], jnp.where(row == 1, prev[7:8], r2[:SUBLANE])),
                          r2[SUBLANE:]], 0)
    p1 = jnp.concatenate([rp[:t - SUBLANE], jnp.where(row == SUBLANE - 1, nxt[0:1], rp[t - SUBLANE:])], 0)
    return cw[0:1] * m2 + cw[1:2] * m1 + cw[2:3] * u + cw[3:4] * p1 + cb


def _lru_coeffs(uc, wa_ref, ba, wx_ref, bx, lam, a_ref, b_ref):
    neg_log_a1 = LRU_C * jnp.log1p(jnp.exp(-lam))
    for k in range(LRU_BLOCKS):
        sl = slice(k * LANE, (k + 1) * LANE)
        ub = uc[:, sl]
        r = _sigmoid(_bf16_dot(ub, wa_ref[k]) + ba[:, sl])
        ig = _sigmoid(_bf16_dot(ub, wx_ref[k]) + bx[:, sl])
        a = jnp.exp2(r * (-LOG2E * neg_log_a1[:, sl]))
        a_ref[:, sl] = a
        y = jnp.tanh(r * neg_log_a1[:, sl]) * (a * a + 1.0)
        root = jnp.where(y > 0.0, y * lax.rsqrt(y), 0.0)
        b_ref[:, sl] = root * (ig * ub)


def _scan_rows(a_ref, b_ref, h_out_ref, h0, t, reverse):
    def step(s, h):
        r = (t - 1 - s) if reverse else s
        h = a_ref[pl.ds(r, 1), :] * h + b_ref[pl.ds(r, 1), :]
        h_out_ref[pl.ds(r, 1), :] = h
        return h

    return lax.fori_loop(0, t, step, h0, unroll=8)


def _lru_ctx_kernel(u_ref, cw_ref, cb_ref, wa_ref, ba_ref, wx_ref, bx_ref, lam_ref, h_ref,
                    a_s, b_s, hs):
    u = u_ref[0]
    zero8 = jnp.zeros((SUBLANE, u.shape[1]), jnp.float32)
    uc = _conv_tile(u, zero8, zero8, cw_ref[...], cb_ref[...])
    t = u.shape[0]
    for d in range(2):
        _lru_coeffs(uc, wa_ref.at[d], ba_ref[d], wx_ref.at[d], bx_ref[d], lam_ref[d], a_s, b_s)
        h = _scan_rows(a_s, b_s, hs, jnp.zeros((1, u.shape[1]), jnp.float32), t, reverse=(d == 1))
        h_ref[0, d:d + 1, :] = h


def _lru_ctx(u_all, n, c, cw, cb, wa, ba, wx, bx, lam):
    b, _, w = u_all.shape
    full = lambda a: pl.BlockSpec(a.shape, lambda bi: (0,) * a.ndim)
    return pl.pallas_call(
        _lru_ctx_kernel,
        grid=(b,),
        in_specs=[pl.BlockSpec((1, c, w), lambda bi: (bi, n // c, 0)),
                  full(cw), full(cb), full(wa), full(ba), full(wx), full(bx), full(lam)],
        out_specs=pl.BlockSpec((1, 2, w), lambda bi: (bi, 0, 0)),
        out_shape=jax.ShapeDtypeStruct((b, 2, w), jnp.float32),
        scratch_shapes=[pltpu.VMEM((c, w), jnp.float32)] * 3,
        compiler_params=_cparams("parallel"),
        name="lru_ctx",
    )(u_all, cw, cb, wa, ba, wx, bx, lam)


def _lru_fwd_kernel(n_tiles, u_ref, up_ref, un_ref, h0_ref, cw_ref, cb_ref, wa_ref, ba_ref, wx_ref, bx_ref,
                    lam_ref, hf_ref, uc_ref, a_s, b_s, carry):
    tile = pl.program_id(1)

    @pl.when(tile == 0)
    def _():
        carry[...] = h0_ref[0, 0:1, :]

    u = u_ref[0]
    prev = jnp.where(tile == 0, 0.0, up_ref[0])
    nxt = jnp.where(tile == n_tiles - 1, 0.0, un_ref[0])
    uc = _conv_tile(u, prev, nxt, cw_ref[...], cb_ref[...])
    uc_ref[0] = uc
    _lru_coeffs(uc, wa_ref.at[0], ba_ref[0], wx_ref.at[0], bx_ref[0], lam_ref[0], a_s, b_s)
    carry[...] = _scan_rows(a_s, b_s, hf_ref.at[0], carry[...], u.shape[0], reverse=False)


def _lru_bwd_kernel(uc_ref, h0_ref, wa_ref, ba_ref, wx_ref, bx_ref, lam_ref, hf_ref, gate_ref, x_ref, mod_ref,
                    wout_ref, lng_ref, lnb_ref, out_ref, a_s, b_s, hs, carry):
    @pl.when(pl.program_id(1) == 0)
    def _():
        carry[...] = h0_ref[0, 1:2, :]

    uc = uc_ref[0]
    _lru_coeffs(uc, wa_ref.at[1], ba_ref[1], wx_ref.at[1], bx_ref[1], lam_ref[1], a_s, b_s)
    carry[...] = _scan_rows(a_s, b_s, hs, carry[...], uc.shape[0], reverse=True)
    g = gate_ref[0].astype(jnp.float32)
    hx = hf_ref[0] + hs[...]
    y = _bf16_dot(hx * (g * _sigmoid(g)), wout_ref[...])
    out_ref[0] = _residual_ln(x_ref[0], y, mod_ref[0, 0][2:3], lng_ref[...], lnb_ref[...])


def _lru_fwd(u_all, n, h0, cw, cb, wa, ba, wx, bx, lam):
    b, _, w = u_all.shape
    t = min(SCAN_TILE, n)
    n_tiles = n // t
    r = t // SUBLANE
    full = lambda a: pl.BlockSpec(a.shape, lambda bi, s: (0,) * a.ndim)
    tile_spec = pl.BlockSpec((1, t, w), lambda bi, s: (bi, s, 0))
    return pl.pallas_call(
        functools.partial(_lru_fwd_kernel, n_tiles),
        grid=(b, n_tiles),
        in_specs=[
            tile_spec,
            pl.BlockSpec((1, SUBLANE, w), lambda bi, s: (bi, jnp.maximum(s * r - 1, 0), 0)),
            pl.BlockSpec((1, SUBLANE, w), lambda bi, s: (bi, (s + 1) * r, 0)),
            pl.BlockSpec((1, 2, w), lambda bi, s: (bi, 0, 0)),
            full(cw), full(cb), full(wa), full(ba), full(wx), full(bx), full(lam),
        ],
        out_specs=[tile_spec, tile_spec],
        out_shape=[jax.ShapeDtypeStruct((b, n, w), jnp.float32)] * 2,
        scratch_shapes=[pltpu.VMEM((t, w), jnp.float32), pltpu.VMEM((t, w), jnp.float32),
                        pltpu.VMEM((1, w), jnp.float32)],
        compiler_params=_cparams("parallel", "arbitrary"),
        name="lru_fwd",
    )(u_all, u_all, u_all, h0, cw, cb, wa, ba, wx, bx, lam)


def _lru_bwd_out(uc, h0, wa, ba, wx, bx, lam, hf, gate, xc, modsel, wout, ln_g, ln_b):
    b, n, w = uc.shape
    d = xc.shape[2]
    t = min(SCAN_TILE, n)
    n_tiles = n // t
    full = lambda a: pl.BlockSpec(a.shape, lambda bi, s: (0,) * a.ndim)
    tile_spec = lambda width: pl.BlockSpec((1, t, width), lambda bi, s: (bi, n_tiles - 1 - s, 0))
    return pl.pallas_call(
        _lru_bwd_kernel,
        grid=(b, n_tiles),
        in_specs=[
            tile_spec(w),
            pl.BlockSpec((1, 2, w), lambda bi, s: (bi, 0, 0)),
            full(wa), full(ba), full(wx), full(bx), full(lam),
            tile_spec(w), tile_spec(w), tile_spec(d),
            pl.BlockSpec((1, 1, 3, d), lambda bi, s: (bi, 0, 0, 0)),
            full(wout), full(ln_g), full(ln_b),
        ],
        out_specs=tile_spec(d),
        out_shape=jax.ShapeDtypeStruct((b, n, d), jnp.float32),
        scratch_shapes=[pltpu.VMEM((t, w), jnp.float32)] * 3 + [pltpu.VMEM((1, w), jnp.float32)],
        compiler_params=_cparams("parallel", "arbitrary"),
        name="lru_bwd_out",
    )(uc, h0, wa, ba, wx, bx, lam, hf, gate, xc, modsel, wout, ln_g, ln_b)


def _rope_tables(n, c):
    t = np.arange(n)
    rows = (t // GRID_W).astype(np.float32)
    cols = (t % GRID_W).astype(np.float32)

    def ang(rot_dim):
        n_freq = rot_dim // 4
        freqs = np.float32(ROPE_THETA) ** (-np.arange(n_freq, dtype=np.float32) / np.float32(n_freq))
        a = np.concatenate([rows[:, None] * freqs, cols[:, None] * freqs], -1)
        return np.cos(a.astype(np.float64)), np.sin(a.astype(np.float64))

    (cos_a, sin_a), (cos_b, sin_b) = ang(MLA_ROPE), ang(DIFF_QK)
    one = np.ones((n, MLA_NOPE))
    zero = np.zeros((n, MLA_NOPE))
    pad1 = np.ones((n, LANE - MLA_QK))
    pad0 = np.zeros((n, LANE - MLA_QK))
    ca = np.concatenate([one, cos_a, cos_a, pad1], 1)
    sa = np.concatenate([zero, -sin_a, sin_a, pad0], 1)
    cb = np.concatenate([cos_b, cos_b] * 2, 1)
    sb = np.concatenate([-sin_b, sin_b] * 2, 1)
    ident = lambda tab, v: jnp.asarray(np.concatenate([tab, np.full((c, LANE), v)], 0), jnp.float32)
    return ident(ca, 1.0), ident(sa, 0.0), ident(cb, 1.0), ident(sb, 0.0)


def _even_weights(w_in, w_uq, w_ukv):
    d = w_in.shape[0]
    bf = jnp.bfloat16
    o = [0, 384, 640, 672, 1184, 1696, 2208, 3232]
    cq, ckv, kr, dq, dk, dv, gate = (w_in[:, o[i]:o[i + 1]] for i in range(7))
    kr_group = jnp.concatenate([jnp.zeros((d, MLA_NOPE), w_in.dtype), kr,
                                jnp.zeros((d, LANE - MLA_QK), w_in.dtype)], 1)
    w1 = jnp.concatenate([cq, ckv, kr_group, dq, dk, dv, gate], 1).astype(bf)
    wuq = jnp.pad(w_uq.reshape(MLA_Q_LORA, MLA_HEADS, MLA_QK),
                  ((0, 0), (0, 0), (0, LANE - MLA_QK))).reshape(MLA_Q_LORA, MLA_HEADS * LANE).astype(bf)
    ukv = w_ukv.reshape(MLA_KV_LORA, MLA_HEADS, MLA_NOPE + MLA_V)
    wuk = jnp.pad(ukv[..., :MLA_NOPE], ((0, 0), (0, 0), (0, LANE - MLA_NOPE))).reshape(MLA_KV_LORA, MLA_HEADS * LANE)
    wuv = ukv[..., MLA_NOPE:].reshape(MLA_KV_LORA, MLA_HEADS * MLA_V)
    wkv = jnp.concatenate([wuk, wuv], 1).astype(bf)
    return w1, wuq, wkv


def _mod_select(mods_l, b):
    d = mods_l.shape[1] // 3
    mx = mods_l[:b].reshape(b, 3, d)
    mc = jnp.broadcast_to(mods_l[b].reshape(1, 3, d), (b, 3, d))
    return jnp.stack([mx, mc], axis=1)


def kernel(x, c, ctx, c_ctx, ada_w, ada_b, post_ln_g, post_ln_b, e_w_in, e_q_norm_g, e_w_uq, e_kv_norm_g, e_w_ukv, e_lam_q1, e_lam_k1, e_lam_q2, e_lam_k2, e_subln_g, e_w_out, o_w_in, o_conv_w, o_conv_b, o_gate_a_w, o_gate_a_b, o_gate_x_w, o_gate_x_b, o_lru_lambda, o_w_out):
    b, n, d = x.shape
    cl = ctx.shape[1]
    assert ada_w.shape[0] == DEPTH and b < SUBLANE
    assert n % Q_TILE == 0 and cl % TOKEN_TILE == 0 and n % cl == 0
    bf = jnp.bfloat16

    cond = jnp.concatenate([c, c_ctx[None], jnp.zeros((SUBLANE - b - 1, d), c.dtype)], 0)
    lamv = jnp.concatenate([e_lam_q1, e_lam_k1, e_lam_q2, e_lam_k2], 0)
    mods, lam = _adaln(cond, ada_w, ada_b, lamv)
    mod0, mod1 = _mod_select(mods[0], b), _mod_select(mods[1], b)

    w1, wuq, wkv = _even_weights(e_w_in[0], e_w_uq[0], e_w_ukv[0])
    tabs = _rope_tables(n, cl)
    qt, ka, vat, dqt, dk, dvt, gate = _even_proj(
        x, ctx, mod0, w1, e_q_norm_g[0][None], wuq, e_kv_norm_g[0][None], wkv, tabs)
    subln_col = e_subln_g[0][:, None]
    oax = _mla_attn(qt, ka, vat, n, n + cl)
    obx = _diff_attn(dqt, dk, dvt, lam, subln_col, n, n + cl)
    oac = _mla_attn(qt, ka, vat, cl, cl, tail=True)
    obc = _diff_attn(dqt, dk, dvt, lam, subln_col, cl, cl, tail=True)
    xc = _even_merge(x, ctx, mod0, oax, oac, obx, obc, gate, e_w_out[0].astype(bf),
                     post_ln_g[0][None], post_ln_b[0][None])

    u_all, gate1 = _odd_proj(xc, mod1, o_w_in[0].astype(bf), n // TOKEN_TILE)
    cw, cb = o_conv_w[0], o_conv_b[0][None]
    wa, wx = o_gate_a_w[0].astype(bf), o_gate_x_w[0].astype(bf)
    w = u_all.shape[2]
    ba = o_gate_a_b[0].reshape(2, 1, w)
    bx = o_gate_x_b[0].reshape(2, 1, w)
    lru_lam = o_lru_lambda[0].reshape(2, 1, w)
    h0 = _lru_ctx(u_all, n, cl, cw, cb, wa, ba, wx, bx, lru_lam)
    hf, uc = _lru_fwd(u_all, n, h0, cw, cb, wa, ba, wx, bx, lru_lam)
    return _lru_bwd_out(uc, h0, wa, ba, wx, bx, lru_lam, hf, gate1, xc, mod1, o_w_out[0].astype(bf),
                        post_ln_g[1][None], post_ln_b[1][None])
```

```python
import functools
import math

import jax
import jax.numpy as jnp
import numpy as np
from jax import lax
from jax.experimental import pallas as pl
from jax.experimental.pallas import tpu as pltpu

GRID_W = 64
ROPE_THETA = 10000.0
LN_EPS = 1e-6
RMS_EPS = 1e-6

MLA_HEADS = 8
MLA_Q_LORA = 384
MLA_KV_LORA = 256
MLA_NOPE = 64
MLA_ROPE = 32
MLA_V = 64
MLA_QK = MLA_NOPE + MLA_ROPE
MLA_SCALE = MLA_QK ** -0.5

DIFF_HEADS = 4
DIFF_QK = 64
DIFF_V = 2 * DIFF_QK
DIFF_SCALE = DIFF_QK ** -0.5

LRU_BLOCKS = 8
LRU_C = 8.0
CONV_W = 4

DEPTH = 2
DEEPNORM_ALPHA = (2 * DEPTH) ** 0.25
LAMBDA_INIT_0 = 0.8 - 0.6 * math.exp(-0.3 * 0)

LANE = 128
SUBLANE = 8
MXU_DEPTH = 256
LOG2E = 1.4426950408889634
NEG_BIG = -1e30

TOKEN_TILE = 256
Q_TILE = 512
KV_CHUNK = 1408
SCAN_TILE = 512
VMEM_LIMIT = 56 * 1024 * 1024

_G_CQ = (0, 384)
_G_CKV = (384, 640)
_G_KR = (640, 768)
_G_DQ = (768, 1280)
_G_DK = (1280, 1792)
_G_DV = (1792, 2304)
_G_GATE = (2304, 3328)
_EVEN_W = 3328


def _cparams(*sem):
    return pltpu.CompilerParams(dimension_semantics=sem, vmem_limit_bytes=VMEM_LIMIT)


def _bf16_dot(a, b):
    return jnp.dot(a.astype(jnp.bfloat16), b.astype(jnp.bfloat16), preferred_element_type=jnp.float32)


def _sigmoid(x):
    return 1.0 / (1.0 + jnp.exp(-x))


def _adaln_kernel(cond_ref, w_ref, b_ref, lamv_ref, mod_ref, lam_ref):
    cond = cond_ref[...]
    h = cond * _sigmoid(cond)
    mod_ref[0] = jnp.dot(h, w_ref[0], preferred_element_type=jnp.float32,
                         precision=lax.Precision.HIGHEST) + b_ref[0]
    lv = lamv_ref[...]
    d1 = jnp.sum(lv[0:1] * lv[1:2], axis=-1, keepdims=True)
    d2 = jnp.sum(lv[2:3] * lv[3:4], axis=-1, keepdims=True)
    lam = jnp.exp(d1) - jnp.exp(d2) + LAMBDA_INIT_0
    lam_ref[...] = jnp.broadcast_to(lam, lam_ref.shape)


def _adaln(cond, ada_w, ada_b, lamv):
    depth, d, d3 = ada_w.shape
    nj = d3 // d
    return pl.pallas_call(
        _adaln_kernel,
        grid=(depth, nj),
        in_specs=[
            pl.BlockSpec((SUBLANE, d), lambda l, j: (0, 0)),
            pl.BlockSpec((1, d, d), lambda l, j: (l, 0, j)),
            pl.BlockSpec((1, 1, d), lambda l, j: (l, 0, j)),
            pl.BlockSpec(lamv.shape, lambda l, j: (0, 0)),
        ],
        out_specs=[
            pl.BlockSpec((1, SUBLANE, d), lambda l, j: (l, 0, j)),
            pl.BlockSpec((SUBLANE, LANE), lambda l, j: (0, 0)),
        ],
        out_shape=[
            jax.ShapeDtypeStruct((depth, SUBLANE, d3), jnp.float32),
            jax.ShapeDtypeStruct((SUBLANE, LANE), jnp.float32),
        ],
        compiler_params=_cparams("arbitrary", "arbitrary"),
        name="adaln",
    )(cond, ada_w, ada_b.reshape(depth, 1, d3), lamv)


def _rms(x, g):
    return x * lax.rsqrt(jnp.mean(x * x, axis=-1, keepdims=True) + RMS_EPS) * g


def _rope_group(x, cos, sin, half, first_half_mask):
    partner = jnp.where(first_half_mask, pltpu.roll(x, LANE - half, 1), pltpu.roll(x, half, 1))
    return x * cos + partner * sin


def _even_proj_kernel(nx, x_ref, c_ref, mod_ref, w1_ref, qg_ref, wuq_ref, kvg_ref, wkv_ref,
                      ca_ref, sa_ref, cb_ref, sb_ref,
                      qt_ref, ka_ref, vat_ref, dqt_ref, dk_ref, dvt_ref, gate_ref):
    i = pl.program_id(1)
    xin = jnp.where(i >= nx, c_ref[0], x_ref[0])
    mod = mod_ref[0, 0]
    xm = (xin * (1.0 + mod[1:2]) + mod[0:1]).astype(w1_ref.dtype)

    def zcols(lo, hi):
        return jnp.dot(xm, w1_ref[:, lo:hi], preferred_element_type=jnp.float32)

    lane = lax.broadcasted_iota(jnp.int32, (1, LANE), 1)
    mla_first = jnp.logical_and(lane >= MLA_NOPE, lane < MLA_NOPE + MLA_ROPE // 2)
    diff_first = (lane % DIFF_QK) < DIFF_QK // 2
    ca, sa, cb, sb = ca_ref[...], sa_ref[...], cb_ref[...], sb_ref[...]

    z_lat = zcols(_G_CQ[0], _G_KR[1])
    z_dq = zcols(*_G_DQ)
    z_dk = zcols(*_G_DK)

    cqn = _rms(z_lat[:, _G_CQ[0]:_G_CQ[1]], qg_ref[...])
    q = _bf16_dot(cqn, wuq_ref[...])
    ckvn = _rms(z_lat[:, _G_CKV[0]:_G_CKV[1]], kvg_ref[...])
    kv = _bf16_dot(ckvn, wkv_ref[...])
    z_dv = zcols(*_G_DV)
    gate_ref[0] = zcols(*_G_GATE).astype(gate_ref.dtype)

    q_heads = []
    for h in range(MLA_HEADS):
        qh = _rope_group(q[:, h * LANE:(h + 1) * LANE], ca, sa, MLA_ROPE // 2, mla_first)
        q_heads.append(qh * (MLA_SCALE * LOG2E))
    qt_ref[0] = jnp.concatenate(q_heads, axis=1).T.astype(qt_ref.dtype)

    kr = _rope_group(z_lat[:, _G_KR[0]:_G_KR[1]], ca, sa, MLA_ROPE // 2, mla_first)
    for h in range(MLA_HEADS):
        ka_ref[0, :, h * LANE:(h + 1) * LANE] = (kv[:, h * LANE:(h + 1) * LANE] + kr).astype(ka_ref.dtype)
    vat_ref[0] = kv[:, MLA_HEADS * LANE:].T.astype(vat_ref.dtype)

    dq_heads = []
    for h in range(DIFF_HEADS):
        sl = slice(h * LANE, (h + 1) * LANE)
        dq_heads.append(_rope_group(z_dq[:, sl], cb, sb, DIFF_QK // 2, diff_first) * (DIFF_SCALE * LOG2E))
        dk_ref[0, :, sl] = _rope_group(z_dk[:, sl], cb, sb, DIFF_QK // 2, diff_first).astype(dk_ref.dtype)
    dqt_ref[0] = jnp.concatenate(dq_heads, axis=1).T.astype(dqt_ref.dtype)
    dvt_ref[0] = z_dv.T.astype(dvt_ref.dtype)


def _even_proj(x, ctx, modsel, w1, qg, wuq, kvg, wkv, tabs):
    b, n, d = x.shape
    c = ctx.shape[1]
    t = TOKEN_TILE
    nx, nc = n // t, c // t
    nt = n + c
    full = lambda a: pl.BlockSpec(a.shape, lambda bi, i: (0,) * a.ndim)
    tab_spec = pl.BlockSpec((t, LANE), lambda bi, i: (i, 0))
    tok = lambda w: pl.BlockSpec((1, t, w), lambda bi, i: (bi, i, 0))
    tok_t = lambda w: pl.BlockSpec((1, w, t), lambda bi, i: (bi, 0, i))
    bf = jnp.bfloat16
    return pl.pallas_call(
        functools.partial(_even_proj_kernel, nx),
        grid=(b, nx + nc),
        in_specs=[
            pl.BlockSpec((1, t, d), lambda bi, i: (bi, jnp.minimum(i, nx - 1), 0)),
            pl.BlockSpec((1, t, d), lambda bi, i: (bi, jnp.maximum(i - nx, 0), 0)),
            pl.BlockSpec((1, 1, 3, d), lambda bi, i: (bi, jnp.where(i >= nx, 1, 0), 0, 0)),
            full(w1), full(qg), full(wuq), full(kvg), full(wkv),
            tab_spec, tab_spec, tab_spec, tab_spec,
        ],
        out_specs=[
            tok_t(MLA_HEADS * LANE), tok(MLA_HEADS * LANE), tok_t(MLA_HEADS * MLA_V),
            tok_t(DIFF_HEADS * LANE), tok(DIFF_HEADS * LANE), tok_t(DIFF_HEADS * DIFF_V),
            tok(MLA_HEADS * MLA_V + DIFF_HEADS * DIFF_V),
        ],
        out_shape=[
            jax.ShapeDtypeStruct((b, MLA_HEADS * LANE, nt), bf),
            jax.ShapeDtypeStruct((b, nt, MLA_HEADS * LANE), bf),
            jax.ShapeDtypeStruct((b, MLA_HEADS * MLA_V, nt), bf),
            jax.ShapeDtypeStruct((b, DIFF_HEADS * LANE, nt), bf),
            jax.ShapeDtypeStruct((b, nt, DIFF_HEADS * LANE), bf),
            jax.ShapeDtypeStruct((b, DIFF_HEADS * DIFF_V, nt), bf),
            jax.ShapeDtypeStruct((b, nt, MLA_HEADS * MLA_V + DIFF_HEADS * DIFF_V), bf),
        ],
        compiler_params=_cparams("parallel", "parallel"),
        name="even_proj",
    )(x, ctx, modsel, w1, qg, wuq, kvg, wkv, *tabs)


def _kv_chunk(nk):
    if nk <= KV_CHUNK:
        return nk
    for m in range(KV_CHUNK // LANE, 0, -1):
        if nk % (m * LANE) == 0 and (nk // (m * LANE)) % 2 == 0:
            return m * LANE
    raise ValueError(f"no even chunking of {nk} keys")


def _attn_sweep(n_chunks, tk, tq, n_qt, q_of, k_ref, vt_ref, s_scr, finish):
    dv = vt_ref.shape[1]

    def scores(t, j, slot):
        k = k_ref[0, pl.ds(pl.multiple_of(j * tk, LANE), tk), :]
        cms = []
        for si, qt in enumerate(q_of(t)):
            s = jnp.dot(k, qt, preferred_element_type=jnp.float32)
            s_scr[slot, si] = s
            cms.append(jnp.max(s, axis=0, keepdims=True))
        return tuple(cms)

    n_ops = s_scr.shape[1]
    ones = jnp.ones((2 * SUBLANE, MXU_DEPTH), vt_ref.dtype)
    init = (jnp.full((1, tq), NEG_BIG, jnp.float32), jnp.zeros((dv + 2 * SUBLANE, tq), jnp.float32)) * n_ops
    blocks = [(r0, min(MXU_DEPTH, tk - r0)) for r0 in range(0, tk, MXU_DEPTH)]

    def step(nxt, j_cur, slot_cur, cms, state):
        off_cur = pl.multiple_of(j_cur * tk, LANE)
        if nxt is not None:
            t_nxt, j_nxt, slot_nxt = nxt
            off_nxt = pl.multiple_of(j_nxt * tk, LANE)
            q_nxt = q_of(t_nxt)
        m_new = [jnp.maximum(state[2 * si], cms[si]) for si in range(n_ops)]
        cm_nxt, pv = [None] * n_ops, [None] * n_ops
        for r0, r in blocks:
            if nxt is not None:
                k = k_ref[0, pl.ds(off_nxt + r0, r), :]
            vt = vt_ref[0, :, pl.ds(off_cur + r0, r)]
            vt_ext = jnp.concatenate([vt, ones[:, :r]], axis=0)
            for si in range(n_ops):
                if nxt is not None:
                    s = jnp.dot(k, q_nxt[si], preferred_element_type=jnp.float32)
                    s_scr[slot_nxt, si, r0:r0 + r] = s
                    cmax = jnp.max(s, axis=0, keepdims=True)
                    cm_nxt[si] = cmax if cm_nxt[si] is None else jnp.maximum(cm_nxt[si], cmax)
                p = jnp.exp2(s_scr[slot_cur, si, r0:r0 + r] - m_new[si]).astype(vt.dtype)
                d = jnp.dot(vt_ext, p, preferred_element_type=jnp.float32)
                pv[si] = d if pv[si] is None else pv[si] + d
        out = []
        for si in range(n_ops):
            alpha = jnp.exp2(state[2 * si] - m_new[si])
            out += [m_new[si], alpha * state[2 * si + 1] + pv[si]]
        return tuple(cm_nxt), tuple(out)

    def result(st):
        return [st[2 * si + 1][:dv] / st[2 * si + 1][dv:dv + 1] for si in range(n_ops)]

    if n_chunks == 1:
        def tile(t, carry):
            finish(t, result(step(None, 0, 0, scores(t, 0, 0), init)[1]))
            return carry
        lax.fori_loop(0, n_qt, tile, 0)
        return

    assert n_chunks % 2 == 0

    def tile(t, cm, last):
        def pair(i, carry):
            cm_a, st = carry
            cm_b, st = step((t, 2 * i + 1, 1), 2 * i, 0, cm_a, st)
            cm_a, st = step((t, 2 * i + 2, 0), 2 * i + 1, 1, cm_b, st)
            return cm_a, st

        cm_a, st = lax.fori_loop(0, n_chunks // 2 - 1, pair, (cm, init), unroll=True)
        cm_b, st = step((t, n_chunks - 1, 1), n_chunks - 2, 0, cm_a, st)
        cm_next, st = step(None if last else (t + 1, 0, 0), n_chunks - 1, 1, cm_b, st)
        finish(t, result(st))
        return cm_next

    cm = lax.fori_loop(0, n_qt - 1, lambda t, cm: tile(t, cm, False), scores(0, 0, 0), unroll=2)
    tile(jnp.int32(n_qt - 1), cm, True)


def _q_cols(t, tq):
    return pl.ds(pl.multiple_of(t * tq, LANE), tq)


def _mla_attn_kernel(n_chunks, tk, tq, qt_ref, k_ref, vt_ref, o_ref, s_scr):
    n_ops = s_scr.shape[1]

    def q_of(t):
        return [qt_ref[0, :, _q_cols(t * n_ops + i, tq)] for i in range(n_ops)]

    def finish(t, outs):
        for i in range(n_ops):
            o_ref[0, :, _q_cols(t * n_ops + i, tq)] = outs[i].astype(o_ref.dtype)

    _attn_sweep(n_chunks, tk, tq, qt_ref.shape[2] // (tq * n_ops), q_of, k_ref, vt_ref, s_scr, finish)


def _attn_call(kernel_fn, name, heads, dv, n_ops, qt, k, vt, nq, nk, tail, extra=()):
    b, _, nt = vt.shape
    assert not tail or ((nt - nq) % nq == 0 and (nt - nk) % nk == 0)
    qb = (nt - nq) // nq if tail else 0
    kb = (nt - nk) // nk if tail else 0
    tq = min(Q_TILE, nq)
    tk = _kv_chunk(nk)
    return pl.pallas_call(
        functools.partial(kernel_fn, nk // tk, tk, tq),
        grid=(b, heads),
        in_specs=[
            pl.BlockSpec((1, LANE, nq), lambda bi, h: (bi, h, qb)),
            pl.BlockSpec((1, nk, LANE), lambda bi, h: (bi, kb, h)),
            pl.BlockSpec((1, dv, nk), lambda bi, h: (bi, h, kb)),
        ] + [pl.BlockSpec(a.shape, lambda bi, h: (0, 0)) for a in extra],
        out_specs=pl.BlockSpec((1, dv, nq), lambda bi, h: (bi, h, 0)),
        out_shape=jax.ShapeDtypeStruct((b, heads * dv, nq), jnp.bfloat16),
        scratch_shapes=[pltpu.VMEM((2, n_ops, tk, tq), jnp.float32)],
        compiler_params=_cparams("parallel", "parallel"),
        name=name,
    )(qt, k, vt, *extra)


def _mla_attn(qt, ka, vat, nq, nk, tail=False):
    n_ops = 2 if nq % (2 * Q_TILE) == 0 else 1
    return _attn_call(_mla_attn_kernel, "mla_attn", MLA_HEADS, MLA_V, n_ops, qt, ka, vat, nq, nk, tail)


def _diff_attn_kernel(n_chunks, tk, tq, qt_ref, k_ref, vt_ref, lam_ref, g_ref, o_ref, s_scr):
    row = lax.broadcasted_iota(jnp.int32, (LANE, 1), 0)

    def q_of(t):
        qt = qt_ref[0, :, _q_cols(t, tq)]
        zero = jnp.zeros_like(qt)
        return [jnp.where(row < DIFF_QK, qt, zero), jnp.where(row >= DIFF_QK, qt, zero)]

    def finish(t, outs):
        o = outs[0] - lam_ref[0:1, 0:1] * outs[1]
        o = o * lax.rsqrt(jnp.mean(o * o, axis=0, keepdims=True) + RMS_EPS) * g_ref[...]
        o_ref[0, :, _q_cols(t, tq)] = (o * (1.0 - LAMBDA_INIT_0)).astype(o_ref.dtype)

    _attn_sweep(n_chunks, tk, tq, qt_ref.shape[2] // tq, q_of, k_ref, vt_ref, s_scr, finish)


def _diff_attn(dqt, dk, dvt, lam, subln_col, nq, nk, tail=False):
    return _attn_call(_diff_attn_kernel, "diff_attn", DIFF_HEADS, DIFF_V, 2, dqt, dk, dvt, nq, nk, tail,
                      extra=(lam, subln_col))


def _residual_ln(xin, y, gate_vec, ln_g, ln_b):
    r = DEEPNORM_ALPHA * xin + gate_vec * y
    mu = jnp.mean(r, axis=-1, keepdims=True)
    rc = r - mu
    var = jnp.mean(rc * rc, axis=-1, keepdims=True)
    return rc * lax.rsqrt(var + LN_EPS) * ln_g + ln_b


def _even_merge_kernel(nx, x_ref, c_ref, mod_ref, oax_ref, oac_ref, obx_ref, obc_ref, gate_ref,
                       wout_ref, lng_ref, lnb_ref, mod1_ref, w1_ref, out_ref, u_ref, gate1_ref):
    i = pl.program_id(1)
    is_ctx = i >= nx
    xin = jnp.where(is_ctx, c_ref[0], x_ref[0])
    oa = jnp.where(is_ctx, oac_ref[0], oax_ref[0]).astype(jnp.float32).T
    ob = jnp.where(is_ctx, obc_ref[0], obx_ref[0]).astype(jnp.float32).T
    o = jnp.concatenate([oa, ob], axis=1)
    g = gate_ref[0].astype(jnp.float32)
    y = _bf16_dot(o * (g * _sigmoid(g)), wout_ref[...])
    xc = _residual_ln(xin, y, mod_ref[0, 0][2:3], lng_ref[...], lnb_ref[...])
    out_ref[0] = xc

    mod1 = mod1_ref[0, 0]
    z = _bf16_dot(xc * (1.0 + mod1[1:2]) + mod1[0:1], w1_ref[...])
    w = u_ref.shape[2]
    u_ref[0] = z[:, :w]
    gate1_ref[0] = z[:, w:].astype(gate1_ref.dtype)


def _even_merge(x, ctx, modsel, oax, oac, obx, obc, gate, wout, ln_g, ln_b, modsel1, w_in1):
    b, n, d = x.shape
    c = ctx.shape[1]
    t = TOKEN_TILE
    nx, nc = n // t, c // t
    wa, wb = oax.shape[1], obx.shape[1]
    w = w_in1.shape[1] // 2
    full = lambda a: pl.BlockSpec(a.shape, lambda bi, i: (0,) * a.ndim)
    xi = lambda bi, i: (bi, jnp.minimum(i, nx - 1), 0)
    ci = lambda bi, i: (bi, jnp.maximum(i - nx, 0), 0)
    xit = lambda bi, i: (bi, 0, jnp.minimum(i, nx - 1))
    cit = lambda bi, i: (bi, 0, jnp.maximum(i - nx, 0))
    mod_spec = pl.BlockSpec((1, 1, 3, d), lambda bi, i: (bi, jnp.where(i >= nx, 1, 0), 0, 0))
    tok = lambda width: pl.BlockSpec((1, t, width), lambda bi, i: (bi, i, 0))
    return pl.pallas_call(
        functools.partial(_even_merge_kernel, nx),
        grid=(b, nx + nc),
        in_specs=[
            pl.BlockSpec((1, t, d), xi),
            pl.BlockSpec((1, t, d), ci),
            mod_spec,
            pl.BlockSpec((1, wa, t), xit), pl.BlockSpec((1, wa, t), cit),
            pl.BlockSpec((1, wb, t), xit), pl.BlockSpec((1, wb, t), cit),
            tok(wa + wb),
            full(wout), full(ln_g), full(ln_b),
            mod_spec, full(w_in1),
        ],
        out_specs=[tok(d), tok(w), tok(w)],
        out_shape=[jax.ShapeDtypeStruct((b, n + c, d), jnp.float32),
                   jax.ShapeDtypeStruct((b, n + c, w), jnp.float32),
                   jax.ShapeDtypeStruct((b, n + c, w), jnp.bfloat16)],
        compiler_params=_cparams("parallel", "parallel"),
        name="even_merge",
    )(x, ctx, modsel, oax, oac, obx, obc, gate, wout, ln_g, ln_b, modsel1, w_in1)


def _conv_tile(u, prev, nxt, cw, cb):
    t = u.shape[0]
    row = lax.broadcasted_iota(jnp.int32, (SUBLANE, 1), 0)
    r1, r2, rp = pltpu.roll(u, 1, 0), pltpu.roll(u, 2, 0), pltpu.roll(u, t - 1, 0)
    m1 = jnp.concatenate([jnp.where(row == 0, prev[7:8], r1[:SUBLANE]), r1[SUBLANE:]], 0)
    m2 = jnp.concatenate([jnp.where(row == 0, prev[6:7], jnp.where(row == 1, prev[7:8], r2[:SUBLANE])),
                          r2[SUBLANE:]], 0)
    p1 = jnp.concatenate([rp[:t - SUBLANE], jnp.where(row == SUBLANE - 1, nxt[0:1], rp[t - SUBLANE:])], 0)
    return cw[0:1] * m2 + cw[1:2] * m1 + cw[2:3] * u + cw[3:4] * p1 + cb


def _lru_coeffs(uc, wa_ref, ba, wx_ref, bx, lam, a_ref, b_ref):
    neg_log_a1 = LRU_C * jnp.log1p(jnp.exp(-lam))
    for k in range(LRU_BLOCKS):
        sl = slice(k * LANE, (k + 1) * LANE)
        ub = uc[:, sl]
        r = _sigmoid(_bf16_dot(ub, wa_ref[k]) + ba[:, sl])
        ig = _sigmoid(_bf16_dot(ub, wx_ref[k]) + bx[:, sl])
        a = jnp.exp2(r * (-LOG2E * neg_log_a1[:, sl]))
        a_ref[:, sl] = a
        y = jnp.tanh(r * neg_log_a1[:, sl]) * (a * a + 1.0)
        root = jnp.where(y > 0.0, y * lax.rsqrt(y), 0.0)
        b_ref[:, sl] = root * (ig * ub)


def _scan_rows(a_ref, b_ref, h_out_ref, h0, t, reverse):
    def step(s, h):
        r = (t - 1 - s) if reverse else s
        h = a_ref[pl.ds(r, 1), :] * h + b_ref[pl.ds(r, 1), :]
        h_out_ref[pl.ds(r, 1), :] = h
        return h

    return lax.fori_loop(0, t, step, h0, unroll=8)


def _lru_ctx_kernel(u_ref, cw_ref, cb_ref, wa_ref, ba_ref, wx_ref, bx_ref, lam_ref, h_ref,
                    a_s, b_s, hs):
    u = u_ref[0]
    zero8 = jnp.zeros((SUBLANE, u.shape[1]), jnp.float32)
    uc = _conv_tile(u, zero8, zero8, cw_ref[...], cb_ref[...])
    t = u.shape[0]
    for d in range(2):
        _lru_coeffs(uc, wa_ref.at[d], ba_ref[d], wx_ref.at[d], bx_ref[d], lam_ref[d], a_s, b_s)
        h = _scan_rows(a_s, b_s, hs, jnp.zeros((1, u.shape[1]), jnp.float32), t, reverse=(d == 1))
        h_ref[0, d:d + 1, :] = h


def _lru_ctx(u_all, n, c, cw, cb, wa, ba, wx, bx, lam):
    b, _, w = u_all.shape
    full = lambda a: pl.BlockSpec(a.shape, lambda bi: (0,) * a.ndim)
    return pl.pallas_call(
        _lru_ctx_kernel,
        grid=(b,),
        in_specs=[pl.BlockSpec((1, c, w), lambda bi: (bi, n // c, 0)),
                  full(cw), full(cb), full(wa), full(ba), full(wx), full(bx), full(lam)],
        out_specs=pl.BlockSpec((1, 2, w), lambda bi: (bi, 0, 0)),
        out_shape=jax.ShapeDtypeStruct((b, 2, w), jnp.float32),
        scratch_shapes=[pltpu.VMEM((c, w), jnp.float32)] * 3,
        compiler_params=_cparams("parallel"),
        name="lru_ctx",
    )(u_all, cw, cb, wa, ba, wx, bx, lam)


def _lru_fwd_kernel(n_tiles, u_ref, up_ref, un_ref, h0_ref, cw_ref, cb_ref, wa_ref, ba_ref, wx_ref, bx_ref,
                    lam_ref, hf_ref, uc_ref, a_s, b_s, carry):
    tile = pl.program_id(1)

    @pl.when(tile == 0)
    def _():
        carry[...] = h0_ref[0, 0:1, :]

    u = u_ref[0]
    prev = jnp.where(tile == 0, 0.0, up_ref[0])
    nxt = jnp.where(tile == n_tiles - 1, 0.0, un_ref[0])
    uc = _conv_tile(u, prev, nxt, cw_ref[...], cb_ref[...])
    uc_ref[0] = uc
    _lru_coeffs(uc, wa_ref.at[0], ba_ref[0], wx_ref.at[0], bx_ref[0], lam_ref[0], a_s, b_s)
    carry[...] = _scan_rows(a_s, b_s, hf_ref.at[0], carry[...], u.shape[0], reverse=False)


def _lru_bwd_kernel(uc_ref, h0_ref, wa_ref, ba_ref, wx_ref, bx_ref, lam_ref, hf_ref, gate_ref, x_ref, mod_ref,
                    wout_ref, lng_ref, lnb_ref, out_ref, a_s, b_s, hs, carry):
    @pl.when(pl.program_id(1) == 0)
    def _():
        carry[...] = h0_ref[0, 1:2, :]

    uc = uc_ref[0]
    _lru_coeffs(uc, wa_ref.at[1], ba_ref[1], wx_ref.at[1], bx_ref[1], lam_ref[1], a_s, b_s)
    carry[...] = _scan_rows(a_s, b_s, hs, carry[...], uc.shape[0], reverse=True)
    g = gate_ref[0].astype(jnp.float32)
    hx = hf_ref[0] + hs[...]
    y = _bf16_dot(hx * (g * _sigmoid(g)), wout_ref[...])
    out_ref[0] = _residual_ln(x_ref[0], y, mod_ref[0, 0][2:3], lng_ref[...], lnb_ref[...])


def _lru_fwd(u_all, n, h0, cw, cb, wa, ba, wx, bx, lam):
    b, _, w = u_all.shape
    t = min(SCAN_TILE, n)
    n_tiles = n // t
    r = t // SUBLANE
    full = lambda a: pl.BlockSpec(a.shape, lambda bi, s: (0,) * a.ndim)
    tile_spec = pl.BlockSpec((1, t, w), lambda bi, s: (bi, s, 0))
    return pl.pallas_call(
        functools.partial(_lru_fwd_kernel, n_tiles),
        grid=(b, n_tiles),
        in_specs=[
            tile_spec,
            pl.BlockSpec((1, SUBLANE, w), lambda bi, s: (bi, jnp.maximum(s * r - 1, 0), 0)),
            pl.BlockSpec((1, SUBLANE, w), lambda bi, s: (bi, (s + 1) * r, 0)),
            pl.BlockSpec((1, 2, w), lambda bi, s: (bi, 0, 0)),
            full(cw), full(cb), full(wa), full(ba), full(wx), full(bx), full(lam),
        ],
        out_specs=[tile_spec, tile_spec],
        out_shape=[jax.ShapeDtypeStruct((b, n, w), jnp.float32)] * 2,
        scratch_shapes=[pltpu.VMEM((t, w), jnp.float32), pltpu.VMEM((t, w), jnp.float32),
                        pltpu.VMEM((1, w), jnp.float32)],
        compiler_params=_cparams("parallel", "arbitrary"),
        name="lru_fwd",
    )(u_all, u_all, u_all, h0, cw, cb, wa, ba, wx, bx, lam)


def _lru_bwd_out(uc, h0, wa, ba, wx, bx, lam, hf, gate, xc, modsel, wout, ln_g, ln_b):
    b, n, w = uc.shape
    d = xc.shape[2]
    t = min(SCAN_TILE, n)
    n_tiles = n // t
    full = lambda a: pl.BlockSpec(a.shape, lambda bi, s: (0,) * a.ndim)
    tile_spec = lambda width: pl.BlockSpec((1, t, width), lambda bi, s: (bi, n_tiles - 1 - s, 0))
    return pl.pallas_call(
        _lru_bwd_kernel,
        grid=(b, n_tiles),
        in_specs=[
            tile_spec(w),
            pl.BlockSpec((1, 2, w), lambda bi, s: (bi, 0, 0)),
            full(wa), full(ba), full(wx), full(bx), full(lam),
            tile_spec(w), tile_spec(w), tile_spec(d),
            pl.BlockSpec((1, 1, 3, d), lambda bi, s: (bi, 0, 0, 0)),
            full(wout), full(ln_g), full(ln_b),
        ],
        out_specs=tile_spec(d),
        out_shape=jax.ShapeDtypeStruct((b, n, d), jnp.float32),
        scratch_shapes=[pltpu.VMEM((t, w), jnp.float32)] * 3 + [pltpu.VMEM((1, w), jnp.float32)],
        compiler_params=_cparams("parallel", "arbitrary"),
        name="lru_bwd_out",
    )(uc, h0, wa, ba, wx, bx, lam, hf, gate, xc, modsel, wout, ln_g, ln_b)


def _rope_tables(n, c):
    t = np.arange(n)
    rows = (t // GRID_W).astype(np.float32)
    cols = (t % GRID_W).astype(np.float32)

    def ang(rot_dim):
        n_freq = rot_dim // 4
        freqs = np.float32(ROPE_THETA) ** (-np.arange(n_freq, dtype=np.float32) / np.float32(n_freq))
        a = np.concatenate([rows[:, None] * freqs, cols[:, None] * freqs], -1)
        return np.cos(a.astype(np.float64)), np.sin(a.astype(np.float64))

    (cos_a, sin_a), (cos_b, sin_b) = ang(MLA_ROPE), ang(DIFF_QK)
    one = np.ones((n, MLA_NOPE))
    zero = np.zeros((n, MLA_NOPE))
    pad1 = np.ones((n, LANE - MLA_QK))
    pad0 = np.zeros((n, LANE - MLA_QK))
    ca = np.concatenate([one, cos_a, cos_a, pad1], 1)
    sa = np.concatenate([zero, -sin_a, sin_a, pad0], 1)
    cb = np.concatenate([cos_b, cos_b] * 2, 1)
    sb = np.concatenate([-sin_b, sin_b] * 2, 1)
    ident = lambda tab, v: jnp.asarray(np.concatenate([tab, np.full((c, LANE), v)], 0), jnp.float32)
    return ident(ca, 1.0), ident(sa, 0.0), ident(cb, 1.0), ident(sb, 0.0)


def _even_weights(w_in, w_uq, w_ukv):
    d = w_in.shape[0]
    bf = jnp.bfloat16
    o = [0, 384, 640, 672, 1184, 1696, 2208, 3232]
    cq, ckv, kr, dq, dk, dv, gate = (w_in[:, o[i]:o[i + 1]] for i in range(7))
    kr_group = jnp.concatenate([jnp.zeros((d, MLA_NOPE), w_in.dtype), kr,
                                jnp.zeros((d, LANE - MLA_QK), w_in.dtype)], 1)
    w1 = jnp.concatenate([cq, ckv, kr_group, dq, dk, dv, gate], 1).astype(bf)
    wuq = jnp.pad(w_uq.reshape(MLA_Q_LORA, MLA_HEADS, MLA_QK),
                  ((0, 0), (0, 0), (0, LANE - MLA_QK))).reshape(MLA_Q_LORA, MLA_HEADS * LANE).astype(bf)
    ukv = w_ukv.reshape(MLA_KV_LORA, MLA_HEADS, MLA_NOPE + MLA_V)
    wuk = jnp.pad(ukv[..., :MLA_NOPE], ((0, 0), (0, 0), (0, LANE - MLA_NOPE))).reshape(MLA_KV_LORA, MLA_HEADS * LANE)
    wuv = ukv[..., MLA_NOPE:].reshape(MLA_KV_LORA, MLA_HEADS * MLA_V)
    wkv = jnp.concatenate([wuk, wuv], 1).astype(bf)
    return w1, wuq, wkv


def _mod_select(mods_l, b):
    d = mods_l.shape[1] // 3
    mx = mods_l[:b].reshape(b, 3, d)
    mc = jnp.broadcast_to(mods_l[b].reshape(1, 3, d), (b, 3, d))
    return jnp.stack([mx, mc], axis=1)


def kernel(x, c, ctx, c_ctx, ada_w, ada_b, post_ln_g, post_ln_b, e_w_in, e_q_norm_g, e_w_uq, e_kv_norm_g, e_w_ukv, e_lam_q1, e_lam_k1, e_lam_q2, e_lam_k2, e_subln_g, e_w_out, o_w_in, o_conv_w, o_conv_b, o_gate_a_w, o_gate_a_b, o_gate_x_w, o_gate_x_b, o_lru_lambda, o_w_out):
    b, n, d = x.shape
    cl = ctx.shape[1]
    assert ada_w.shape[0] == DEPTH and b < SUBLANE
    assert n % Q_TILE == 0 and cl % TOKEN_TILE == 0 and n % cl == 0
    bf = jnp.bfloat16

    cond = jnp.concatenate([c, c_ctx[None], jnp.zeros((SUBLANE - b - 1, d), c.dtype)], 0)
    lamv = jnp.concatenate([e_lam_q1, e_lam_k1, e_lam_q2, e_lam_k2], 0)
    mods, lam = _adaln(cond, ada_w, ada_b, lamv)
    mod0, mod1 = _mod_select(mods[0], b), _mod_select(mods[1], b)

    w1, wuq, wkv = _even_weights(e_w_in[0], e_w_uq[0], e_w_ukv[0])
    tabs = _rope_tables(n, cl)
    qt, ka, vat, dqt, dk, dvt, gate = _even_proj(
        x, ctx, mod0, w1, e_q_norm_g[0][None], wuq, e_kv_norm_g[0][None], wkv, tabs)
    subln_col = e_subln_g[0][:, None]
    oax = _mla_attn(qt, ka, vat, n, n + cl)
    obx = _diff_attn(dqt, dk, dvt, lam, subln_col, n, n + cl)
    oac = _mla_attn(qt, ka, vat, cl, cl, tail=True)
    obc = _diff_attn(dqt, dk, dvt, lam, subln_col, cl, cl, tail=True)
    xc, u_all, gate1 = _even_merge(x, ctx, mod0, oax, oac, obx, obc, gate, e_w_out[0].astype(bf),
                                   post_ln_g[0][None], post_ln_b[0][None], mod1, o_w_in[0].astype(bf))

    cw, cb = o_conv_w[0], o_conv_b[0][None]
    wa, wx = o_gate_a_w[0].astype(bf), o_gate_x_w[0].astype(bf)
    w = u_all.shape[2]
    ba = o_gate_a_b[0].reshape(2, 1, w)
    bx = o_gate_x_b[0].reshape(2, 1, w)
    lru_lam = o_lru_lambda[0].reshape(2, 1, w)
    h0 = _lru_ctx(u_all, n, cl, cw, cb, wa, ba, wx, bx, lru_lam)
    hf, uc = _lru_fwd(u_all, n, h0, cw, cb, wa, ba, wx, bx, lru_lam)
    return _lru_bwd_out(uc, h0, wa, ba, wx, bx, lru_lam, hf, gate1, xc, mod1, o_w_out[0].astype(bf),
                        post_ln_g[1][None], post_ln_b[1][None])
```

```python
import functools
import math

import jax
import jax.numpy as jnp
import numpy as np
from jax import lax
from jax.experimental import pallas as pl
from jax.experimental.pallas import tpu as pltpu

GRID_W = 64
ROPE_THETA = 10000.0
LN_EPS = 1e-6
RMS_EPS = 1e-6

MLA_HEADS = 8
MLA_Q_LORA = 384
MLA_KV_LORA = 256
MLA_NOPE = 64
MLA_ROPE = 32
MLA_V = 64
MLA_QK = MLA_NOPE + MLA_ROPE
MLA_SCALE = MLA_QK ** -0.5

DIFF_HEADS = 4
DIFF_QK = 64
DIFF_V = 2 * DIFF_QK
DIFF_SCALE = DIFF_QK ** -0.5

LRU_BLOCKS = 8
LRU_C = 8.0

DEPTH = 2
DEEPNORM_ALPHA = (2 * DEPTH) ** 0.25
LAMBDA_INIT_0 = 0.8 - 0.6 * math.exp(-0.3 * 0)

LANE = 128
SUBLANE = 8
MXU_DEPTH = 256
LOG2E = 1.4426950408889634
NEG_BIG = -1e30

TOKEN_TILE = 256
Q_TILE = 512
KV_CHUNK = 1408
SCAN_TILE = 512
VMEM_LIMIT = 56 * 1024 * 1024

_G_CQ = (0, 384)
_G_CKV = (384, 640)
_G_KR = (640, 768)
_G_DQ = (768, 1280)
_G_DK = (1280, 1792)
_G_DV = (1792, 2304)
_G_GATE = (2304, 3328)


def _cparams(*sem):
    return pltpu.CompilerParams(dimension_semantics=sem, vmem_limit_bytes=VMEM_LIMIT)


def _bf16_dot(a, b):
    return jnp.dot(a.astype(jnp.bfloat16), b.astype(jnp.bfloat16), preferred_element_type=jnp.float32)


def _sigmoid(x):
    return 1.0 / (1.0 + jnp.exp(-x))


def _adaln_kernel(cond_ref, w_ref, b_ref, lamv_ref, mod_ref, lam_ref):
    cond = cond_ref[...]
    h = cond * _sigmoid(cond)
    mod_ref[0] = jnp.dot(h, w_ref[0], preferred_element_type=jnp.float32,
                         precision=lax.Precision.HIGHEST) + b_ref[0]
    lv = lamv_ref[...]
    d1 = jnp.sum(lv[0:1] * lv[1:2], axis=-1, keepdims=True)
    d2 = jnp.sum(lv[2:3] * lv[3:4], axis=-1, keepdims=True)
    lam = jnp.exp(d1) - jnp.exp(d2) + LAMBDA_INIT_0
    lam_ref[...] = jnp.broadcast_to(lam, lam_ref.shape)


def _adaln(cond, ada_w, ada_b, lamv):
    depth, d, d3 = ada_w.shape
    nj = d3 // d
    return pl.pallas_call(
        _adaln_kernel,
        grid=(depth, nj),
        in_specs=[
            pl.BlockSpec((SUBLANE, d), lambda l, j: (0, 0)),
            pl.BlockSpec((1, d, d), lambda l, j: (l, 0, j)),
            pl.BlockSpec((1, 1, d), lambda l, j: (l, 0, j)),
            pl.BlockSpec(lamv.shape, lambda l, j: (0, 0)),
        ],
        out_specs=[
            pl.BlockSpec((1, SUBLANE, d), lambda l, j: (l, 0, j)),
            pl.BlockSpec((SUBLANE, LANE), lambda l, j: (0, 0)),
        ],
        out_shape=[
            jax.ShapeDtypeStruct((depth, SUBLANE, d3), jnp.float32),
            jax.ShapeDtypeStruct((SUBLANE, LANE), jnp.float32),
        ],
        compiler_params=_cparams("arbitrary", "arbitrary"),
        name="adaln",
    )(cond, ada_w, ada_b.reshape(depth, 1, d3), lamv)


def _rms(x, g):
    return x * lax.rsqrt(jnp.mean(x * x, axis=-1, keepdims=True) + RMS_EPS) * g


def _rope_group(x, cos, sin, half, first_half_mask):
    partner = jnp.where(first_half_mask, pltpu.roll(x, LANE - half, 1), pltpu.roll(x, half, 1))
    return x * cos + partner * sin


def _even_proj_kernel(nx, x_ref, c_ref, mod_ref, w1_ref, qg_ref, wuq_ref, kvg_ref, wkv_ref,
                      ca_ref, sa_ref, cb_ref, sb_ref,
                      qt_ref, ka_ref, vat_ref, dqt_ref, dk_ref, dvt_ref, gate_ref):
    i = pl.program_id(1)
    xin = jnp.where(i >= nx, c_ref[0], x_ref[0])
    mod = mod_ref[0, 0]
    xm = (xin * (1.0 + mod[1:2]) + mod[0:1]).astype(w1_ref.dtype)

    def zcols(lo, hi):
        return jnp.dot(xm, w1_ref[:, lo:hi], preferred_element_type=jnp.float32)

    lane = lax.broadcasted_iota(jnp.int32, (1, LANE), 1)
    mla_first = jnp.logical_and(lane >= MLA_NOPE, lane < MLA_NOPE + MLA_ROPE // 2)
    diff_first = (lane % DIFF_QK) < DIFF_QK // 2
    ca, sa, cb, sb = ca_ref[...], sa_ref[...], cb_ref[...], sb_ref[...]

    z_lat = zcols(_G_CQ[0], _G_KR[1])
    z_dq = zcols(*_G_DQ)
    z_dk = zcols(*_G_DK)

    cqn = _rms(z_lat[:, _G_CQ[0]:_G_CQ[1]], qg_ref[...])
    q = _bf16_dot(cqn, wuq_ref[...])
    ckvn = _rms(z_lat[:, _G_CKV[0]:_G_CKV[1]], kvg_ref[...])
    kv = _bf16_dot(ckvn, wkv_ref[...])
    z_dv = zcols(*_G_DV)
    gate_ref[0] = zcols(*_G_GATE).astype(gate_ref.dtype)

    q_heads = []
    for h in range(MLA_HEADS):
        qh = _rope_group(q[:, h * LANE:(h + 1) * LANE], ca, sa, MLA_ROPE // 2, mla_first)
        q_heads.append(qh * (MLA_SCALE * LOG2E))
    qt_ref[0] = jnp.concatenate(q_heads, axis=1).T.astype(qt_ref.dtype)

    kr = _rope_group(z_lat[:, _G_KR[0]:_G_KR[1]], ca, sa, MLA_ROPE // 2, mla_first)
    for h in range(MLA_HEADS):
        ka_ref[0, :, h * LANE:(h + 1) * LANE] = (kv[:, h * LANE:(h + 1) * LANE] + kr).astype(ka_ref.dtype)
    vat_ref[0] = kv[:, MLA_HEADS * LANE:].T.astype(vat_ref.dtype)

    dq_heads = []
    for h in range(DIFF_HEADS):
        sl = slice(h * LANE, (h + 1) * LANE)
        dq_heads.append(_rope_group(z_dq[:, sl], cb, sb, DIFF_QK // 2, diff_first) * (DIFF_SCALE * LOG2E))
        dk_ref[0, :, sl] = _rope_group(z_dk[:, sl], cb, sb, DIFF_QK // 2, diff_first).astype(dk_ref.dtype)
    dqt_ref[0] = jnp.concatenate(dq_heads, axis=1).T.astype(dqt_ref.dtype)
    dvt_ref[0] = z_dv.T.astype(dvt_ref.dtype)


def _even_proj(x, ctx, modsel, w1, qg, wuq, kvg, wkv, tabs):
    b, n, d = x.shape
    c = ctx.shape[1]
    t = TOKEN_TILE
    nx, nc = n // t, c // t
    nt = n + c
    full = lambda a: pl.BlockSpec(a.shape, lambda bi, i: (0,) * a.ndim)
    tab_spec = pl.BlockSpec((t, LANE), lambda bi, i: (i, 0))
    tok = lambda w: pl.BlockSpec((1, t, w), lambda bi, i: (bi, i, 0))
    tok_t = lambda w: pl.BlockSpec((1, w, t), lambda bi, i: (bi, 0, i))
    bf = jnp.bfloat16
    return pl.pallas_call(
        functools.partial(_even_proj_kernel, nx),
        grid=(b, nx + nc),
        in_specs=[
            pl.BlockSpec((1, t, d), lambda bi, i: (bi, jnp.minimum(i, nx - 1), 0)),
            pl.BlockSpec((1, t, d), lambda bi, i: (bi, jnp.maximum(i - nx, 0), 0)),
            pl.BlockSpec((1, 1, 3, d), lambda bi, i: (bi, jnp.where(i >= nx, 1, 0), 0, 0)),
            full(w1), full(qg), full(wuq), full(kvg), full(wkv),
            tab_spec, tab_spec, tab_spec, tab_spec,
        ],
        out_specs=[
            tok_t(MLA_HEADS * LANE), tok(MLA_HEADS * LANE), tok_t(MLA_HEADS * MLA_V),
            tok_t(DIFF_HEADS * LANE), tok(DIFF_HEADS * LANE), tok_t(DIFF_HEADS * DIFF_V),
            tok(MLA_HEADS * MLA_V + DIFF_HEADS * DIFF_V),
        ],
        out_shape=[
            jax.ShapeDtypeStruct((b, MLA_HEADS * LANE, nt), bf),
            jax.ShapeDtypeStruct((b, nt, MLA_HEADS * LANE), bf),
            jax.ShapeDtypeStruct((b, MLA_HEADS * MLA_V, nt), bf),
            jax.ShapeDtypeStruct((b, DIFF_HEADS * LANE, nt), bf),
            jax.ShapeDtypeStruct((b, nt, DIFF_HEADS * LANE), bf),
            jax.ShapeDtypeStruct((b, DIFF_HEADS * DIFF_V, nt), bf),
            jax.ShapeDtypeStruct((b, nt, MLA_HEADS * MLA_V + DIFF_HEADS * DIFF_V), bf),
        ],
        compiler_params=_cparams("parallel", "parallel"),
        name="even_proj",
    )(x, ctx, modsel, w1, qg, wuq, kvg, wkv, *tabs)


def _kv_chunk(nk):
    if nk <= KV_CHUNK:
        return nk
    for m in range(KV_CHUNK // LANE, 0, -1):
        if nk % (m * LANE) == 0 and (nk // (m * LANE)) % 2 == 0:
            return m * LANE
    raise ValueError(f"no even chunking of {nk} keys")


def _attn_sweep(n_chunks, tk, tq, n_qt, q_of, k_ref, vt_ref, s_scr, finish):
    dv = vt_ref.shape[1]

    def scores(t, j, slot):
        k = k_ref[0, pl.ds(pl.multiple_of(j * tk, LANE), tk), :]
        cms = []
        for si, qt in enumerate(q_of(t)):
            s = jnp.dot(k, qt, preferred_element_type=jnp.float32)
            s_scr[slot, si] = s
            cms.append(jnp.max(s, axis=0, keepdims=True))
        return tuple(cms)

    n_ops = s_scr.shape[1]
    ones = jnp.ones((2 * SUBLANE, MXU_DEPTH), vt_ref.dtype)
    init = (jnp.full((1, tq), NEG_BIG, jnp.float32), jnp.zeros((dv + 2 * SUBLANE, tq), jnp.float32)) * n_ops
    blocks = [(r0, min(MXU_DEPTH, tk - r0)) for r0 in range(0, tk, MXU_DEPTH)]

    def step(nxt, j_cur, slot_cur, cms, state):
        off_cur = pl.multiple_of(j_cur * tk, LANE)
        if nxt is not None:
            t_nxt, j_nxt, slot_nxt = nxt
            off_nxt = pl.multiple_of(j_nxt * tk, LANE)
            q_nxt = q_of(t_nxt)
        m_new = [jnp.maximum(state[2 * si], cms[si]) for si in range(n_ops)]
        cm_nxt, pv = [None] * n_ops, [None] * n_ops
        for r0, r in blocks:
            if nxt is not None:
                k = k_ref[0, pl.ds(off_nxt + r0, r), :]
            vt = vt_ref[0, :, pl.ds(off_cur + r0, r)]
            vt_ext = jnp.concatenate([vt, ones[:, :r]], axis=0)
            for si in range(n_ops):
                if nxt is not None:
                    s = jnp.dot(k, q_nxt[si], preferred_element_type=jnp.float32)
                    s_scr[slot_nxt, si, r0:r0 + r] = s
                    cmax = jnp.max(s, axis=0, keepdims=True)
                    cm_nxt[si] = cmax if cm_nxt[si] is None else jnp.maximum(cm_nxt[si], cmax)
                p = jnp.exp2(s_scr[slot_cur, si, r0:r0 + r] - m_new[si]).astype(vt.dtype)
                d = jnp.dot(vt_ext, p, preferred_element_type=jnp.float32)
                pv[si] = d if pv[si] is None else pv[si] + d
        out = []
        for si in range(n_ops):
            alpha = jnp.exp2(state[2 * si] - m_new[si])
            out += [m_new[si], alpha * state[2 * si + 1] + pv[si]]
        return tuple(cm_nxt), tuple(out)

    def result(st):
        return [st[2 * si + 1][:dv] / st[2 * si + 1][dv:dv + 1] for si in range(n_ops)]

    if n_chunks == 1:
        def tile(t, carry):
            finish(t, result(step(None, 0, 0, scores(t, 0, 0), init)[1]))
            return carry
        lax.fori_loop(0, n_qt, tile, 0)
        return

    assert n_chunks % 2 == 0

    def tile(t, cm):
        def pair(i, carry):
            cm_a, st = carry
            cm_b, st = step((t, 2 * i + 1, 1), 2 * i, 0, cm_a, st)
            cm_a, st = step((t, 2 * i + 2, 0), 2 * i + 1, 1, cm_b, st)
            return cm_a, st

        cm_a, st = lax.fori_loop(0, n_chunks // 2 - 1, pair, (cm, init), unroll=True)
        cm_b, st = step((t, n_chunks - 1, 1), n_chunks - 2, 0, cm_a, st)
        cm_next, st = step((jnp.minimum(t + 1, n_qt - 1), 0, 0), n_chunks - 1, 1, cm_b, st)
        finish(t, result(st))
        return cm_next

    lax.fori_loop(0, n_qt, tile, scores(0, 0, 0), unroll=2)


def _q_cols(t, tq):
    return pl.ds(pl.multiple_of(t * tq, LANE), tq)


def _mla_attn_kernel(n_chunks, tk, tq, qt_ref, k_ref, vt_ref, o_ref, s_scr):
    n_ops = s_scr.shape[1]

    def q_of(t):
        return [qt_ref[0, :, _q_cols(t * n_ops + i, tq)] for i in range(n_ops)]

    def finish(t, outs):
        for i in range(n_ops):
            o_ref[0, :, _q_cols(t * n_ops + i, tq)] = outs[i].astype(o_ref.dtype)

    _attn_sweep(n_chunks, tk, tq, qt_ref.shape[2] // (tq * n_ops), q_of, k_ref, vt_ref, s_scr, finish)


def _attn_call(kernel_fn, name, heads, dv, n_ops, qt, k, vt, nq, nk, tail, extra=()):
    b, _, nt = vt.shape
    assert not tail or ((nt - nq) % nq == 0 and (nt - nk) % nk == 0)
    qb = (nt - nq) // nq if tail else 0
    kb = (nt - nk) // nk if tail else 0
    tq = min(Q_TILE, nq)
    tk = _kv_chunk(nk)
    return pl.pallas_call(
        functools.partial(kernel_fn, nk // tk, tk, tq),
        grid=(b, heads),
        in_specs=[
            pl.BlockSpec((1, LANE, nq), lambda bi, h: (bi, h, qb)),
            pl.BlockSpec((1, nk, LANE), lambda bi, h: (bi, kb, h)),
            pl.BlockSpec((1, dv, nk), lambda bi, h: (bi, h, kb)),
        ] + [pl.BlockSpec(a.shape, lambda bi, h: (0, 0)) for a in extra],
        out_specs=pl.BlockSpec((1, dv, nq), lambda bi, h: (bi, h, 0)),
        out_shape=jax.ShapeDtypeStruct((b, heads * dv, nq), jnp.bfloat16),
        scratch_shapes=[pltpu.VMEM((2, n_ops, tk, tq), jnp.float32)],
        compiler_params=_cparams("parallel", "parallel"),
        name=name,
    )(qt, k, vt, *extra)


def _mla_attn(qt, ka, vat, nq, nk, tail=False):
    n_ops = 2 if nq % (2 * Q_TILE) == 0 else 1
    return _attn_call(_mla_attn_kernel, "mla_attn", MLA_HEADS, MLA_V, n_ops, qt, ka, vat, nq, nk, tail)


def _diff_attn_kernel(n_chunks, tk, tq, qt_ref, k_ref, vt_ref, lam_ref, g_ref, o_ref, s_scr):
    row = lax.broadcasted_iota(jnp.int32, (LANE, 1), 0)

    def q_of(t):
        qt = qt_ref[0, :, _q_cols(t, tq)]
        zero = jnp.zeros_like(qt)
        return [jnp.where(row < DIFF_QK, qt, zero), jnp.where(row >= DIFF_QK, qt, zero)]

    def finish(t, outs):
        o = outs[0] - lam_ref[0:1, 0:1] * outs[1]
        o = o * lax.rsqrt(jnp.mean(o * o, axis=0, keepdims=True) + RMS_EPS) * g_ref[...]
        o_ref[0, :, _q_cols(t, tq)] = (o * (1.0 - LAMBDA_INIT_0)).astype(o_ref.dtype)

    _attn_sweep(n_chunks, tk, tq, qt_ref.shape[2] // tq, q_of, k_ref, vt_ref, s_scr, finish)


def _diff_attn(dqt, dk, dvt, lam, subln_col, nq, nk, tail=False):
    return _attn_call(_diff_attn_kernel, "diff_attn", DIFF_HEADS, DIFF_V, 2, dqt, dk, dvt, nq, nk, tail,
                      extra=(lam, subln_col))


def _residual_ln(xin, y, gate_vec, ln_g, ln_b):
    r = DEEPNORM_ALPHA * xin + gate_vec * y
    mu = jnp.mean(r, axis=-1, keepdims=True)
    rc = r - mu
    var = jnp.mean(rc * rc, axis=-1, keepdims=True)
    return rc * lax.rsqrt(var + LN_EPS) * ln_g + ln_b


def _even_merge_kernel(nx, x_ref, c_ref, mod_ref, oax_ref, oac_ref, obx_ref, obc_ref, gate_ref,
                       wout_ref, lng_ref, lnb_ref, mod1_ref, w1_ref, out_ref, u_ref, gate1_ref):
    i = pl.program_id(1)
    is_ctx = i >= nx
    xin = jnp.where(is_ctx, c_ref[0], x_ref[0])
    oa = jnp.where(is_ctx, oac_ref[0], oax_ref[0]).astype(jnp.float32).T
    ob = jnp.where(is_ctx, obc_ref[0], obx_ref[0]).astype(jnp.float32).T
    o = jnp.concatenate([oa, ob], axis=1)
    g = gate_ref[0].astype(jnp.float32)
    y = _bf16_dot(o * (g * _sigmoid(g)), wout_ref[...])
    xc = _residual_ln(xin, y, mod_ref[0, 0][2:3], lng_ref[...], lnb_ref[...])
    out_ref[0] = xc

    mod1 = mod1_ref[0, 0]
    z = _bf16_dot(xc * (1.0 + mod1[1:2]) + mod1[0:1], w1_ref[...])
    w = u_ref.shape[2]
    u_ref[0] = z[:, :w]
    gate1_ref[0] = z[:, w:]


def _even_merge(x, ctx, modsel, oax, oac, obx, obc, gate, wout, ln_g, ln_b, modsel1, w_in1):
    b, n, d = x.shape
    c = ctx.shape[1]
    t = TOKEN_TILE
    nx, nc = n // t, c // t
    wa, wb = oax.shape[1], obx.shape[1]
    w = w_in1.shape[1] // 2
    full = lambda a: pl.BlockSpec(a.shape, lambda bi, i: (0,) * a.ndim)
    xi = lambda bi, i: (bi, jnp.minimum(i, nx - 1), 0)
    ci = lambda bi, i: (bi, jnp.maximum(i - nx, 0), 0)
    xit = lambda bi, i: (bi, 0, jnp.minimum(i, nx - 1))
    cit = lambda bi, i: (bi, 0, jnp.maximum(i - nx, 0))
    mod_spec = pl.BlockSpec((1, 1, 3, d), lambda bi, i: (bi, jnp.where(i >= nx, 1, 0), 0, 0))
    tok = lambda width: pl.BlockSpec((1, t, width), lambda bi, i: (bi, i, 0))
    return pl.pallas_call(
        functools.partial(_even_merge_kernel, nx),
        grid=(b, nx + nc),
        in_specs=[
            pl.BlockSpec((1, t, d), xi),
            pl.BlockSpec((1, t, d), ci),
            mod_spec,
            pl.BlockSpec((1, wa, t), xit), pl.BlockSpec((1, wa, t), cit),
            pl.BlockSpec((1, wb, t), xit), pl.BlockSpec((1, wb, t), cit),
            tok(wa + wb),
            full(wout), full(ln_g), full(ln_b),
            mod_spec, full(w_in1),
        ],
        out_specs=[tok(d), tok(w), tok(w)],
        out_shape=[jax.ShapeDtypeStruct((b, n + c, d), jnp.float32),
                   jax.ShapeDtypeStruct((b, n + c, w), jnp.float32),
                   jax.ShapeDtypeStruct((b, n + c, w), jnp.float32)],
        compiler_params=_cparams("parallel", "parallel"),
        name="even_merge",
    )(x, ctx, modsel, oax, oac, obx, obc, gate, wout, ln_g, ln_b, modsel1, w_in1)


def _conv_tile(u, prev, nxt, cw, cb):
    t = u.shape[0]
    row = lax.broadcasted_iota(jnp.int32, (SUBLANE, 1), 0)
    r1, r2, rp = pltpu.roll(u, 1, 0), pltpu.roll(u, 2, 0), pltpu.roll(u, t - 1, 0)
    m1 = jnp.concatenate([jnp.where(row == 0, prev[7:8], r1[:SUBLANE]), r1[SUBLANE:]], 0)
    m2 = jnp.concatenate([jnp.where(row == 0, prev[6:7], jnp.where(row == 1, prev[7:8], r2[:SUBLANE])),
                          r2[SUBLANE:]], 0)
    p1 = jnp.concatenate([rp[:t - SUBLANE], jnp.where(row == SUBLANE - 1, nxt[0:1], rp[t - SUBLANE:])], 0)
    return cw[0:1] * m2 + cw[1:2] * m1 + cw[2:3] * u + cw[3:4] * p1 + cb


def _lru_coeffs(uc, wa_ref, ba, wx_ref, bx, lam, a_ref, b_ref):
    neg_log_a1 = LRU_C * jnp.log1p(jnp.exp(-lam))
    for k in range(LRU_BLOCKS):
        sl = slice(k * LANE, (k + 1) * LANE)
        ub = uc[:, sl]
        r = _sigmoid(_bf16_dot(ub, wa_ref[k]) + ba[:, sl])
        ig = _sigmoid(_bf16_dot(ub, wx_ref[k]) + bx[:, sl])
        a = jnp.exp2(r * (-LOG2E * neg_log_a1[:, sl]))
        a_ref[:, sl] = a
        y = jnp.tanh(r * neg_log_a1[:, sl]) * (a * a + 1.0)
        root = jnp.where(y > 0.0, y * lax.rsqrt(y), 0.0)
        b_ref[:, sl] = root * (ig * ub)


def _scan_rows(a_ref, b_ref, h_out_ref, h0, t, reverse):
    def step(s, h):
        r = (t - 1 - s) if reverse else s
        h = a_ref[pl.ds(r, 1), :] * h + b_ref[pl.ds(r, 1), :]
        h_out_ref[pl.ds(r, 1), :] = h
        return h

    return lax.fori_loop(0, t, step, h0, unroll=8)


def _lru_ctx_kernel(u_ref, cw_ref, cb_ref, wa_ref, ba_ref, wx_ref, bx_ref, lam_ref, h_ref,
                    a_s, b_s, hs):
    u = u_ref[0]
    zero8 = jnp.zeros((SUBLANE, u.shape[1]), jnp.float32)
    uc = _conv_tile(u, zero8, zero8, cw_ref[...], cb_ref[...])
    t = u.shape[0]
    for d in range(2):
        _lru_coeffs(uc, wa_ref.at[d], ba_ref[d], wx_ref.at[d], bx_ref[d], lam_ref[d], a_s, b_s)
        h = _scan_rows(a_s, b_s, hs, jnp.zeros((1, u.shape[1]), jnp.float32), t, reverse=(d == 1))
        h_ref[0, d:d + 1, :] = h


def _lru_ctx(u_all, n, c, cw, cb, wa, ba, wx, bx, lam):
    b, _, w = u_all.shape
    full = lambda a: pl.BlockSpec(a.shape, lambda bi: (0,) * a.ndim)
    return pl.pallas_call(
        _lru_ctx_kernel,
        grid=(b,),
        in_specs=[pl.BlockSpec((1, c, w), lambda bi: (bi, n // c, 0)),
                  full(cw), full(cb), full(wa), full(ba), full(wx), full(bx), full(lam)],
        out_specs=pl.BlockSpec((1, 2, w), lambda bi: (bi, 0, 0)),
        out_shape=jax.ShapeDtypeStruct((b, 2, w), jnp.float32),
        scratch_shapes=[pltpu.VMEM((c, w), jnp.float32)] * 3,
        compiler_params=_cparams("parallel"),
        name="lru_ctx",
    )(u_all, cw, cb, wa, ba, wx, bx, lam)


def _lru_fwd_kernel(n_tiles, u_ref, up_ref, un_ref, h0_ref, cw_ref, cb_ref, wa_ref, ba_ref, wx_ref, bx_ref,
                    lam_ref, hf_ref, uc_ref, a_s, b_s, carry):
    tile = pl.program_id(1)

    @pl.when(tile == 0)
    def _():
        carry[...] = h0_ref[0, 0:1, :]

    u = u_ref[0]
    prev = jnp.where(tile == 0, 0.0, up_ref[0])
    nxt = jnp.where(tile == n_tiles - 1, 0.0, un_ref[0])
    uc = _conv_tile(u, prev, nxt, cw_ref[...], cb_ref[...])
    uc_ref[0] = uc
    _lru_coeffs(uc, wa_ref.at[0], ba_ref[0], wx_ref.at[0], bx_ref[0], lam_ref[0], a_s, b_s)
    carry[...] = _scan_rows(a_s, b_s, hf_ref.at[0], carry[...], u.shape[0], reverse=False)


def _lru_bwd_kernel(uc_ref, h0_ref, wa_ref, ba_ref, wx_ref, bx_ref, lam_ref, hf_ref, gate_ref, x_ref, mod_ref,
                    wout_ref, lng_ref, lnb_ref, out_ref, a_s, b_s, hs, carry):
    @pl.when(pl.program_id(1) == 0)
    def _():
        carry[...] = h0_ref[0, 1:2, :]

    uc = uc_ref[0]
    _lru_coeffs(uc, wa_ref.at[1], ba_ref[1], wx_ref.at[1], bx_ref[1], lam_ref[1], a_s, b_s)
    carry[...] = _scan_rows(a_s, b_s, hs, carry[...], uc.shape[0], reverse=True)
    g = gate_ref[0]
    hx = hf_ref[0] + hs[...]
    y = _bf16_dot(hx * (g * _sigmoid(g)), wout_ref[...])
    out_ref[0] = _residual_ln(x_ref[0], y, mod_ref[0, 0][2:3], lng_ref[...], lnb_ref[...])


def _lru_fwd(u_all, n, h0, cw, cb, wa, ba, wx, bx, lam):
    b, _, w = u_all.shape
    t = min(SCAN_TILE, n)
    n_tiles = n // t
    r = t // SUBLANE
    full = lambda a: pl.BlockSpec(a.shape, lambda bi, s: (0,) * a.ndim)
    tile_spec = pl.BlockSpec((1, t, w), lambda bi, s: (bi, s, 0))
    return pl.pallas_call(
        functools.partial(_lru_fwd_kernel, n_tiles),
        grid=(b, n_tiles),
        in_specs=[
            tile_spec,
            pl.BlockSpec((1, SUBLANE, w), lambda bi, s: (bi, jnp.maximum(s * r - 1, 0), 0)),
            pl.BlockSpec((1, SUBLANE, w), lambda bi, s: (bi, (s + 1) * r, 0)),
            pl.BlockSpec((1, 2, w), lambda bi, s: (bi, 0, 0)),
            full(cw), full(cb), full(wa), full(ba), full(wx), full(bx), full(lam),
        ],
        out_specs=[tile_spec, tile_spec],
        out_shape=[jax.ShapeDtypeStruct((b, n, w), jnp.float32)] * 2,
        scratch_shapes=[pltpu.VMEM((t, w), jnp.float32), pltpu.VMEM((t, w), jnp.float32),
                        pltpu.VMEM((1, w), jnp.float32)],
        compiler_params=_cparams("parallel", "arbitrary"),
        name="lru_fwd",
    )(u_all, u_all, u_all, h0, cw, cb, wa, ba, wx, bx, lam)


def _lru_bwd_out(uc, h0, wa, ba, wx, bx, lam, hf, gate, xc, modsel, wout, ln_g, ln_b):
    b, n, w = uc.shape
    d = xc.shape[2]
    t = min(SCAN_TILE, n)
    n_tiles = n // t
    full = lambda a: pl.BlockSpec(a.shape, lambda bi, s: (0,) * a.ndim)
    tile_spec = lambda width: pl.BlockSpec((1, t, width), lambda bi, s: (bi, n_tiles - 1 - s, 0))
    return pl.pallas_call(
        _lru_bwd_kernel,
        grid=(b, n_tiles),
        in_specs=[
            tile_spec(w),
            pl.BlockSpec((1, 2, w), lambda bi, s: (bi, 0, 0)),
            full(wa), full(ba), full(wx), full(bx), full(lam),
            tile_spec(w), tile_spec(w), tile_spec(d),
            pl.BlockSpec((1, 1, 3, d), lambda bi, s: (bi, 0, 0, 0)),
            full(wout), full(ln_g), full(ln_b),
        ],
        out_specs=tile_spec(d),
        out_shape=jax.ShapeDtypeStruct((b, n, d), jnp.float32),
        scratch_shapes=[pltpu.VMEM((t, w), jnp.float32)] * 3 + [pltpu.VMEM((1, w), jnp.float32)],
        compiler_params=_cparams("parallel", "arbitrary"),
        name="lru_bwd_out",
    )(uc, h0, wa, ba, wx, bx, lam, hf, gate, xc, modsel, wout, ln_g, ln_b)


def _rope_tables(n, c):
    t = np.arange(n)
    rows = (t // GRID_W).astype(np.float32)
    cols = (t % GRID_W).astype(np.float32)

    def ang(rot_dim):
        n_freq = rot_dim // 4
        freqs = np.float32(ROPE_THETA) ** (-np.arange(n_freq, dtype=np.float32) / np.float32(n_freq))
        a = np.concatenate([rows[:, None] * freqs, cols[:, None] * freqs], -1)
        return np.cos(a.astype(np.float64)), np.sin(a.astype(np.float64))

    (cos_a, sin_a), (cos_b, sin_b) = ang(MLA_ROPE), ang(DIFF_QK)
    one = np.ones((n, MLA_NOPE))
    zero = np.zeros((n, MLA_NOPE))
    pad1 = np.ones((n, LANE - MLA_QK))
    pad0 = np.zeros((n, LANE - MLA_QK))
    ca = np.concatenate([one, cos_a, cos_a, pad1], 1)
    sa = np.concatenate([zero, -sin_a, sin_a, pad0], 1)
    cb = np.concatenate([cos_b, cos_b] * 2, 1)
    sb = np.concatenate([-sin_b, sin_b] * 2, 1)
    ident = lambda tab, v: jnp.asarray(np.concatenate([tab, np.full((c, LANE), v)], 0), jnp.float32)
    return ident(ca, 1.0), ident(sa, 0.0), ident(cb, 1.0), ident(sb, 0.0)


def _even_weights(w_in, w_uq, w_ukv):
    d = w_in.shape[0]
    bf = jnp.bfloat16
    o = [0, 384, 640, 672, 1184, 1696, 2208, 3232]
    cq, ckv, kr, dq, dk, dv, gate = (w_in[:, o[i]:o[i + 1]] for i in range(7))
    kr_group = jnp.concatenate([jnp.zeros((d, MLA_NOPE), w_in.dtype), kr,
                                jnp.zeros((d, LANE - MLA_QK), w_in.dtype)], 1)
    w1 = jnp.concatenate([cq, ckv, kr_group, dq, dk, dv, gate], 1).astype(bf)
    wuq = jnp.pad(w_uq.reshape(MLA_Q_LORA, MLA_HEADS, MLA_QK),
                  ((0, 0), (0, 0), (0, LANE - MLA_QK))).reshape(MLA_Q_LORA, MLA_HEADS * LANE).astype(bf)
    ukv = w_ukv.reshape(MLA_KV_LORA, MLA_HEADS, MLA_NOPE + MLA_V)
    wuk = jnp.pad(ukv[..., :MLA_NOPE], ((0, 0), (0, 0), (0, LANE - MLA_NOPE))).reshape(MLA_KV_LORA, MLA_HEADS * LANE)
    wuv = ukv[..., MLA_NOPE:].reshape(MLA_KV_LORA, MLA_HEADS * MLA_V)
    wkv = jnp.concatenate([wuk, wuv], 1).astype(bf)
    return w1, wuq, wkv


def _mod_select(mods_l, b):
    d = mods_l.shape[1] // 3
    mx = mods_l[:b].reshape(b, 3, d)
    mc = jnp.broadcast_to(mods_l[b].reshape(1, 3, d), (b, 3, d))
    return jnp.stack([mx, mc], axis=1)


def kernel(x, c, ctx, c_ctx, ada_w, ada_b, post_ln_g, post_ln_b, e_w_in, e_q_norm_g, e_w_uq, e_kv_norm_g, e_w_ukv, e_lam_q1, e_lam_k1, e_lam_q2, e_lam_k2, e_subln_g, e_w_out, o_w_in, o_conv_w, o_conv_b, o_gate_a_w, o_gate_a_b, o_gate_x_w, o_gate_x_b, o_lru_lambda, o_w_out):
    b, n, d = x.shape
    cl = ctx.shape[1]
    assert ada_w.shape[0] == DEPTH and b < SUBLANE
    assert n % Q_TILE == 0 and cl % TOKEN_TILE == 0 and n % cl == 0
    bf = jnp.bfloat16

    cond = jnp.concatenate([c, c_ctx[None], jnp.zeros((SUBLANE - b - 1, d), c.dtype)], 0)
    lamv = jnp.concatenate([e_lam_q1, e_lam_k1, e_lam_q2, e_lam_k2], 0)
    mods, lam = _adaln(cond, ada_w, ada_b, lamv)
    mod0, mod1 = _mod_select(mods[0], b), _mod_select(mods[1], b)

    w1, wuq, wkv = _even_weights(e_w_in[0], e_w_uq[0], e_w_ukv[0])
    tabs = _rope_tables(n, cl)
    qt, ka, vat, dqt, dk, dvt, gate = _even_proj(
        x, ctx, mod0, w1, e_q_norm_g[0][None], wuq, e_kv_norm_g[0][None], wkv, tabs)
    subln_col = e_subln_g[0][:, None]
    oax = _mla_attn(qt, ka, vat, n, n + cl)
    obx = _diff_attn(dqt, dk, dvt, lam, subln_col, n, n + cl)
    oac = _mla_attn(qt, ka, vat, cl, cl, tail=True)
    obc = _diff_attn(dqt, dk, dvt, lam, subln_col, cl, cl, tail=True)
    xc, u_all, gate1 = _even_merge(x, ctx, mod0, oax, oac, obx, obc, gate, e_w_out[0].astype(bf),
                                   post_ln_g[0][None], post_ln_b[0][None], mod1, o_w_in[0].astype(bf))

    cw, cb = o_conv_w[0], o_conv_b[0][None]
    wa, wx = o_gate_a_w[0].astype(bf), o_gate_x_w[0].astype(bf)
    w = u_all.shape[2]
    ba = o_gate_a_b[0].reshape(2, 1, w)
    bx = o_gate_x_b[0].reshape(2, 1, w)
    lru_lam = o_lru_lambda[0].reshape(2, 1, w)
    h0 = _lru_ctx(u_all, n, cl, cw, cb, wa, ba, wx, bx, lru_lam)
    hf, uc = _lru_fwd(u_all, n, h0, cw, cb, wa, ba, wx, bx, lru_lam)
    return _lru_bwd_out(uc, h0, wa, ba, wx, bx, lru_lam, hf, gate1, xc, mod1, o_w_out[0].astype(bf),
                        post_ln_g[1][None], post_ln_b[1][None])
```

```python
import functools
import math

import jax
import jax.numpy as jnp
import numpy as np
from jax import lax
from jax.experimental import pallas as pl
from jax.experimental.pallas import tpu as pltpu

GRID_W = 64
ROPE_THETA = 10000.0
LN_EPS = 1e-6
RMS_EPS = 1e-6

MLA_HEADS = 8
MLA_Q_LORA = 384
MLA_KV_LORA = 256
MLA_NOPE = 64
MLA_ROPE = 32
MLA_V = 64
MLA_QK = MLA_NOPE + MLA_ROPE
MLA_SCALE = MLA_QK ** -0.5

DIFF_HEADS = 4
DIFF_QK = 64
DIFF_V = 2 * DIFF_QK
DIFF_SCALE = DIFF_QK ** -0.5

LRU_BLOCKS = 8
LRU_C = 8.0

DEPTH = 2
DEEPNORM_ALPHA = (2 * DEPTH) ** 0.25
LAMBDA_INIT_0 = 0.8 - 0.6 * math.exp(-0.3 * 0)

LANE = 128
SUBLANE = 8
MXU_DEPTH = 256
LOG2E = 1.4426950408889634
NEG_BIG = -1e30

TOKEN_TILE = 256
Q_TILE = 512
KV_CHUNK = 1408
SCAN_TILE = 512
VMEM_LIMIT = 56 * 1024 * 1024

_G_CQ = (0, 384)
_G_CKV = (384, 640)
_G_KR = (640, 768)
_G_DQ = (768, 1280)
_G_DK = (1280, 1792)
_G_DV = (1792, 2304)
_G_GATE = (2304, 3328)


def _cparams(*sem):
    return pltpu.CompilerParams(dimension_semantics=sem, vmem_limit_bytes=VMEM_LIMIT)


def _bf16_dot(a, b):
    return jnp.dot(a.astype(jnp.bfloat16), b.astype(jnp.bfloat16), preferred_element_type=jnp.float32)


def _sigmoid(x):
    return 1.0 / (1.0 + jnp.exp(-x))


def _adaln_kernel(cond_ref, w_ref, b_ref, lamv_ref, mod_ref, lam_ref):
    cond = cond_ref[...]
    h = cond * _sigmoid(cond)
    mod_ref[0] = jnp.dot(h, w_ref[0], preferred_element_type=jnp.float32,
                         precision=lax.Precision.HIGHEST) + b_ref[0]
    lv = lamv_ref[...]
    d1 = jnp.sum(lv[0:1] * lv[1:2], axis=-1, keepdims=True)
    d2 = jnp.sum(lv[2:3] * lv[3:4], axis=-1, keepdims=True)
    lam = jnp.exp(d1) - jnp.exp(d2) + LAMBDA_INIT_0
    lam_ref[...] = jnp.broadcast_to(lam, lam_ref.shape)


def _adaln(cond, ada_w, ada_b, lamv):
    depth, d, d3 = ada_w.shape
    nj = d3 // d
    return pl.pallas_call(
        _adaln_kernel,
        grid=(depth, nj),
        in_specs=[
            pl.BlockSpec((SUBLANE, d), lambda l, j: (0, 0)),
            pl.BlockSpec((1, d, d), lambda l, j: (l, 0, j)),
            pl.BlockSpec((1, 1, d), lambda l, j: (l, 0, j)),
            pl.BlockSpec(lamv.shape, lambda l, j: (0, 0)),
        ],
        out_specs=[
            pl.BlockSpec((1, SUBLANE, d), lambda l, j: (l, 0, j)),
            pl.BlockSpec((SUBLANE, LANE), lambda l, j: (0, 0)),
        ],
        out_shape=[
            jax.ShapeDtypeStruct((depth, SUBLANE, d3), jnp.float32),
            jax.ShapeDtypeStruct((SUBLANE, LANE), jnp.float32),
        ],
        compiler_params=_cparams("arbitrary", "arbitrary"),
        name="adaln",
    )(cond, ada_w, ada_b.reshape(depth, 1, d3), lamv)


def _rms(x, g):
    return x * lax.rsqrt(jnp.mean(x * x, axis=-1, keepdims=True) + RMS_EPS) * g


def _rope_group(x, cos, sin, half, first_half_mask):
    partner = jnp.where(first_half_mask, pltpu.roll(x, LANE - half, 1), pltpu.roll(x, half, 1))
    return x * cos + partner * sin


def _even_proj_kernel(nx, x_ref, c_ref, mod_ref, w1_ref, qg_ref, wuq_ref, kvg_ref, wkv_ref,
                      ca_ref, sa_ref, cb_ref, sb_ref,
                      qt_ref, ka_ref, vat_ref, dqt_ref, dk_ref, dvt_ref, gate_ref):
    i = pl.program_id(1)
    xin = jnp.where(i >= nx, c_ref[0], x_ref[0])
    mod = mod_ref[0, 0]
    xm = (xin * (1.0 + mod[1:2]) + mod[0:1]).astype(w1_ref.dtype)

    def zcols(lo, hi):
        return jnp.dot(xm, w1_ref[:, lo:hi], preferred_element_type=jnp.float32)

    lane = lax.broadcasted_iota(jnp.int32, (1, LANE), 1)
    mla_first = jnp.logical_and(lane >= MLA_NOPE, lane < MLA_NOPE + MLA_ROPE // 2)
    diff_first = (lane % DIFF_QK) < DIFF_QK // 2
    ca, sa, cb, sb = ca_ref[...], sa_ref[...], cb_ref[...], sb_ref[...]

    z_lat = zcols(_G_CQ[0], _G_KR[1])
    z_dq = zcols(*_G_DQ)
    z_dk = zcols(*_G_DK)

    cqn = _rms(z_lat[:, _G_CQ[0]:_G_CQ[1]], qg_ref[...])
    q = _bf16_dot(cqn, wuq_ref[...])
    ckvn = _rms(z_lat[:, _G_CKV[0]:_G_CKV[1]], kvg_ref[...])
    kv = _bf16_dot(ckvn, wkv_ref[...])
    z_dv = zcols(*_G_DV)
    gate_ref[0] = zcols(*_G_GATE)

    q_heads = []
    for h in range(MLA_HEADS):
        qh = _rope_group(q[:, h * LANE:(h + 1) * LANE], ca, sa, MLA_ROPE // 2, mla_first)
        q_heads.append(qh * (MLA_SCALE * LOG2E))
    qt_ref[0] = jnp.concatenate(q_heads, axis=1).T.astype(qt_ref.dtype)

    kr = _rope_group(z_lat[:, _G_KR[0]:_G_KR[1]], ca, sa, MLA_ROPE // 2, mla_first)
    for h in range(MLA_HEADS):
        ka_ref[0, :, h * LANE:(h + 1) * LANE] = (kv[:, h * LANE:(h + 1) * LANE] + kr).astype(ka_ref.dtype)
    vat_ref[0] = kv[:, MLA_HEADS * LANE:].T.astype(vat_ref.dtype)

    dq_heads = []
    for h in range(DIFF_HEADS):
        sl = slice(h * LANE, (h + 1) * LANE)
        dq_heads.append(_rope_group(z_dq[:, sl], cb, sb, DIFF_QK // 2, diff_first) * (DIFF_SCALE * LOG2E))
        dk_ref[0, :, sl] = _rope_group(z_dk[:, sl], cb, sb, DIFF_QK // 2, diff_first).astype(dk_ref.dtype)
    dqt_ref[0] = jnp.concatenate(dq_heads, axis=1).T.astype(dqt_ref.dtype)
    dvt_ref[0] = z_dv.T.astype(dvt_ref.dtype)


def _even_proj(x, ctx, modsel, w1, qg, wuq, kvg, wkv, tabs):
    b, n, d = x.shape
    c = ctx.shape[1]
    t = TOKEN_TILE
    nx, nc = n // t, c // t
    nt = n + c
    full = lambda a: pl.BlockSpec(a.shape, lambda bi, i: (0,) * a.ndim)
    tab_spec = pl.BlockSpec((t, LANE), lambda bi, i: (i, 0))
    tok = lambda w: pl.BlockSpec((1, t, w), lambda bi, i: (bi, i, 0))
    tok_t = lambda w: pl.BlockSpec((1, w, t), lambda bi, i: (bi, 0, i))
    bf = jnp.bfloat16
    return pl.pallas_call(
        functools.partial(_even_proj_kernel, nx),
        grid=(b, nx + nc),
        in_specs=[
            pl.BlockSpec((1, t, d), lambda bi, i: (bi, jnp.minimum(i, nx - 1), 0)),
            pl.BlockSpec((1, t, d), lambda bi, i: (bi, jnp.maximum(i - nx, 0), 0)),
            pl.BlockSpec((1, 1, 3, d), lambda bi, i: (bi, jnp.where(i >= nx, 1, 0), 0, 0)),
            full(w1), full(qg), full(wuq), full(kvg), full(wkv),
            tab_spec, tab_spec, tab_spec, tab_spec,
        ],
        out_specs=[
            tok_t(MLA_HEADS * LANE), tok(MLA_HEADS * LANE), tok_t(MLA_HEADS * MLA_V),
            tok_t(DIFF_HEADS * LANE), tok(DIFF_HEADS * LANE), tok_t(DIFF_HEADS * DIFF_V),
            tok(MLA_HEADS * MLA_V + DIFF_HEADS * DIFF_V),
        ],
        out_shape=[
            jax.ShapeDtypeStruct((b, MLA_HEADS * LANE, nt), bf),
            jax.ShapeDtypeStruct((b, nt, MLA_HEADS * LANE), bf),
            jax.ShapeDtypeStruct((b, MLA_HEADS * MLA_V, nt), bf),
            jax.ShapeDtypeStruct((b, DIFF_HEADS * LANE, nt), bf),
            jax.ShapeDtypeStruct((b, nt, DIFF_HEADS * LANE), bf),
            jax.ShapeDtypeStruct((b, DIFF_HEADS * DIFF_V, nt), bf),
            jax.ShapeDtypeStruct((b, nt, MLA_HEADS * MLA_V + DIFF_HEADS * DIFF_V), jnp.float32),
        ],
        compiler_params=_cparams("parallel", "parallel"),
        name="even_proj",
    )(x, ctx, modsel, w1, qg, wuq, kvg, wkv, *tabs)


def _kv_chunk(nk):
    if nk <= KV_CHUNK:
        return nk
    for m in range(KV_CHUNK // LANE, 0, -1):
        if nk % (m * LANE) == 0 and (nk // (m * LANE)) % 2 == 0:
            return m * LANE
    raise ValueError(f"no even chunking of {nk} keys")


def _attn_sweep(n_chunks, tk, tq, n_qt, q_of, k_ref, vt_ref, s_scr, finish):
    dv = vt_ref.shape[1]

    def scores(t, j, slot):
        k = k_ref[0, pl.ds(pl.multiple_of(j * tk, LANE), tk), :]
        cms = []
        for si, qt in enumerate(q_of(t)):
            s = jnp.dot(k, qt, preferred_element_type=jnp.float32)
            s_scr[slot, si] = s
            cms.append(jnp.max(s, axis=0, keepdims=True))
        return tuple(cms)

    n_ops = s_scr.shape[1]
    ones = jnp.ones((2 * SUBLANE, MXU_DEPTH), vt_ref.dtype)
    init = (jnp.full((1, tq), NEG_BIG, jnp.float32), jnp.zeros((dv + 2 * SUBLANE, tq), jnp.float32)) * n_ops
    blocks = [(r0, min(MXU_DEPTH, tk - r0)) for r0 in range(0, tk, MXU_DEPTH)]

    def step(nxt, j_cur, slot_cur, cms, state):
        off_cur = pl.multiple_of(j_cur * tk, LANE)
        if nxt is not None:
            t_nxt, j_nxt, slot_nxt = nxt
            off_nxt = pl.multiple_of(j_nxt * tk, LANE)
            q_nxt = q_of(t_nxt)
        m_new = [jnp.maximum(state[2 * si], cms[si]) for si in range(n_ops)]
        cm_nxt, pv = [None] * n_ops, [None] * n_ops
        for r0, r in blocks:
            if nxt is not None:
                k = k_ref[0, pl.ds(off_nxt + r0, r), :]
            vt = vt_ref[0, :, pl.ds(off_cur + r0, r)]
            vt_ext = jnp.concatenate([vt, ones[:, :r]], axis=0)
            for si in range(n_ops):
                if nxt is not None:
                    s = jnp.dot(k, q_nxt[si], preferred_element_type=jnp.float32)
                    s_scr[slot_nxt, si, r0:r0 + r] = s
                    cmax = jnp.max(s, axis=0, keepdims=True)
                    cm_nxt[si] = cmax if cm_nxt[si] is None else jnp.maximum(cm_nxt[si], cmax)
                p = jnp.exp2(s_scr[slot_cur, si, r0:r0 + r] - m_new[si]).astype(vt.dtype)
                d = jnp.dot(vt_ext, p, preferred_element_type=jnp.float32)
                pv[si] = d if pv[si] is None else pv[si] + d
        out = []
        for si in range(n_ops):
            alpha = jnp.exp2(state[2 * si] - m_new[si])
            out += [m_new[si], alpha * state[2 * si + 1] + pv[si]]
        return tuple(cm_nxt), tuple(out)

    def result(st):
        return [st[2 * si + 1][:dv] / st[2 * si + 1][dv:dv + 1] for si in range(n_ops)]

    if n_chunks == 1:
        def tile(t, carry):
            finish(t, result(step(None, 0, 0, scores(t, 0, 0), init)[1]))
            return carry
        lax.fori_loop(0, n_qt, tile, 0)
        return

    assert n_chunks % 2 == 0

    def tile(t, cm):
        def pair(i, carry):
            cm_a, st = carry
            cm_b, st = step((t, 2 * i + 1, 1), 2 * i, 0, cm_a, st)
            cm_a, st = step((t, 2 * i + 2, 0), 2 * i + 1, 1, cm_b, st)
            return cm_a, st

        cm_a, st = lax.fori_loop(0, n_chunks // 2 - 1, pair, (cm, init), unroll=True)
        cm_b, st = step((t, n_chunks - 1, 1), n_chunks - 2, 0, cm_a, st)
        cm_next, st = step((jnp.minimum(t + 1, n_qt - 1), 0, 0), n_chunks - 1, 1, cm_b, st)
        finish(t, result(st))
        return cm_next

    lax.fori_loop(0, n_qt, tile, scores(0, 0, 0), unroll=2)


def _q_cols(t, tq):
    return pl.ds(pl.multiple_of(t * tq, LANE), tq)


def _mla_attn_kernel(n_chunks, tk, tq, qt_ref, k_ref, vt_ref, o_ref, s_scr):
    n_ops = s_scr.shape[1]

    def q_of(t):
        return [qt_ref[0, :, _q_cols(t * n_ops + i, tq)] for i in range(n_ops)]

    def finish(t, outs):
        for i in range(n_ops):
            o_ref[0, :, _q_cols(t * n_ops + i, tq)] = outs[i]

    _attn_sweep(n_chunks, tk, tq, qt_ref.shape[2] // (tq * n_ops), q_of, k_ref, vt_ref, s_scr, finish)


def _attn_call(kernel_fn, name, heads, dv, n_ops, qt, k, vt, nq, nk, tail, extra=()):
    b, _, nt = vt.shape
    assert not tail or ((nt - nq) % nq == 0 and (nt - nk) % nk == 0)
    qb = (nt - nq) // nq if tail else 0
    kb = (nt - nk) // nk if tail else 0
    tq = min(Q_TILE, nq)
    tk = _kv_chunk(nk)
    return pl.pallas_call(
        functools.partial(kernel_fn, nk // tk, tk, tq),
        grid=(b, heads),
        in_specs=[
            pl.BlockSpec((1, LANE, nq), lambda bi, h: (bi, h, qb)),
            pl.BlockSpec((1, nk, LANE), lambda bi, h: (bi, kb, h)),
            pl.BlockSpec((1, dv, nk), lambda bi, h: (bi, h, kb)),
        ] + [pl.BlockSpec(a.shape, lambda bi, h: (0, 0)) for a in extra],
        out_specs=pl.BlockSpec((1, dv, nq), lambda bi, h: (bi, h, 0)),
        out_shape=jax.ShapeDtypeStruct((b, heads * dv, nq), jnp.float32),
        scratch_shapes=[pltpu.VMEM((2, n_ops, tk, tq), jnp.float32)],
        compiler_params=_cparams("parallel", "parallel"),
        name=name,
    )(qt, k, vt, *extra)


def _mla_attn(qt, ka, vat, nq, nk, tail=False):
    n_ops = 2 if nq % (2 * Q_TILE) == 0 else 1
    return _attn_call(_mla_attn_kernel, "mla_attn", MLA_HEADS, MLA_V, n_ops, qt, ka, vat, nq, nk, tail)


def _diff_attn_kernel(n_chunks, tk, tq, qt_ref, k_ref, vt_ref, lam_ref, g_ref, o_ref, s_scr):
    row = lax.broadcasted_iota(jnp.int32, (LANE, 1), 0)

    def q_of(t):
        qt = qt_ref[0, :, _q_cols(t, tq)]
        zero = jnp.zeros_like(qt)
        return [jnp.where(row < DIFF_QK, qt, zero), jnp.where(row >= DIFF_QK, qt, zero)]

    def finish(t, outs):
        o = outs[0] - lam_ref[0:1, 0:1] * outs[1]
        o = o * lax.rsqrt(jnp.mean(o * o, axis=0, keepdims=True) + RMS_EPS) * g_ref[...]
        o_ref[0, :, _q_cols(t, tq)] = o * (1.0 - LAMBDA_INIT_0)

    _attn_sweep(n_chunks, tk, tq, qt_ref.shape[2] // tq, q_of, k_ref, vt_ref, s_scr, finish)


def _diff_attn(dqt, dk, dvt, lam, subln_col, nq, nk, tail=False):
    return _attn_call(_diff_attn_kernel, "diff_attn", DIFF_HEADS, DIFF_V, 2, dqt, dk, dvt, nq, nk, tail,
                      extra=(lam, subln_col))


def _residual_ln(xin, y, gate_vec, ln_g, ln_b):
    r = DEEPNORM_ALPHA * xin + gate_vec * y
    mu = jnp.mean(r, axis=-1, keepdims=True)
    rc = r - mu
    var = jnp.mean(rc * rc, axis=-1, keepdims=True)
    return rc * lax.rsqrt(var + LN_EPS) * ln_g + ln_b


def _even_merge_kernel(nx, x_ref, c_ref, mod_ref, oax_ref, oac_ref, obx_ref, obc_ref, gate_ref,
                       wout_ref, lng_ref, lnb_ref, mod1_ref, w1_ref, out_ref, u_ref, gate1_ref):
    i = pl.program_id(1)
    is_ctx = i >= nx
    xin = jnp.where(is_ctx, c_ref[0], x_ref[0])
    oa = jnp.where(is_ctx, oac_ref[0], oax_ref[0]).T
    ob = jnp.where(is_ctx, obc_ref[0], obx_ref[0]).T
    o = jnp.concatenate([oa, ob], axis=1)
    g = gate_ref[0]
    y = _bf16_dot(o * (g * _sigmoid(g)), wout_ref[...])
    xc = _residual_ln(xin, y, mod_ref[0, 0][2:3], lng_ref[...], lnb_ref[...])
    out_ref[0] = xc

    mod1 = mod1_ref[0, 0]
    z = _bf16_dot(xc * (1.0 + mod1[1:2]) + mod1[0:1], w1_ref[...])
    w = u_ref.shape[2]
    u_ref[0] = z[:, :w]
    gate1_ref[0] = z[:, w:]


def _even_merge(x, ctx, modsel, oax, oac, obx, obc, gate, wout, ln_g, ln_b, modsel1, w_in1):
    b, n, d = x.shape
    c = ctx.shape[1]
    t = TOKEN_TILE
    nx, nc = n // t, c // t
    wa, wb = oax.shape[1], obx.shape[1]
    w = w_in1.shape[1] // 2
    full = lambda a: pl.BlockSpec(a.shape, lambda bi, i: (0,) * a.ndim)
    xi = lambda bi, i: (bi, jnp.minimum(i, nx - 1), 0)
    ci = lambda bi, i: (bi, jnp.maximum(i - nx, 0), 0)
    xit = lambda bi, i: (bi, 0, jnp.minimum(i, nx - 1))
    cit = lambda bi, i: (bi, 0, jnp.maximum(i - nx, 0))
    mod_spec = pl.BlockSpec((1, 1, 3, d), lambda bi, i: (bi, jnp.where(i >= nx, 1, 0), 0, 0))
    tok = lambda width: pl.BlockSpec((1, t, width), lambda bi, i: (bi, i, 0))
    return pl.pallas_call(
        functools.partial(_even_merge_kernel, nx),
        grid=(b, nx + nc),
        in_specs=[
            pl.BlockSpec((1, t, d), xi),
            pl.BlockSpec((1, t, d), ci),
            mod_spec,
            pl.BlockSpec((1, wa, t), xit), pl.BlockSpec((1, wa, t), cit),
            pl.BlockSpec((1, wb, t), xit), pl.BlockSpec((1, wb, t), cit),
            tok(wa + wb),
            full(wout), full(ln_g), full(ln_b),
            mod_spec, full(w_in1),
        ],
        out_specs=[tok(d), tok(w), tok(w)],
        out_shape=[jax.ShapeDtypeStruct((b, n + c, d), jnp.float32),
                   jax.ShapeDtypeStruct((b, n + c, w), jnp.float32),
                   jax.ShapeDtypeStruct((b, n + c, w), jnp.float32)],
        compiler_params=_cparams("parallel", "parallel"),
        name="even_merge",
    )(x, ctx, modsel, oax, oac, obx, obc, gate, wout, ln_g, ln_b, modsel1, w_in1)


def _conv_tile(u, prev, nxt, cw, cb):
    t = u.shape[0]
    row = lax.broadcasted_iota(jnp.int32, (SUBLANE, 1), 0)
    r1, r2, rp = pltpu.roll(u, 1, 0), pltpu.roll(u, 2, 0), pltpu.roll(u, t - 1, 0)
    m1 = jnp.concatenate([jnp.where(row == 0, prev[7:8], r1[:SUBLANE]), r1[SUBLANE:]], 0)
    m2 = jnp.concatenate([jnp.where(row == 0, prev[6:7], jnp.where(row == 1, prev[7:8], r2[:SUBLANE])),
                          r2[SUBLANE:]], 0)
    p1 = jnp.concatenate([rp[:t - SUBLANE], jnp.where(row == SUBLANE - 1, nxt[0:1], rp[t - SUBLANE:])], 0)
    return cw[0:1] * m2 + cw[1:2] * m1 + cw[2:3] * u + cw[3:4] * p1 + cb


def _lru_coeffs(uc, wa_ref, ba, wx_ref, bx, lam, a_ref, b_ref):
    neg_log_a1 = LRU_C * jnp.log1p(jnp.exp(-lam))
    for k in range(LRU_BLOCKS):
        sl = slice(k * LANE, (k + 1) * LANE)
        ub = uc[:, sl]
        r = _sigmoid(_bf16_dot(ub, wa_ref[k]) + ba[:, sl])
        ig = _sigmoid(_bf16_dot(ub, wx_ref[k]) + bx[:, sl])
        a = jnp.exp2(r * (-LOG2E * neg_log_a1[:, sl]))
        a_ref[:, sl] = a
        y = jnp.tanh(r * neg_log_a1[:, sl]) * (a * a + 1.0)
        root = jnp.where(y > 0.0, y * lax.rsqrt(y), 0.0)
        b_ref[:, sl] = root * (ig * ub)


def _scan_rows(a_ref, b_ref, h_out_ref, h0, t, reverse):
    def step(s, h):
        r = (t - 1 - s) if reverse else s
        h = a_ref[pl.ds(r, 1), :] * h + b_ref[pl.ds(r, 1), :]
        h_out_ref[pl.ds(r, 1), :] = h
        return h

    return lax.fori_loop(0, t, step, h0, unroll=8)


def _lru_ctx_kernel(u_ref, cw_ref, cb_ref, wa_ref, ba_ref, wx_ref, bx_ref, lam_ref, h_ref,
                    a_s, b_s, hs):
    u = u_ref[0]
    zero8 = jnp.zeros((SUBLANE, u.shape[1]), jnp.float32)
    uc = _conv_tile(u, zero8, zero8, cw_ref[...], cb_ref[...])
    t = u.shape[0]
    for d in range(2):
        _lru_coeffs(uc, wa_ref.at[d], ba_ref[d], wx_ref.at[d], bx_ref[d], lam_ref[d], a_s, b_s)
        h = _scan_rows(a_s, b_s, hs, jnp.zeros((1, u.shape[1]), jnp.float32), t, reverse=(d == 1))
        h_ref[0, d:d + 1, :] = h


def _lru_ctx(u_all, n, c, cw, cb, wa, ba, wx, bx, lam):
    b, _, w = u_all.shape
    full = lambda a: pl.BlockSpec(a.shape, lambda bi: (0,) * a.ndim)
    return pl.pallas_call(
        _lru_ctx_kernel,
        grid=(b,),
        in_specs=[pl.BlockSpec((1, c, w), lambda bi: (bi, n // c, 0)),
                  full(cw), full(cb), full(wa), full(ba), full(wx), full(bx), full(lam)],
        out_specs=pl.BlockSpec((1, 2, w), lambda bi: (bi, 0, 0)),
        out_shape=jax.ShapeDtypeStruct((b, 2, w), jnp.float32),
        scratch_shapes=[pltpu.VMEM((c, w), jnp.float32)] * 3,
        compiler_params=_cparams("parallel"),
        name="lru_ctx",
    )(u_all, cw, cb, wa, ba, wx, bx, lam)


def _lru_fwd_kernel(n_tiles, u_ref, up_ref, un_ref, h0_ref, cw_ref, cb_ref, wa_ref, ba_ref, wx_ref, bx_ref,
                    lam_ref, hf_ref, uc_ref, a_s, b_s, carry):
    tile = pl.program_id(1)

    @pl.when(tile == 0)
    def _():
        carry[...] = h0_ref[0, 0:1, :]

    u = u_ref[0]
    prev = jnp.where(tile == 0, 0.0, up_ref[0])
    nxt = jnp.where(tile == n_tiles - 1, 0.0, un_ref[0])
    uc = _conv_tile(u, prev, nxt, cw_ref[...], cb_ref[...])
    uc_ref[0] = uc
    _lru_coeffs(uc, wa_ref.at[0], ba_ref[0], wx_ref.at[0], bx_ref[0], lam_ref[0], a_s, b_s)
    carry[...] = _scan_rows(a_s, b_s, hf_ref.at[0], carry[...], u.shape[0], reverse=False)


def _lru_bwd_kernel(uc_ref, h0_ref, wa_ref, ba_ref, wx_ref, bx_ref, lam_ref, hf_ref, gate_ref, x_ref, mod_ref,
                    wout_ref, lng_ref, lnb_ref, out_ref, a_s, b_s, hs, carry):
    @pl.when(pl.program_id(1) == 0)
    def _():
        carry[...] = h0_ref[0, 1:2, :]

    uc = uc_ref[0]
    _lru_coeffs(uc, wa_ref.at[1], ba_ref[1], wx_ref.at[1], bx_ref[1], lam_ref[1], a_s, b_s)
    carry[...] = _scan_rows(a_s, b_s, hs, carry[...], uc.shape[0], reverse=True)
    g = gate_ref[0]
    hx = hf_ref[0] + hs[...]
    y = _bf16_dot(hx * (g * _sigmoid(g)), wout_ref[...])
    out_ref[0] = _residual_ln(x_ref[0], y, mod_ref[0, 0][2:3], lng_ref[...], lnb_ref[...])


def _lru_fwd(u_all, n, h0, cw, cb, wa, ba, wx, bx, lam):
    b, _, w = u_all.shape
    t = min(SCAN_TILE, n)
    n_tiles = n // t
    r = t // SUBLANE
    full = lambda a: pl.BlockSpec(a.shape, lambda bi, s: (0,) * a.ndim)
    tile_spec = pl.BlockSpec((1, t, w), lambda bi, s: (bi, s, 0))
    return pl.pallas_call(
        functools.partial(_lru_fwd_kernel, n_tiles),
        grid=(b, n_tiles),
        in_specs=[
            tile_spec,
            pl.BlockSpec((1, SUBLANE, w), lambda bi, s: (bi, jnp.maximum(s * r - 1, 0), 0)),
            pl.BlockSpec((1, SUBLANE, w), lambda bi, s: (bi, (s + 1) * r, 0)),
            pl.BlockSpec((1, 2, w), lambda bi, s: (bi, 0, 0)),
            full(cw), full(cb), full(wa), full(ba), full(wx), full(bx), full(lam),
        ],
        out_specs=[tile_spec, tile_spec],
        out_shape=[jax.ShapeDtypeStruct((b, n, w), jnp.float32)] * 2,
        scratch_shapes=[pltpu.VMEM((t, w), jnp.float32), pltpu.VMEM((t, w), jnp.float32),
                        pltpu.VMEM((1, w), jnp.float32)],
        compiler_params=_cparams("parallel", "arbitrary"),
        name="lru_fwd",
    )(u_all, u_all, u_all, h0, cw, cb, wa, ba, wx, bx, lam)


def _lru_bwd_out(uc, h0, wa, ba, wx, bx, lam, hf, gate, xc, modsel, wout, ln_g, ln_b):
    b, n, w = uc.shape
    d = xc.shape[2]
    t = min(SCAN_TILE, n)
    n_tiles = n // t
    full = lambda a: pl.BlockSpec(a.shape, lambda bi, s: (0,) * a.ndim)
    tile_spec = lambda width: pl.BlockSpec((1, t, width), lambda bi, s: (bi, n_tiles - 1 - s, 0))
    return pl.pallas_call(
        _lru_bwd_kernel,
        grid=(b, n_tiles),
        in_specs=[
            tile_spec(w),
            pl.BlockSpec((1, 2, w), lambda bi, s: (bi, 0, 0)),
            full(wa), full(ba), full(wx), full(bx), full(lam),
            tile_spec(w), tile_spec(w), tile_spec(d),
            pl.BlockSpec((1, 1, 3, d), lambda bi, s: (bi, 0, 0, 0)),
            full(wout), full(ln_g), full(ln_b),
        ],
        out_specs=tile_spec(d),
        out_shape=jax.ShapeDtypeStruct((b, n, d), jnp.float32),
        scratch_shapes=[pltpu.VMEM((t, w), jnp.float32)] * 3 + [pltpu.VMEM((1, w), jnp.float32)],
        compiler_params=_cparams("parallel", "arbitrary"),
        name="lru_bwd_out",
    )(uc, h0, wa, ba, wx, bx, lam, hf, gate, xc, modsel, wout, ln_g, ln_b)


def _rope_tables(n, c):
    t = np.arange(n)
    rows = (t // GRID_W).astype(np.float32)
    cols = (t % GRID_W).astype(np.float32)

    def ang(rot_dim):
        n_freq = rot_dim // 4
        freqs = np.float32(ROPE_THETA) ** (-np.arange(n_freq, dtype=np.float32) / np.float32(n_freq))
        a = np.concatenate([rows[:, None] * freqs, cols[:, None] * freqs], -1)
        return np.cos(a.astype(np.float64)), np.sin(a.astype(np.float64))

    (cos_a, sin_a), (cos_b, sin_b) = ang(MLA_ROPE), ang(DIFF_QK)
    one = np.ones((n, MLA_NOPE))
    zero = np.zeros((n, MLA_NOPE))
    pad1 = np.ones((n, LANE - MLA_QK))
    pad0 = np.zeros((n, LANE - MLA_QK))
    ca = np.concatenate([one, cos_a, cos_a, pad1], 1)
    sa = np.concatenate([zero, -sin_a, sin_a, pad0], 1)
    cb = np.concatenate([cos_b, cos_b] * 2, 1)
    sb = np.concatenate([-sin_b, sin_b] * 2, 1)
    ident = lambda tab, v: jnp.asarray(np.concatenate([tab, np.full((c, LANE), v)], 0), jnp.float32)
    return ident(ca, 1.0), ident(sa, 0.0), ident(cb, 1.0), ident(sb, 0.0)


def _even_weights(w_in, w_uq, w_ukv):
    d = w_in.shape[0]
    bf = jnp.bfloat16
    o = [0, 384, 640, 672, 1184, 1696, 2208, 3232]
    cq, ckv, kr, dq, dk, dv, gate = (w_in[:, o[i]:o[i + 1]] for i in range(7))
    kr_group = jnp.concatenate([jnp.zeros((d, MLA_NOPE), w_in.dtype), kr,
                                jnp.zeros((d, LANE - MLA_QK), w_in.dtype)], 1)
    w1 = jnp.concatenate([cq, ckv, kr_group, dq, dk, dv, gate], 1).astype(bf)
    wuq = jnp.pad(w_uq.reshape(MLA_Q_LORA, MLA_HEADS, MLA_QK),
                  ((0, 0), (0, 0), (0, LANE - MLA_QK))).reshape(MLA_Q_LORA, MLA_HEADS * LANE).astype(bf)
    ukv = w_ukv.reshape(MLA_KV_LORA, MLA_HEADS, MLA_NOPE + MLA_V)
    wuk = jnp.pad(ukv[..., :MLA_NOPE], ((0, 0), (0, 0), (0, LANE - MLA_NOPE))).reshape(MLA_KV_LORA, MLA_HEADS * LANE)
    wuv = ukv[..., MLA_NOPE:].reshape(MLA_KV_LORA, MLA_HEADS * MLA_V)
    wkv = jnp.concatenate([wuk, wuv], 1).astype(bf)
    return w1, wuq, wkv


def _mod_select(mods_l, b):
    d = mods_l.shape[1] // 3
    mx = mods_l[:b].reshape(b, 3, d)
    mc = jnp.broadcast_to(mods_l[b].reshape(1, 3, d), (b, 3, d))
    return jnp.stack([mx, mc], axis=1)


def kernel(x, c, ctx, c_ctx, ada_w, ada_b, post_ln_g, post_ln_b, e_w_in, e_q_norm_g, e_w_uq, e_kv_norm_g, e_w_ukv, e_lam_q1, e_lam_k1, e_lam_q2, e_lam_k2, e_subln_g, e_w_out, o_w_in, o_conv_w, o_conv_b, o_gate_a_w, o_gate_a_b, o_gate_x_w, o_gate_x_b, o_lru_lambda, o_w_out):
    b, n, d = x.shape
    cl = ctx.shape[1]
    assert ada_w.shape[0] == DEPTH and b < SUBLANE
    assert n % Q_TILE == 0 and cl % TOKEN_TILE == 0 and n % cl == 0
    bf = jnp.bfloat16

    cond = jnp.concatenate([c, c_ctx[None], jnp.zeros((SUBLANE - b - 1, d), c.dtype)], 0)
    lamv = jnp.concatenate([e_lam_q1, e_lam_k1, e_lam_q2, e_lam_k2], 0)
    mods, lam = _adaln(cond, ada_w, ada_b, lamv)
    mod0, mod1 = _mod_select(mods[0], b), _mod_select(mods[1], b)

    w1, wuq, wkv = _even_weights(e_w_in[0], e_w_uq[0], e_w_ukv[0])
    tabs = _rope_tables(n, cl)
    qt, ka, vat, dqt, dk, dvt, gate = _even_proj(
        x, ctx, mod0, w1, e_q_norm_g[0][None], wuq, e_kv_norm_g[0][None], wkv, tabs)
    subln_col = e_subln_g[0][:, None]
    oax = _mla_attn(qt, ka, vat, n, n + cl)
    obx = _diff_attn(dqt, dk, dvt, lam, subln_col, n, n + cl)
    oac = _mla_attn(qt, ka, vat, cl, cl, tail=True)
    obc = _diff_attn(dqt, dk, dvt, lam, subln_col, cl, cl, tail=True)
    xc, u_all, gate1 = _even_merge(x, ctx, mod0, oax, oac, obx, obc, gate, e_w_out[0].astype(bf),
                                   post_ln_g[0][None], post_ln_b[0][None], mod1, o_w_in[0].astype(bf))

    cw, cb = o_conv_w[0], o_conv_b[0][None]
    wa, wx = o_gate_a_w[0].astype(bf), o_gate_x_w[0].astype(bf)
    w = u_all.shape[2]
    ba = o_gate_a_b[0].reshape(2, 1, w)
    bx = o_gate_x_b[0].reshape(2, 1, w)
    lru_lam = o_lru_lambda[0].reshape(2, 1, w)
    h0 = _lru_ctx(u_all, n, cl, cw, cb, wa, ba, wx, bx, lru_lam)
    hf, uc = _lru_fwd(u_all, n, h0, cw, cb, wa, ba, wx, bx, lru_lam)
    return _lru_bwd_out(uc, h0, wa, ba, wx, bx, lru_lam, hf, gate1, xc, mod1, o_w_out[0].astype(bf),
                        post_ln_g[1][None], post_ln_b[1][None])
```

```python
import functools
import math

import jax
import jax.numpy as jnp
import numpy as np
from jax import lax
from jax.experimental import pallas as pl
from jax.experimental.pallas import tpu as pltpu

GRID_W = 64
ROPE_THETA = 10000.0
LN_EPS = 1e-6
RMS_EPS = 1e-6

MLA_HEADS = 8
MLA_Q_LORA = 384
MLA_KV_LORA = 256
MLA_NOPE = 64
MLA_ROPE = 32
MLA_V = 64
MLA_QK = MLA_NOPE + MLA_ROPE
MLA_SCALE = MLA_QK ** -0.5

DIFF_HEADS = 4
DIFF_QK = 64
DIFF_V = 2 * DIFF_QK
DIFF_SCALE = DIFF_QK ** -0.5

LRU_BLOCKS = 8
LRU_C = 8.0

DEPTH = 2
DEEPNORM_ALPHA = (2 * DEPTH) ** 0.25
LAMBDA_INIT_0 = 0.8 - 0.6 * math.exp(-0.3 * 0)

LANE = 128
SUBLANE = 8
MXU_DEPTH = 256
LOG2E = 1.4426950408889634
NEG_BIG = -1e30

TOKEN_TILE = 256
Q_TILE = 512
KV_CHUNK = 1408
SCAN_TILE = 512
VMEM_LIMIT = 56 * 1024 * 1024

_G_CQ = (0, 384)
_G_CKV = (384, 640)
_G_KR = (640, 768)
_G_DQ = (768, 1280)
_G_DK = (1280, 1792)
_G_DV = (1792, 2304)
_G_GATE = (2304, 3328)


def _cparams(*sem):
    return pltpu.CompilerParams(dimension_semantics=sem, vmem_limit_bytes=VMEM_LIMIT)


def _bf16_dot(a, b):
    return jnp.dot(a.astype(jnp.bfloat16), b.astype(jnp.bfloat16), preferred_element_type=jnp.float32)


def _sigmoid(x):
    return 1.0 / (1.0 + jnp.exp(-x))


def _adaln_kernel(cond_ref, w_ref, b_ref, lamv_ref, mod_ref, lam_ref):
    cond = cond_ref[...]
    h = cond * _sigmoid(cond)
    mod_ref[0] = jnp.dot(h, w_ref[0], preferred_element_type=jnp.float32,
                         precision=lax.Precision.HIGHEST) + b_ref[0]
    lv = lamv_ref[...]
    d1 = jnp.sum(lv[0:1] * lv[1:2], axis=-1, keepdims=True)
    d2 = jnp.sum(lv[2:3] * lv[3:4], axis=-1, keepdims=True)
    lam = jnp.exp(d1) - jnp.exp(d2) + LAMBDA_INIT_0
    lam_ref[...] = jnp.broadcast_to(lam, lam_ref.shape)


def _adaln(cond, ada_w, ada_b, lamv):
    depth, d, d3 = ada_w.shape
    nj = d3 // d
    return pl.pallas_call(
        _adaln_kernel,
        grid=(depth, nj),
        in_specs=[
            pl.BlockSpec((SUBLANE, d), lambda l, j: (0, 0)),
            pl.BlockSpec((1, d, d), lambda l, j: (l, 0, j)),
            pl.BlockSpec((1, 1, d), lambda l, j: (l, 0, j)),
            pl.BlockSpec(lamv.shape, lambda l, j: (0, 0)),
        ],
        out_specs=[
            pl.BlockSpec((1, SUBLANE, d), lambda l, j: (l, 0, j)),
            pl.BlockSpec((SUBLANE, LANE), lambda l, j: (0, 0)),
        ],
        out_shape=[
            jax.ShapeDtypeStruct((depth, SUBLANE, d3), jnp.float32),
            jax.ShapeDtypeStruct((SUBLANE, LANE), jnp.float32),
        ],
        compiler_params=_cparams("arbitrary", "arbitrary"),
        name="adaln",
    )(cond, ada_w, ada_b.reshape(depth, 1, d3), lamv)


def _rms(x, g):
    return x * lax.rsqrt(jnp.mean(x * x, axis=-1, keepdims=True) + RMS_EPS) * g


def _rope_group(x, cos, sin, half, first_half_mask):
    partner = jnp.where(first_half_mask, pltpu.roll(x, LANE - half, 1), pltpu.roll(x, half, 1))
    return x * cos + partner * sin


def _even_proj_kernel(nx, x_ref, c_ref, mod_ref, w1_ref, qg_ref, wuq_ref, kvg_ref, wkv_ref,
                      ca_ref, sa_ref, cb_ref, sb_ref,
                      qt_ref, ka_ref, vat_ref, dqt_ref, dk_ref, dvt_ref, gate_ref):
    i = pl.program_id(1)
    xin = jnp.where(i >= nx, c_ref[0], x_ref[0])
    mod = mod_ref[0, 0]
    xm = (xin * (1.0 + mod[1:2]) + mod[0:1]).astype(w1_ref.dtype)

    def zcols(lo, hi):
        return jnp.dot(xm, w1_ref[:, lo:hi], preferred_element_type=jnp.float32)

    lane = lax.broadcasted_iota(jnp.int32, (1, LANE), 1)
    mla_first = jnp.logical_and(lane >= MLA_NOPE, lane < MLA_NOPE + MLA_ROPE // 2)
    diff_first = (lane % DIFF_QK) < DIFF_QK // 2
    ca, sa, cb, sb = ca_ref[...], sa_ref[...], cb_ref[...], sb_ref[...]

    z_lat = zcols(_G_CQ[0], _G_KR[1])
    z_dq = zcols(*_G_DQ)
    z_dk = zcols(*_G_DK)

    cqn = _rms(z_lat[:, _G_CQ[0]:_G_CQ[1]], qg_ref[...])
    q = _bf16_dot(cqn, wuq_ref[...])
    ckvn = _rms(z_lat[:, _G_CKV[0]:_G_CKV[1]], kvg_ref[...])
    kv = _bf16_dot(ckvn, wkv_ref[...])
    z_dv = zcols(*_G_DV)
    gate_ref[0] = zcols(*_G_GATE)

    q_heads = []
    for h in range(MLA_HEADS):
        qh = _rope_group(q[:, h * LANE:(h + 1) * LANE], ca, sa, MLA_ROPE // 2, mla_first)
        q_heads.append(qh * (MLA_SCALE * LOG2E))
    qt_ref[0] = jnp.concatenate(q_heads, axis=1).T.astype(qt_ref.dtype)

    kr = _rope_group(z_lat[:, _G_KR[0]:_G_KR[1]], ca, sa, MLA_ROPE // 2, mla_first)
    for h in range(MLA_HEADS):
        ka_ref[0, :, h * LANE:(h + 1) * LANE] = (kv[:, h * LANE:(h + 1) * LANE] + kr).astype(ka_ref.dtype)
    vat_ref[0] = kv[:, MLA_HEADS * LANE:].T.astype(vat_ref.dtype)

    dq_heads = []
    for h in range(DIFF_HEADS):
        sl = slice(h * LANE, (h + 1) * LANE)
        dq_heads.append(_rope_group(z_dq[:, sl], cb, sb, DIFF_QK // 2, diff_first) * (DIFF_SCALE * LOG2E))
        dk_ref[0, :, sl] = _rope_group(z_dk[:, sl], cb, sb, DIFF_QK // 2, diff_first).astype(dk_ref.dtype)
    dqt_ref[0] = jnp.concatenate(dq_heads, axis=1).T.astype(dqt_ref.dtype)
    dvt_ref[0] = z_dv.T.astype(dvt_ref.dtype)


def _even_proj(x, ctx, modsel, w1, qg, wuq, kvg, wkv, tabs):
    b, n, d = x.shape
    c = ctx.shape[1]
    t = TOKEN_TILE
    nx, nc = n // t, c // t
    nt = n + c
    full = lambda a: pl.BlockSpec(a.shape, lambda bi, i: (0,) * a.ndim)
    tab_spec = pl.BlockSpec((t, LANE), lambda bi, i: (i, 0))
    tok = lambda w: pl.BlockSpec((1, t, w), lambda bi, i: (bi, i, 0))
    tok_t = lambda w: pl.BlockSpec((1, w, t), lambda bi, i: (bi, 0, i))
    bf = jnp.bfloat16
    return pl.pallas_call(
        functools.partial(_even_proj_kernel, nx),
        grid=(b, nx + nc),
        in_specs=[
            pl.BlockSpec((1, t, d), lambda bi, i: (bi, jnp.minimum(i, nx - 1), 0)),
            pl.BlockSpec((1, t, d), lambda bi, i: (bi, jnp.maximum(i - nx, 0), 0)),
            pl.BlockSpec((1, 1, 3, d), lambda bi, i: (bi, jnp.where(i >= nx, 1, 0), 0, 0)),
            full(w1), full(qg), full(wuq), full(kvg), full(wkv),
            tab_spec, tab_spec, tab_spec, tab_spec,
        ],
        out_specs=[
            tok_t(MLA_HEADS * LANE), tok(MLA_HEADS * LANE), tok_t(MLA_HEADS * MLA_V),
            tok_t(DIFF_HEADS * LANE), tok(DIFF_HEADS * LANE), tok_t(DIFF_HEADS * DIFF_V),
            tok(MLA_HEADS * MLA_V + DIFF_HEADS * DIFF_V),
        ],
        out_shape=[
            jax.ShapeDtypeStruct((b, MLA_HEADS * LANE, nt), bf),
            jax.ShapeDtypeStruct((b, nt, MLA_HEADS * LANE), bf),
            jax.ShapeDtypeStruct((b, MLA_HEADS * MLA_V, nt), bf),
            jax.ShapeDtypeStruct((b, DIFF_HEADS * LANE, nt), bf),
            jax.ShapeDtypeStruct((b, nt, DIFF_HEADS * LANE), bf),
            jax.ShapeDtypeStruct((b, DIFF_HEADS * DIFF_V, nt), bf),
            jax.ShapeDtypeStruct((b, nt, MLA_HEADS * MLA_V + DIFF_HEADS * DIFF_V), jnp.float32),
        ],
        compiler_params=_cparams("parallel", "parallel"),
        name="even_proj",
    )(x, ctx, modsel, w1, qg, wuq, kvg, wkv, *tabs)


def _kv_chunk(nk):
    if nk <= KV_CHUNK:
        return nk
    for m in range(KV_CHUNK // LANE, 0, -1):
        if nk % (m * LANE) == 0 and (nk // (m * LANE)) % 2 == 0:
            return m * LANE
    raise ValueError(f"no even chunking of {nk} keys")


def _attn_sweep(n_chunks, tk, tq, n_qt, q_of, k_ref, vt_ref, s_scr, finish, tile_unroll):
    dv = vt_ref.shape[1]

    def scores(t, j, slot):
        k = k_ref[0, pl.ds(pl.multiple_of(j * tk, LANE), tk), :]
        cms = []
        for si, qt in enumerate(q_of(t)):
            s = jnp.dot(k, qt, preferred_element_type=jnp.float32)
            s_scr[slot, si] = s
            cms.append(jnp.max(s, axis=0, keepdims=True))
        return tuple(cms)

    n_ops = s_scr.shape[1]
    ones = jnp.ones((2 * SUBLANE, MXU_DEPTH), vt_ref.dtype)
    init = (jnp.full((1, tq), NEG_BIG, jnp.float32), jnp.zeros((dv + 2 * SUBLANE, tq), jnp.float32)) * n_ops
    blocks = [(r0, min(MXU_DEPTH, tk - r0)) for r0 in range(0, tk, MXU_DEPTH)]

    def step(nxt, j_cur, slot_cur, cms, state):
        off_cur = pl.multiple_of(j_cur * tk, LANE)
        if nxt is not None:
            t_nxt, j_nxt, slot_nxt = nxt
            off_nxt = pl.multiple_of(j_nxt * tk, LANE)
            q_nxt = q_of(t_nxt)
        m_new = [jnp.maximum(state[2 * si], cms[si]) for si in range(n_ops)]
        cm_nxt, pv = [None] * n_ops, [None] * n_ops
        for r0, r in blocks:
            if nxt is not None:
                k = k_ref[0, pl.ds(off_nxt + r0, r), :]
            vt = vt_ref[0, :, pl.ds(off_cur + r0, r)]
            vt_ext = jnp.concatenate([vt, ones[:, :r]], axis=0)
            for si in range(n_ops):
                if nxt is not None:
                    s = jnp.dot(k, q_nxt[si], preferred_element_type=jnp.float32)
                    s_scr[slot_nxt, si, r0:r0 + r] = s
                    cmax = jnp.max(s, axis=0, keepdims=True)
                    cm_nxt[si] = cmax if cm_nxt[si] is None else jnp.maximum(cm_nxt[si], cmax)
                p = jnp.exp2(s_scr[slot_cur, si, r0:r0 + r] - m_new[si]).astype(vt.dtype)
                d = jnp.dot(vt_ext, p, preferred_element_type=jnp.float32)
                pv[si] = d if pv[si] is None else pv[si] + d
        out = []
        for si in range(n_ops):
            alpha = jnp.exp2(state[2 * si] - m_new[si])
            out += [m_new[si], alpha * state[2 * si + 1] + pv[si]]
        return tuple(cm_nxt), tuple(out)

    def result(st):
        return [st[2 * si + 1][:dv] / st[2 * si + 1][dv:dv + 1] for si in range(n_ops)]

    if n_chunks == 1:
        def tile(t, carry):
            finish(t, result(step(None, 0, 0, scores(t, 0, 0), init)[1]))
            return carry
        lax.fori_loop(0, n_qt, tile, 0)
        return

    assert n_chunks % 2 == 0

    def tile(t, cm):
        def pair(i, carry):
            cm_a, st = carry
            cm_b, st = step((t, 2 * i + 1, 1), 2 * i, 0, cm_a, st)
            cm_a, st = step((t, 2 * i + 2, 0), 2 * i + 1, 1, cm_b, st)
            return cm_a, st

        cm_a, st = lax.fori_loop(0, n_chunks // 2 - 1, pair, (cm, init), unroll=True)
        cm_b, st = step((t, n_chunks - 1, 1), n_chunks - 2, 0, cm_a, st)
        cm_next, st = step((jnp.minimum(t + 1, n_qt - 1), 0, 0), n_chunks - 1, 1, cm_b, st)
        finish(t, result(st))
        return cm_next

    lax.fori_loop(0, n_qt, tile, scores(0, 0, 0), unroll=tile_unroll)


def _q_cols(t, tq):
    return pl.ds(pl.multiple_of(t * tq, LANE), tq)


def _mla_attn_kernel(n_chunks, tk, tq, qt_ref, k_ref, vt_ref, o_ref, s_scr):
    n_ops = s_scr.shape[1]

    def q_of(t):
        return [qt_ref[0, :, _q_cols(t * n_ops + i, tq)] for i in range(n_ops)]

    def finish(t, outs):
        for i in range(n_ops):
            o_ref[0, :, _q_cols(t * n_ops + i, tq)] = outs[i]

    _attn_sweep(n_chunks, tk, tq, qt_ref.shape[2] // (tq * n_ops), q_of, k_ref, vt_ref, s_scr, finish,
                tile_unroll=4)


def _attn_call(kernel_fn, name, heads, dv, n_ops, qt, k, vt, nq, nk, tail, extra=()):
    b, _, nt = vt.shape
    assert not tail or ((nt - nq) % nq == 0 and (nt - nk) % nk == 0)
    qb = (nt - nq) // nq if tail else 0
    kb = (nt - nk) // nk if tail else 0
    tq = min(Q_TILE, nq)
    tk = _kv_chunk(nk)
    return pl.pallas_call(
        functools.partial(kernel_fn, nk // tk, tk, tq),
        grid=(b, heads),
        in_specs=[
            pl.BlockSpec((1, LANE, nq), lambda bi, h: (bi, h, qb)),
            pl.BlockSpec((1, nk, LANE), lambda bi, h: (bi, kb, h)),
            pl.BlockSpec((1, dv, nk), lambda bi, h: (bi, h, kb)),
        ] + [pl.BlockSpec(a.shape, lambda bi, h: (0, 0)) for a in extra],
        out_specs=pl.BlockSpec((1, dv, nq), lambda bi, h: (bi, h, 0)),
        out_shape=jax.ShapeDtypeStruct((b, heads * dv, nq), jnp.float32),
        scratch_shapes=[pltpu.VMEM((2, n_ops, tk, tq), jnp.float32)],
        compiler_params=_cparams("parallel", "parallel"),
        name=name,
    )(qt, k, vt, *extra)


def _mla_attn(qt, ka, vat, nq, nk, tail=False):
    n_ops = 2 if nq % (2 * Q_TILE) == 0 else 1
    return _attn_call(_mla_attn_kernel, "mla_attn", MLA_HEADS, MLA_V, n_ops, qt, ka, vat, nq, nk, tail)


def _diff_attn_kernel(n_chunks, tk, tq, qt_ref, k_ref, vt_ref, lam_ref, g_ref, o_ref, s_scr):
    row = lax.broadcasted_iota(jnp.int32, (LANE, 1), 0)

    def q_of(t):
        qt = qt_ref[0, :, _q_cols(t, tq)]
        zero = jnp.zeros_like(qt)
        return [jnp.where(row < DIFF_QK, qt, zero), jnp.where(row >= DIFF_QK, qt, zero)]

    def finish(t, outs):
        o = outs[0] - lam_ref[0:1, 0:1] * outs[1]
        o = o * lax.rsqrt(jnp.mean(o * o, axis=0, keepdims=True) + RMS_EPS) * g_ref[...]
        o_ref[0, :, _q_cols(t, tq)] = o * (1.0 - LAMBDA_INIT_0)

    _attn_sweep(n_chunks, tk, tq, qt_ref.shape[2] // tq, q_of, k_ref, vt_ref, s_scr, finish, tile_unroll=2)


def _diff_attn(dqt, dk, dvt, lam, subln_col, nq, nk, tail=False):
    return _attn_call(_diff_attn_kernel, "diff_attn", DIFF_HEADS, DIFF_V, 2, dqt, dk, dvt, nq, nk, tail,
                      extra=(lam, subln_col))


def _residual_ln(xin, y, gate_vec, ln_g, ln_b):
    r = DEEPNORM_ALPHA * xin + gate_vec * y
    mu = jnp.mean(r, axis=-1, keepdims=True)
    rc = r - mu
    var = jnp.mean(rc * rc, axis=-1, keepdims=True)
    return rc * lax.rsqrt(var + LN_EPS) * ln_g + ln_b


def _even_merge_kernel(nx, x_ref, c_ref, mod_ref, oax_ref, oac_ref, obx_ref, obc_ref, gate_ref,
                       wout_ref, lng_ref, lnb_ref, mod1_ref, w1_ref, out_ref, u_ref, gate1_ref):
    def run(xin_ref, oa_ref, ob_ref):
        o = jnp.concatenate([oa_ref[0].T, ob_ref[0].T], axis=1)
        g = gate_ref[0]
        y = _bf16_dot(o * (g * _sigmoid(g)), wout_ref[...])
        xc = _residual_ln(xin_ref[0], y, mod_ref[0, 0][2:3], lng_ref[...], lnb_ref[...])
        out_ref[0] = xc

        mod1 = mod1_ref[0, 0]
        z = _bf16_dot(xc * (1.0 + mod1[1:2]) + mod1[0:1], w1_ref[...])
        w = u_ref.shape[2]
        u_ref[0] = z[:, :w]
        gate1_ref[0] = z[:, w:]

    is_ctx = pl.program_id(1) >= nx
    pl.when(jnp.logical_not(is_ctx))(functools.partial(run, x_ref, oax_ref, obx_ref))
    pl.when(is_ctx)(functools.partial(run, c_ref, oac_ref, obc_ref))


def _even_merge(x, ctx, modsel, oax, oac, obx, obc, gate, wout, ln_g, ln_b, modsel1, w_in1):
    b, n, d = x.shape
    c = ctx.shape[1]
    t = TOKEN_TILE
    nx, nc = n // t, c // t
    wa, wb = oax.shape[1], obx.shape[1]
    w = w_in1.shape[1] // 2
    full = lambda a: pl.BlockSpec(a.shape, lambda bi, i: (0,) * a.ndim)
    xi = lambda bi, i: (bi, jnp.minimum(i, nx - 1), 0)
    ci = lambda bi, i: (bi, jnp.maximum(i - nx, 0), 0)
    xit = lambda bi, i: (bi, 0, jnp.minimum(i, nx - 1))
    cit = lambda bi, i: (bi, 0, jnp.maximum(i - nx, 0))
    mod_spec = pl.BlockSpec((1, 1, 3, d), lambda bi, i: (bi, jnp.where(i >= nx, 1, 0), 0, 0))
    tok = lambda width: pl.BlockSpec((1, t, width), lambda bi, i: (bi, i, 0))
    return pl.pallas_call(
        functools.partial(_even_merge_kernel, nx),
        grid=(b, nx + nc),
        in_specs=[
            pl.BlockSpec((1, t, d), xi),
            pl.BlockSpec((1, t, d), ci),
            mod_spec,
            pl.BlockSpec((1, wa, t), xit), pl.BlockSpec((1, wa, t), cit),
            pl.BlockSpec((1, wb, t), xit), pl.BlockSpec((1, wb, t), cit),
            tok(wa + wb),
            full(wout), full(ln_g), full(ln_b),
            mod_spec, full(w_in1),
        ],
        out_specs=[tok(d), tok(w), tok(w)],
        out_shape=[jax.ShapeDtypeStruct((b, n + c, d), jnp.float32),
                   jax.ShapeDtypeStruct((b, n + c, w), jnp.float32),
                   jax.ShapeDtypeStruct((b, n + c, w), jnp.float32)],
        compiler_params=_cparams("parallel", "parallel"),
        name="even_merge",
    )(x, ctx, modsel, oax, oac, obx, obc, gate, wout, ln_g, ln_b, modsel1, w_in1)


def _conv_tile(u, prev, nxt, cw, cb):
    t = u.shape[0]
    row = lax.broadcasted_iota(jnp.int32, (SUBLANE, 1), 0)
    r1, r2, rp = pltpu.roll(u, 1, 0), pltpu.roll(u, 2, 0), pltpu.roll(u, t - 1, 0)
    m1 = jnp.concatenate([jnp.where(row == 0, prev[7:8], r1[:SUBLANE]), r1[SUBLANE:]], 0)
    m2 = jnp.concatenate([jnp.where(row == 0, prev[6:7], jnp.where(row == 1, prev[7:8], r2[:SUBLANE])),
                          r2[SUBLANE:]], 0)
    p1 = jnp.concatenate([rp[:t - SUBLANE], jnp.where(row == SUBLANE - 1, nxt[0:1], rp[t - SUBLANE:])], 0)
    return cw[0:1] * m2 + cw[1:2] * m1 + cw[2:3] * u + cw[3:4] * p1 + cb


def _lru_coeffs(uc, wa_ref, ba, wx_ref, bx, lam, a_ref, b_ref):
    neg_log_a1 = LRU_C * jnp.log1p(jnp.exp(-lam))
    for k in range(LRU_BLOCKS):
        sl = slice(k * LANE, (k + 1) * LANE)
        ub = uc[:, sl]
        r = _sigmoid(_bf16_dot(ub, wa_ref[k]) + ba[:, sl])
        ig = _sigmoid(_bf16_dot(ub, wx_ref[k]) + bx[:, sl])
        a = jnp.exp2(r * (-LOG2E * neg_log_a1[:, sl]))
        a_ref[:, sl] = a
        y = jnp.tanh(r * neg_log_a1[:, sl]) * (a * a + 1.0)
        root = jnp.where(y > 0.0, y * lax.rsqrt(y), 0.0)
        b_ref[:, sl] = root * (ig * ub)


def _scan_rows(a_ref, b_ref, h_out_ref, h0, t, reverse):
    def step(s, h):
        r = (t - 1 - s) if reverse else s
        h = a_ref[pl.ds(r, 1), :] * h + b_ref[pl.ds(r, 1), :]
        h_out_ref[pl.ds(r, 1), :] = h
        return h

    return lax.fori_loop(0, t, step, h0, unroll=8)


def _lru_ctx_kernel(u_ref, cw_ref, cb_ref, wa_ref, ba_ref, wx_ref, bx_ref, lam_ref, h_ref,
                    a_s, b_s, hs):
    u = u_ref[0]
    zero8 = jnp.zeros((SUBLANE, u.shape[1]), jnp.float32)
    uc = _conv_tile(u, zero8, zero8, cw_ref[...], cb_ref[...])
    t = u.shape[0]
    for d in range(2):
        _lru_coeffs(uc, wa_ref.at[d], ba_ref[d], wx_ref.at[d], bx_ref[d], lam_ref[d], a_s, b_s)
        h = _scan_rows(a_s, b_s, hs, jnp.zeros((1, u.shape[1]), jnp.float32), t, reverse=(d == 1))
        h_ref[0, d:d + 1, :] = h


def _lru_ctx(u_all, n, c, cw, cb, wa, ba, wx, bx, lam):
    b, _, w = u_all.shape
    full = lambda a: pl.BlockSpec(a.shape, lambda bi: (0,) * a.ndim)
    return pl.pallas_call(
        _lru_ctx_kernel,
        grid=(b,),
        in_specs=[pl.BlockSpec((1, c, w), lambda bi: (bi, n // c, 0)),
                  full(cw), full(cb), full(wa), full(ba), full(wx), full(bx), full(lam)],
        out_specs=pl.BlockSpec((1, 2, w), lambda bi: (bi, 0, 0)),
        out_shape=jax.ShapeDtypeStruct((b, 2, w), jnp.float32),
        scratch_shapes=[pltpu.VMEM((c, w), jnp.float32)] * 3,
        compiler_params=_cparams("parallel"),
        name="lru_ctx",
    )(u_all, cw, cb, wa, ba, wx, bx, lam)


def _lru_fwd_kernel(n_tiles, u_ref, up_ref, un_ref, h0_ref, cw_ref, cb_ref, wa_ref, ba_ref, wx_ref, bx_ref,
                    lam_ref, hf_ref, uc_ref, a_s, b_s, carry):
    tile = pl.program_id(1)

    @pl.when(tile == 0)
    def _():
        carry[...] = h0_ref[0, 0:1, :]

    u = u_ref[0]
    prev = jnp.where(tile == 0, 0.0, up_ref[0])
    nxt = jnp.where(tile == n_tiles - 1, 0.0, un_ref[0])
    uc = _conv_tile(u, prev, nxt, cw_ref[...], cb_ref[...])
    uc_ref[0] = uc
    _lru_coeffs(uc, wa_ref.at[0], ba_ref[0], wx_ref.at[0], bx_ref[0], lam_ref[0], a_s, b_s)
    carry[...] = _scan_rows(a_s, b_s, hf_ref.at[0], carry[...], u.shape[0], reverse=False)


def _lru_bwd_kernel(uc_ref, h0_ref, wa_ref, ba_ref, wx_ref, bx_ref, lam_ref, hf_ref, gate_ref, x_ref, mod_ref,
                    wout_ref, lng_ref, lnb_ref, out_ref, a_s, b_s, hs, carry):
    @pl.when(pl.program_id(1) == 0)
    def _():
        carry[...] = h0_ref[0, 1:2, :]

    uc = uc_ref[0]
    _lru_coeffs(uc, wa_ref.at[1], ba_ref[1], wx_ref.at[1], bx_ref[1], lam_ref[1], a_s, b_s)
    carry[...] = _scan_rows(a_s, b_s, hs, carry[...], uc.shape[0], reverse=True)
    g = gate_ref[0]
    hx = hf_ref[0] + hs[...]
    y = _bf16_dot(hx * (g * _sigmoid(g)), wout_ref[...])
    out_ref[0] = _residual_ln(x_ref[0], y, mod_ref[0, 0][2:3], lng_ref[...], lnb_ref[...])


def _lru_fwd(u_all, n, h0, cw, cb, wa, ba, wx, bx, lam):
    b, _, w = u_all.shape
    t = min(SCAN_TILE, n)
    n_tiles = n // t
    r = t // SUBLANE
    full = lambda a: pl.BlockSpec(a.shape, lambda bi, s: (0,) * a.ndim)
    tile_spec = pl.BlockSpec((1, t, w), lambda bi, s: (bi, s, 0))
    return pl.pallas_call(
        functools.partial(_lru_fwd_kernel, n_tiles),
        grid=(b, n_tiles),
        in_specs=[
            tile_spec,
            pl.BlockSpec((1, SUBLANE, w), lambda bi, s: (bi, jnp.maximum(s * r - 1, 0), 0)),
            pl.BlockSpec((1, SUBLANE, w), lambda bi, s: (bi, (s + 1) * r, 0)),
            pl.BlockSpec((1, 2, w), lambda bi, s: (bi, 0, 0)),
            full(cw), full(cb), full(wa), full(ba), full(wx), full(bx), full(lam),
        ],
        out_specs=[tile_spec, tile_spec],
        out_shape=[jax.ShapeDtypeStruct((b, n, w), jnp.float32)] * 2,
        scratch_shapes=[pltpu.VMEM((t, w), jnp.float32), pltpu.VMEM((t, w), jnp.float32),
                        pltpu.VMEM((1, w), jnp.float32)],
        compiler_params=_cparams("parallel", "arbitrary"),
        name="lru_fwd",
    )(u_all, u_all, u_all, h0, cw, cb, wa, ba, wx, bx, lam)


def _lru_bwd_out(uc, h0, wa, ba, wx, bx, lam, hf, gate, xc, modsel, wout, ln_g, ln_b):
    b, n, w = uc.shape
    d = xc.shape[2]
    t = min(SCAN_TILE, n)
    n_tiles = n // t
    full = lambda a: pl.BlockSpec(a.shape, lambda bi, s: (0,) * a.ndim)
    tile_spec = lambda width: pl.BlockSpec((1, t, width), lambda bi, s: (bi, n_tiles - 1 - s, 0))
    return pl.pallas_call(
        _lru_bwd_kernel,
        grid=(b, n_tiles),
        in_specs=[
            tile_spec(w),
            pl.BlockSpec((1, 2, w), lambda bi, s: (bi, 0, 0)),
            full(wa), full(ba), full(wx), full(bx), full(lam),
            tile_spec(w), tile_spec(w), tile_spec(d),
            pl.BlockSpec((1, 1, 3, d), lambda bi, s: (bi, 0, 0, 0)),
            full(wout), full(ln_g), full(ln_b),
        ],
        out_specs=tile_spec(d),
        out_shape=jax.ShapeDtypeStruct((b, n, d), jnp.float32),
        scratch_shapes=[pltpu.VMEM((t, w), jnp.float32)] * 3 + [pltpu.VMEM((1, w), jnp.float32)],
        compiler_params=_cparams("parallel", "arbitrary"),
        name="lru_bwd_out",
    )(uc, h0, wa, ba, wx, bx, lam, hf, gate, xc, modsel, wout, ln_g, ln_b)


def _rope_tables(n, c):
    t = np.arange(n)
    rows = (t // GRID_W).astype(np.float32)
    cols = (t % GRID_W).astype(np.float32)

    def ang(rot_dim):
        n_freq = rot_dim // 4
        freqs = np.float32(ROPE_THETA) ** (-np.arange(n_freq, dtype=np.float32) / np.float32(n_freq))
        a = np.concatenate([rows[:, None] * freqs, cols[:, None] * freqs], -1)
        return np.cos(a.astype(np.float64)), np.sin(a.astype(np.float64))

    (cos_a, sin_a), (cos_b, sin_b) = ang(MLA_ROPE), ang(DIFF_QK)
    one = np.ones((n, MLA_NOPE))
    zero = np.zeros((n, MLA_NOPE))
    pad1 = np.ones((n, LANE - MLA_QK))
    pad0 = np.zeros((n, LANE - MLA_QK))
    ca = np.concatenate([one, cos_a, cos_a, pad1], 1)
    sa = np.concatenate([zero, -sin_a, sin_a, pad0], 1)
    cb = np.concatenate([cos_b, cos_b] * 2, 1)
    sb = np.concatenate([-sin_b, sin_b] * 2, 1)
    ident = lambda tab, v: jnp.asarray(np.concatenate([tab, np.full((c, LANE), v)], 0), jnp.float32)
    return ident(ca, 1.0), ident(sa, 0.0), ident(cb, 1.0), ident(sb, 0.0)


def _even_weights(w_in, w_uq, w_ukv):
    d = w_in.shape[0]
    bf = jnp.bfloat16
    o = [0, 384, 640, 672, 1184, 1696, 2208, 3232]
    cq, ckv, kr, dq, dk, dv, gate = (w_in[:, o[i]:o[i + 1]] for i in range(7))
    kr_group = jnp.concatenate([jnp.zeros((d, MLA_NOPE), w_in.dtype), kr,
                                jnp.zeros((d, LANE - MLA_QK), w_in.dtype)], 1)
    w1 = jnp.concatenate([cq, ckv, kr_group, dq, dk, dv, gate], 1).astype(bf)
    wuq = jnp.pad(w_uq.reshape(MLA_Q_LORA, MLA_HEADS, MLA_QK),
                  ((0, 0), (0, 0), (0, LANE - MLA_QK))).reshape(MLA_Q_LORA, MLA_HEADS * LANE).astype(bf)
    ukv = w_ukv.reshape(MLA_KV_LORA, MLA_HEADS, MLA_NOPE + MLA_V)
    wuk = jnp.pad(ukv[..., :MLA_NOPE], ((0, 0), (0, 0), (0, LANE - MLA_NOPE))).reshape(MLA_KV_LORA, MLA_HEADS * LANE)
    wuv = ukv[..., MLA_NOPE:].reshape(MLA_KV_LORA, MLA_HEADS * MLA_V)
    wkv = jnp.concatenate([wuk, wuv], 1).astype(bf)
    return w1, wuq, wkv


def _mod_select(mods_l, b):
    d = mods_l.shape[1] // 3
    mx = mods_l[:b].reshape(b, 3, d)
    mc = jnp.broadcast_to(mods_l[b].reshape(1, 3, d), (b, 3, d))
    return jnp.stack([mx, mc], axis=1)


def kernel(x, c, ctx, c_ctx, ada_w, ada_b, post_ln_g, post_ln_b, e_w_in, e_q_norm_g, e_w_uq, e_kv_norm_g, e_w_ukv, e_lam_q1, e_lam_k1, e_lam_q2, e_lam_k2, e_subln_g, e_w_out, o_w_in, o_conv_w, o_conv_b, o_gate_a_w, o_gate_a_b, o_gate_x_w, o_gate_x_b, o_lru_lambda, o_w_out):
    b, n, d = x.shape
    cl = ctx.shape[1]
    assert ada_w.shape[0] == DEPTH and b < SUBLANE
    assert n % Q_TILE == 0 and cl % TOKEN_TILE == 0 and n % cl == 0
    bf = jnp.bfloat16

    cond = jnp.concatenate([c, c_ctx[None], jnp.zeros((SUBLANE - b - 1, d), c.dtype)], 0)
    lamv = jnp.concatenate([e_lam_q1, e_lam_k1, e_lam_q2, e_lam_k2], 0)
    mods, lam = _adaln(cond, ada_w, ada_b, lamv)
    mod0, mod1 = _mod_select(mods[0], b), _mod_select(mods[1], b)

    w1, wuq, wkv = _even_weights(e_w_in[0], e_w_uq[0], e_w_ukv[0])
    tabs = _rope_tables(n, cl)
    qt, ka, vat, dqt, dk, dvt, gate = _even_proj(
        x, ctx, mod0, w1, e_q_norm_g[0][None], wuq, e_kv_norm_g[0][None], wkv, tabs)
    subln_col = e_subln_g[0][:, None]
    oax = _mla_attn(qt, ka, vat, n, n + cl)
    obx = _diff_attn(dqt, dk, dvt, lam, subln_col, n, n + cl)
    oac = _mla_attn(qt, ka, vat, cl, cl, tail=True)
    obc = _diff_attn(dqt, dk, dvt, lam, subln_col, cl, cl, tail=True)
    xc, u_all, gate1 = _even_merge(x, ctx, mod0, oax, oac, obx, obc, gate, e_w_out[0].astype(bf),
                                   post_ln_g[0][None], post_ln_b[0][None], mod1, o_w_in[0].astype(bf))

    cw, cb = o_conv_w[0], o_conv_b[0][None]
    wa, wx = o_gate_a_w[0].astype(bf), o_gate_x_w[0].astype(bf)
    w = u_all.shape[2]
    ba = o_gate_a_b[0].reshape(2, 1, w)
    bx = o_gate_x_b[0].reshape(2, 1, w)
    lru_lam = o_lru_lambda[0].reshape(2, 1, w)
    h0 = _lru_ctx(u_all, n, cl, cw, cb, wa, ba, wx, bx, lru_lam)
    hf, uc = _lru_fwd(u_all, n, h0, cw, cb, wa, ba, wx, bx, lru_lam)
    return _lru_bwd_out(uc, h0, wa, ba, wx, bx, lru_lam, hf, gate1, xc, mod1, o_w_out[0].astype(bf),
                        post_ln_g[1][None], post_ln_b[1][None])
```

```python
import functools
import math

import jax
import jax.numpy as jnp
import numpy as np
from jax import lax
from jax.experimental import pallas as pl
from jax.experimental.pallas import tpu as pltpu

GRID_W = 64
ROPE_THETA = 10000.0
LN_EPS = 1e-6
RMS_EPS = 1e-6

MLA_HEADS = 8
MLA_Q_LORA = 384
MLA_KV_LORA = 256
MLA_NOPE = 64
MLA_ROPE = 32
MLA_V = 64
MLA_QK = MLA_NOPE + MLA_ROPE
MLA_SCALE = MLA_QK ** -0.5

DIFF_HEADS = 4
DIFF_QK = 64
DIFF_V = 2 * DIFF_QK
DIFF_SCALE = DIFF_QK ** -0.5

LRU_BLOCKS = 8
LRU_C = 8.0

DEPTH = 2
DEEPNORM_ALPHA = (2 * DEPTH) ** 0.25
LAMBDA_INIT_0 = 0.8 - 0.6 * math.exp(-0.3 * 0)

LANE = 128
SUBLANE = 8
MXU_DEPTH = 256
LOG2E = 1.4426950408889634
NEG_BIG = -1e30

TOKEN_TILE = 256
Q_TILE = 512
KV_CHUNK = 1408
SCAN_TILE = 512
VMEM_LIMIT = 56 * 1024 * 1024

_G_CQ = (0, 384)
_G_CKV = (384, 640)
_G_KR = (640, 768)
_G_DQ = (768, 1280)
_G_DK = (1280, 1792)
_G_DV = (1792, 2304)
_G_GATE = (2304, 3328)


def _cparams(*sem):
    return pltpu.CompilerParams(dimension_semantics=sem, vmem_limit_bytes=VMEM_LIMIT)


def _bf16_dot(a, b):
    return jnp.dot(a.astype(jnp.bfloat16), b.astype(jnp.bfloat16), preferred_element_type=jnp.float32)


def _sigmoid(x):
    return 1.0 / (1.0 + jnp.exp(-x))


def _adaln_kernel(cond_ref, w_ref, b_ref, lamv_ref, mod_ref, lam_ref):
    cond = cond_ref[...]
    h = cond * _sigmoid(cond)
    mod_ref[0] = jnp.dot(h, w_ref[0], preferred_element_type=jnp.float32,
                         precision=lax.Precision.HIGHEST) + b_ref[0]
    lv = lamv_ref[...]
    d1 = jnp.sum(lv[0:1] * lv[1:2], axis=-1, keepdims=True)
    d2 = jnp.sum(lv[2:3] * lv[3:4], axis=-1, keepdims=True)
    lam = jnp.exp(d1) - jnp.exp(d2) + LAMBDA_INIT_0
    lam_ref[...] = jnp.broadcast_to(lam, lam_ref.shape)


def _adaln(cond, ada_w, ada_b, lamv):
    depth, d, d3 = ada_w.shape
    nj = d3 // d
    return pl.pallas_call(
        _adaln_kernel,
        grid=(depth, nj),
        in_specs=[
            pl.BlockSpec((SUBLANE, d), lambda l, j: (0, 0)),
            pl.BlockSpec((1, d, d), lambda l, j: (l, 0, j)),
            pl.BlockSpec((1, 1, d), lambda l, j: (l, 0, j)),
            pl.BlockSpec(lamv.shape, lambda l, j: (0, 0)),
        ],
        out_specs=[
            pl.BlockSpec((1, SUBLANE, d), lambda l, j: (l, 0, j)),
            pl.BlockSpec((SUBLANE, LANE), lambda l, j: (0, 0)),
        ],
        out_shape=[
            jax.ShapeDtypeStruct((depth, SUBLANE, d3), jnp.float32),
            jax.ShapeDtypeStruct((SUBLANE, LANE), jnp.float32),
        ],
        compiler_params=_cparams("arbitrary", "arbitrary"),
        name="adaln",
    )(cond, ada_w, ada_b.reshape(depth, 1, d3), lamv)


def _rms(x, g):
    return x * lax.rsqrt(jnp.mean(x * x, axis=-1, keepdims=True) + RMS_EPS) * g


def _rope_group(x, cos, sin, half, first_half_mask):
    partner = jnp.where(first_half_mask, pltpu.roll(x, LANE - half, 1), pltpu.roll(x, half, 1))
    return x * cos + partner * sin


def _even_proj_kernel(nx, x_ref, c_ref, mod_ref, w1_ref, qg_ref, wuq_ref, kvg_ref, wkv_ref,
                      ca_ref, sa_ref, cb_ref, sb_ref,
                      qt_ref, ka_ref, vat_ref, dqt_ref, dk_ref, dvt_ref, gate_ref):
    i = pl.program_id(1)
    xin = jnp.where(i >= nx, c_ref[0], x_ref[0])
    mod = mod_ref[0, 0]
    xm = (xin * (1.0 + mod[1:2]) + mod[0:1]).astype(w1_ref.dtype)

    def zcols(lo, hi):
        return jnp.dot(xm, w1_ref[:, lo:hi], preferred_element_type=jnp.float32)

    lane = lax.broadcasted_iota(jnp.int32, (1, LANE), 1)
    mla_first = jnp.logical_and(lane >= MLA_NOPE, lane < MLA_NOPE + MLA_ROPE // 2)
    diff_first = (lane % DIFF_QK) < DIFF_QK // 2
    ca, sa, cb, sb = ca_ref[...], sa_ref[...], cb_ref[...], sb_ref[...]

    z_lat = zcols(_G_CQ[0], _G_KR[1])
    z_dq = zcols(*_G_DQ)
    z_dk = zcols(*_G_DK)

    cqn = _rms(z_lat[:, _G_CQ[0]:_G_CQ[1]], qg_ref[...])
    q = _bf16_dot(cqn, wuq_ref[...])
    ckvn = _rms(z_lat[:, _G_CKV[0]:_G_CKV[1]], kvg_ref[...])
    kv = _bf16_dot(ckvn, wkv_ref[...])
    z_dv = zcols(*_G_DV)
    gate_ref[0] = zcols(*_G_GATE)

    q_heads = []
    for h in range(MLA_HEADS):
        qh = _rope_group(q[:, h * LANE:(h + 1) * LANE], ca, sa, MLA_ROPE // 2, mla_first)
        q_heads.append(qh * (MLA_SCALE * LOG2E))
    qt_ref[0] = jnp.concatenate(q_heads, axis=1).T.astype(qt_ref.dtype)

    kr = _rope_group(z_lat[:, _G_KR[0]:_G_KR[1]], ca, sa, MLA_ROPE // 2, mla_first)
    for h in range(MLA_HEADS):
        ka_ref[0, :, h * LANE:(h + 1) * LANE] = (kv[:, h * LANE:(h + 1) * LANE] + kr).astype(ka_ref.dtype)
    vat_ref[0] = kv[:, MLA_HEADS * LANE:].T.astype(vat_ref.dtype)

    dq_heads = []
    for h in range(DIFF_HEADS):
        sl = slice(h * LANE, (h + 1) * LANE)
        dq_heads.append(_rope_group(z_dq[:, sl], cb, sb, DIFF_QK // 2, diff_first) * (DIFF_SCALE * LOG2E))
        dk_ref[0, :, sl] = _rope_group(z_dk[:, sl], cb, sb, DIFF_QK // 2, diff_first).astype(dk_ref.dtype)
    dqt_ref[0] = jnp.concatenate(dq_heads, axis=1).T.astype(dqt_ref.dtype)
    dvt_ref[0] = z_dv.T.astype(dvt_ref.dtype)


def _even_proj(x, ctx, modsel, w1, qg, wuq, kvg, wkv, tabs):
    b, n, d = x.shape
    c = ctx.shape[1]
    t = TOKEN_TILE
    nx, nc = n // t, c // t
    nt = n + c
    full = lambda a: pl.BlockSpec(a.shape, lambda bi, i: (0,) * a.ndim)
    tab_spec = pl.BlockSpec((t, LANE), lambda bi, i: (i, 0))
    tok = lambda w: pl.BlockSpec((1, t, w), lambda bi, i: (bi, i, 0))
    tok_t = lambda w: pl.BlockSpec((1, w, t), lambda bi, i: (bi, 0, i))
    bf = jnp.bfloat16
    return pl.pallas_call(
        functools.partial(_even_proj_kernel, nx),
        grid=(b, nx + nc),
        in_specs=[
            pl.BlockSpec((1, t, d), lambda bi, i: (bi, jnp.minimum(i, nx - 1), 0)),
            pl.BlockSpec((1, t, d), lambda bi, i: (bi, jnp.maximum(i - nx, 0), 0)),
            pl.BlockSpec((1, 1, 3, d), lambda bi, i: (bi, jnp.where(i >= nx, 1, 0), 0, 0)),
            full(w1), full(qg), full(wuq), full(kvg), full(wkv),
            tab_spec, tab_spec, tab_spec, tab_spec,
        ],
        out_specs=[
            tok_t(MLA_HEADS * LANE), tok(MLA_HEADS * LANE), tok_t(MLA_HEADS * MLA_V),
            tok_t(DIFF_HEADS * LANE), tok(DIFF_HEADS * LANE), tok_t(DIFF_HEADS * DIFF_V),
            tok(MLA_HEADS * MLA_V + DIFF_HEADS * DIFF_V),
        ],
        out_shape=[
            jax.ShapeDtypeStruct((b, MLA_HEADS * LANE, nt), bf),
            jax.ShapeDtypeStruct((b, nt, MLA_HEADS * LANE), bf),
            jax.ShapeDtypeStruct((b, MLA_HEADS * MLA_V, nt), bf),
            jax.ShapeDtypeStruct((b, DIFF_HEADS * LANE, nt), bf),
            jax.ShapeDtypeStruct((b, nt, DIFF_HEADS * LANE), bf),
            jax.ShapeDtypeStruct((b, DIFF_HEADS * DIFF_V, nt), bf),
            jax.ShapeDtypeStruct((b, nt, MLA_HEADS * MLA_V + DIFF_HEADS * DIFF_V), jnp.float32),
        ],
        compiler_params=_cparams("parallel", "parallel"),
        name="even_proj",
    )(x, ctx, modsel, w1, qg, wuq, kvg, wkv, *tabs)


def _kv_chunk(nk):
    if nk <= KV_CHUNK:
        return nk
    for m in range(KV_CHUNK // LANE, 0, -1):
        if nk % (m * LANE) == 0 and (nk // (m * LANE)) % 2 == 0:
            return m * LANE
    raise ValueError(f"no even chunking of {nk} keys")


def _attn_sweep(n_chunks, tk, tq, n_qt, q_of, k_ref, vt_ref, s_scr, finish):
    dv = vt_ref.shape[1]

    def scores(t, j, slot):
        k = k_ref[0, pl.ds(pl.multiple_of(j * tk, LANE), tk), :]
        cms = []
        for si, qt in enumerate(q_of(t)):
            s = jnp.dot(k, qt, preferred_element_type=jnp.float32)
            s_scr[slot, si] = s
            cms.append(jnp.max(s, axis=0, keepdims=True))
        return tuple(cms)

    n_ops = s_scr.shape[1]
    ones = jnp.ones((2 * SUBLANE, MXU_DEPTH), vt_ref.dtype)
    init = (jnp.full((1, tq), NEG_BIG, jnp.float32), jnp.zeros((dv + 2 * SUBLANE, tq), jnp.float32)) * n_ops
    blocks = [(r0, min(MXU_DEPTH, tk - r0)) for r0 in range(0, tk, MXU_DEPTH)]

    def step(nxt, j_cur, slot_cur, cms, state):
        off_cur = pl.multiple_of(j_cur * tk, LANE)
        if nxt is not None:
            t_nxt, j_nxt, slot_nxt = nxt
            off_nxt = pl.multiple_of(j_nxt * tk, LANE)
            q_nxt = q_of(t_nxt)
        m_new = [jnp.maximum(state[2 * si], cms[si]) for si in range(n_ops)]
        cm_nxt, pv = [None] * n_ops, [None] * n_ops
        for r0, r in blocks:
            if nxt is not None:
                k = k_ref[0, pl.ds(off_nxt + r0, r), :]
            vt = vt_ref[0, :, pl.ds(off_cur + r0, r)]
            vt_ext = jnp.concatenate([vt, ones[:, :r]], axis=0)
            for si in range(n_ops):
                if nxt is not None:
                    s = jnp.dot(k, q_nxt[si], preferred_element_type=jnp.float32)
                    s_scr[slot_nxt, si, r0:r0 + r] = s
                    cmax = jnp.max(s, axis=0, keepdims=True)
                    cm_nxt[si] = cmax if cm_nxt[si] is None else jnp.maximum(cm_nxt[si], cmax)
                p = jnp.exp2(s_scr[slot_cur, si, r0:r0 + r] - m_new[si]).astype(vt.dtype)
                d = jnp.dot(vt_ext, p, preferred_element_type=jnp.float32)
                pv[si] = d if pv[si] is None else pv[si] + d
        out = []
        for si in range(n_ops):
            alpha = jnp.exp2(state[2 * si] - m_new[si])
            out += [m_new[si], alpha * state[2 * si + 1] + pv[si]]
        return tuple(cm_nxt), tuple(out)

    def result(st):
        return [st[2 * si + 1][:dv] / st[2 * si + 1][dv:dv + 1] for si in range(n_ops)]

    if n_chunks == 1:
        def tile(t, carry):
            finish(t, result(step(None, 0, 0, scores(t, 0, 0), init)[1]))
            return carry
        lax.fori_loop(0, n_qt, tile, 0)
        return

    assert n_chunks % 2 == 0

    def tile(t, cm):
        def pair(i, carry):
            cm_a, st = carry
            cm_b, st = step((t, 2 * i + 1, 1), 2 * i, 0, cm_a, st)
            cm_a, st = step((t, 2 * i + 2, 0), 2 * i + 1, 1, cm_b, st)
            return cm_a, st

        cm_a, st = lax.fori_loop(0, n_chunks // 2 - 1, pair, (cm, init), unroll=True)
        cm_b, st = step((t, n_chunks - 1, 1), n_chunks - 2, 0, cm_a, st)
        cm_next, st = step((jnp.minimum(t + 1, n_qt - 1), 0, 0), n_chunks - 1, 1, cm_b, st)
        finish(t, result(st))
        return cm_next

    lax.fori_loop(0, n_qt, tile, scores(0, 0, 0), unroll=2)


def _q_cols(t, tq):
    return pl.ds(pl.multiple_of(t * tq, LANE), tq)


def _mla_attn_kernel(n_chunks, tk, tq, qt_ref, k_ref, vt_ref, o_ref, s_scr):
    n_ops = s_scr.shape[1]

    def q_of(t):
        return [qt_ref[0, :, _q_cols(t * n_ops + i, tq)] for i in range(n_ops)]

    def finish(t, outs):
        for i in range(n_ops):
            o_ref[0, :, _q_cols(t * n_ops + i, tq)] = outs[i]

    _attn_sweep(n_chunks, tk, tq, qt_ref.shape[2] // (tq * n_ops), q_of, k_ref, vt_ref, s_scr, finish)


def _attn_call(kernel_fn, name, heads, dv, n_ops, qt, k, vt, nq, nk, tail, extra=()):
    b, _, nt = vt.shape
    assert not tail or ((nt - nq) % nq == 0 and (nt - nk) % nk == 0)
    qb = (nt - nq) // nq if tail else 0
    kb = (nt - nk) // nk if tail else 0
    tq = min(Q_TILE, nq)
    tk = _kv_chunk(nk)
    return pl.pallas_call(
        functools.partial(kernel_fn, nk // tk, tk, tq),
        grid=(b, heads),
        in_specs=[
            pl.BlockSpec((1, LANE, nq), lambda bi, h: (bi, h, qb)),
            pl.BlockSpec((1, nk, LANE), lambda bi, h: (bi, kb, h)),
            pl.BlockSpec((1, dv, nk), lambda bi, h: (bi, h, kb)),
        ] + [pl.BlockSpec(a.shape, lambda bi, h: (0, 0)) for a in extra],
        out_specs=pl.BlockSpec((1, dv, nq), lambda bi, h: (bi, h, 0)),
        out_shape=jax.ShapeDtypeStruct((b, heads * dv, nq), jnp.float32),
        scratch_shapes=[pltpu.VMEM((2, n_ops, tk, tq), jnp.float32)],
        compiler_params=_cparams("parallel", "parallel"),
        name=name,
    )(qt, k, vt, *extra)


def _mla_attn(qt, ka, vat, nq, nk, tail=False):
    n_ops = 2 if nq % (2 * Q_TILE) == 0 else 1
    return _attn_call(_mla_attn_kernel, "mla_attn", MLA_HEADS, MLA_V, n_ops, qt, ka, vat, nq, nk, tail)


def _diff_attn_kernel(n_chunks, tk, tq, qt_ref, k_ref, vt_ref, lam_ref, g_ref, o_ref, s_scr):
    row = lax.broadcasted_iota(jnp.int32, (LANE, 1), 0)

    def q_of(t):
        qt = qt_ref[0, :, _q_cols(t, tq)]
        zero = jnp.zeros_like(qt)
        return [jnp.where(row < DIFF_QK, qt, zero), jnp.where(row >= DIFF_QK, qt, zero)]

    def finish(t, outs):
        o = outs[0] - lam_ref[0:1, 0:1] * outs[1]
        o = o * lax.rsqrt(jnp.mean(o * o, axis=0, keepdims=True) + RMS_EPS) * g_ref[...]
        o_ref[0, :, _q_cols(t, tq)] = o * (1.0 - LAMBDA_INIT_0)

    _attn_sweep(n_chunks, tk, tq, qt_ref.shape[2] // tq, q_of, k_ref, vt_ref, s_scr, finish)


def _diff_attn(dqt, dk, dvt, lam, subln_col, nq, nk, tail=False):
    return _attn_call(_diff_attn_kernel, "diff_attn", DIFF_HEADS, DIFF_V, 2, dqt, dk, dvt, nq, nk, tail,
                      extra=(lam, subln_col))


def _residual_ln(xin, y, gate_vec, ln_g, ln_b):
    r = DEEPNORM_ALPHA * xin + gate_vec * y
    mu = jnp.mean(r, axis=-1, keepdims=True)
    rc = r - mu
    var = jnp.mean(rc * rc, axis=-1, keepdims=True)
    return rc * lax.rsqrt(var + LN_EPS) * ln_g + ln_b


def _even_merge_kernel(nx, x_ref, c_ref, mod_ref, oax_ref, oac_ref, obx_ref, obc_ref, gate_ref,
                       wout_ref, lng_ref, lnb_ref, mod1_ref, w1_ref, out_ref, u_ref, gate1_ref):
    i = pl.program_id(1)
    is_ctx = i >= nx
    xin = jnp.where(is_ctx, c_ref[0], x_ref[0])
    oa = jnp.where(is_ctx, oac_ref[0], oax_ref[0]).T
    ob = jnp.where(is_ctx, obc_ref[0], obx_ref[0]).T
    o = jnp.concatenate([oa, ob], axis=1)
    g = gate_ref[0]
    y = _bf16_dot(o * (g * _sigmoid(g)), wout_ref[...])
    xc = _residual_ln(xin, y, mod_ref[0, 0][2:3], lng_ref[...], lnb_ref[...])
    out_ref[0] = xc

    mod1 = mod1_ref[0, 0]
    z = _bf16_dot(xc * (1.0 + mod1[1:2]) + mod1[0:1], w1_ref[...])
    w = u_ref.shape[2]
    u_ref[0] = z[:, :w]
    gate1_ref[0] = z[:, w:]


def _even_merge(x, ctx, modsel, oax, oac, obx, obc, gate, wout, ln_g, ln_b, modsel1, w_in1):
    b, n, d = x.shape
    c = ctx.shape[1]
    t = TOKEN_TILE
    nx, nc = n // t, c // t
    wa, wb = oax.shape[1], obx.shape[1]
    w = w_in1.shape[1] // 2
    full = lambda a: pl.BlockSpec(a.shape, lambda bi, i: (0,) * a.ndim)
    xi = lambda bi, i: (bi, jnp.minimum(i, nx - 1), 0)
    ci = lambda bi, i: (bi, jnp.maximum(i - nx, 0), 0)
    xit = lambda bi, i: (bi, 0, jnp.minimum(i, nx - 1))
    cit = lambda bi, i: (bi, 0, jnp.maximum(i - nx, 0))
    mod_spec = pl.BlockSpec((1, 1, 3, d), lambda bi, i: (bi, jnp.where(i >= nx, 1, 0), 0, 0))
    tok = lambda width: pl.BlockSpec((1, t, width), lambda bi, i: (bi, i, 0))
    return pl.pallas_call(
        functools.partial(_even_merge_kernel, nx),
        grid=(b, nx + nc),
        in_specs=[
            pl.BlockSpec((1, t, d), xi),
            pl.BlockSpec((1, t, d), ci),
            mod_spec,
            pl.BlockSpec((1, wa, t), xit), pl.BlockSpec((1, wa, t), cit),
            pl.BlockSpec((1, wb, t), xit), pl.BlockSpec((1, wb, t), cit),
            tok(wa + wb),
            full(wout), full(ln_g), full(ln_b),
            mod_spec, full(w_in1),
        ],
        out_specs=[tok(d), tok(w), tok(w)],
        out_shape=[jax.ShapeDtypeStruct((b, n + c, d), jnp.float32),
                   jax.ShapeDtypeStruct((b, n + c, w), jnp.float32),
                   jax.ShapeDtypeStruct((b, n + c, w), jnp.float32)],
        compiler_params=pltpu.CompilerParams(
            dimension_semantics=("parallel", "parallel"), vmem_limit_bytes=VMEM_LIMIT,
            allow_input_fusion=[False] * 8 + [True] + [False] * 3 + [True]),
        name="even_merge",
    )(x, ctx, modsel, oax, oac, obx, obc, gate, wout, ln_g, ln_b, modsel1, w_in1)


def _conv_tile(u, prev, nxt, cw, cb):
    t = u.shape[0]
    row = lax.broadcasted_iota(jnp.int32, (SUBLANE, 1), 0)
    r1, r2, rp = pltpu.roll(u, 1, 0), pltpu.roll(u, 2, 0), pltpu.roll(u, t - 1, 0)
    m1 = jnp.concatenate([jnp.where(row == 0, prev[7:8], r1[:SUBLANE]), r1[SUBLANE:]], 0)
    m2 = jnp.concatenate([jnp.where(row == 0, prev[6:7], jnp.where(row == 1, prev[7:8], r2[:SUBLANE])),
                          r2[SUBLANE:]], 0)
    p1 = jnp.concatenate([rp[:t - SUBLANE], jnp.where(row == SUBLANE - 1, nxt[0:1], rp[t - SUBLANE:])], 0)
    return cw[0:1] * m2 + cw[1:2] * m1 + cw[2:3] * u + cw[3:4] * p1 + cb


def _lru_coeffs(uc, wa_ref, ba, wx_ref, bx, lam, a_ref, b_ref):
    neg_log_a1 = LRU_C * jnp.log1p(jnp.exp(-lam))
    for k in range(LRU_BLOCKS):
        sl = slice(k * LANE, (k + 1) * LANE)
        ub = uc[:, sl]
        r = _sigmoid(_bf16_dot(ub, wa_ref[k]) + ba[:, sl])
        ig = _sigmoid(_bf16_dot(ub, wx_ref[k]) + bx[:, sl])
        a = jnp.exp2(r * (-LOG2E * neg_log_a1[:, sl]))
        a_ref[:, sl] = a
        y = jnp.tanh(r * neg_log_a1[:, sl]) * (a * a + 1.0)
        root = jnp.where(y > 0.0, y * lax.rsqrt(y), 0.0)
        b_ref[:, sl] = root * (ig * ub)


def _scan_rows(a_ref, b_ref, h_out_ref, h0, t, reverse):
    def step(s, h):
        r = (t - 1 - s) if reverse else s
        h = a_ref[pl.ds(r, 1), :] * h + b_ref[pl.ds(r, 1), :]
        h_out_ref[pl.ds(r, 1), :] = h
        return h

    return lax.fori_loop(0, t, step, h0, unroll=8)


def _lru_ctx_kernel(u_ref, cw_ref, cb_ref, wa_ref, ba_ref, wx_ref, bx_ref, lam_ref, h_ref,
                    a_s, b_s, hs):
    u = u_ref[0]
    zero8 = jnp.zeros((SUBLANE, u.shape[1]), jnp.float32)
    uc = _conv_tile(u, zero8, zero8, cw_ref[...], cb_ref[...])
    t = u.shape[0]
    for d in range(2):
        _lru_coeffs(uc, wa_ref.at[d], ba_ref[d], wx_ref.at[d], bx_ref[d], lam_ref[d], a_s, b_s)
        h = _scan_rows(a_s, b_s, hs, jnp.zeros((1, u.shape[1]), jnp.float32), t, reverse=(d == 1))
        h_ref[0, d:d + 1, :] = h


def _lru_ctx(u_all, n, c, cw, cb, wa, ba, wx, bx, lam):
    b, _, w = u_all.shape
    full = lambda a: pl.BlockSpec(a.shape, lambda bi: (0,) * a.ndim)
    return pl.pallas_call(
        _lru_ctx_kernel,
        grid=(b,),
        in_specs=[pl.BlockSpec((1, c, w), lambda bi: (bi, n // c, 0)),
                  full(cw), full(cb), full(wa), full(ba), full(wx), full(bx), full(lam)],
        out_specs=pl.BlockSpec((1, 2, w), lambda bi: (bi, 0, 0)),
        out_shape=jax.ShapeDtypeStruct((b, 2, w), jnp.float32),
        scratch_shapes=[pltpu.VMEM((c, w), jnp.float32)] * 3,
        compiler_params=_cparams("parallel"),
        name="lru_ctx",
    )(u_all, cw, cb, wa, ba, wx, bx, lam)


def _lru_fwd_kernel(n_tiles, u_ref, up_ref, un_ref, h0_ref, cw_ref, cb_ref, wa_ref, ba_ref, wx_ref, bx_ref,
                    lam_ref, hf_ref, uc_ref, a_s, b_s, carry):
    tile = pl.program_id(1)

    @pl.when(tile == 0)
    def _():
        carry[...] = h0_ref[0, 0:1, :]

    u = u_ref[0]
    prev = jnp.where(tile == 0, 0.0, up_ref[0])
    nxt = jnp.where(tile == n_tiles - 1, 0.0, un_ref[0])
    uc = _conv_tile(u, prev, nxt, cw_ref[...], cb_ref[...])
    uc_ref[0] = uc
    _lru_coeffs(uc, wa_ref.at[0], ba_ref[0], wx_ref.at[0], bx_ref[0], lam_ref[0], a_s, b_s)
    carry[...] = _scan_rows(a_s, b_s, hf_ref.at[0], carry[...], u.shape[0], reverse=False)


def _lru_bwd_kernel(uc_ref, h0_ref, wa_ref, ba_ref, wx_ref, bx_ref, lam_ref, hf_ref, gate_ref, x_ref, mod_ref,
                    wout_ref, lng_ref, lnb_ref, out_ref, a_s, b_s, hs, carry):
    @pl.when(pl.program_id(1) == 0)
    def _():
        carry[...] = h0_ref[0, 1:2, :]

    uc = uc_ref[0]
    _lru_coeffs(uc, wa_ref.at[1], ba_ref[1], wx_ref.at[1], bx_ref[1], lam_ref[1], a_s, b_s)
    carry[...] = _scan_rows(a_s, b_s, hs, carry[...], uc.shape[0], reverse=True)
    g = gate_ref[0]
    hx = hf_ref[0] + hs[...]
    y = _bf16_dot(hx * (g * _sigmoid(g)), wout_ref[...])
    out_ref[0] = _residual_ln(x_ref[0], y, mod_ref[0, 0][2:3], lng_ref[...], lnb_ref[...])


def _lru_fwd(u_all, n, h0, cw, cb, wa, ba, wx, bx, lam):
    b, _, w = u_all.shape
    t = min(SCAN_TILE, n)
    n_tiles = n // t
    r = t // SUBLANE
    full = lambda a: pl.BlockSpec(a.shape, lambda bi, s: (0,) * a.ndim)
    tile_spec = pl.BlockSpec((1, t, w), lambda bi, s: (bi, s, 0))
    return pl.pallas_call(
        functools.partial(_lru_fwd_kernel, n_tiles),
        grid=(b, n_tiles),
        in_specs=[
            tile_spec,
            pl.BlockSpec((1, SUBLANE, w), lambda bi, s: (bi, jnp.maximum(s * r - 1, 0), 0)),
            pl.BlockSpec((1, SUBLANE, w), lambda bi, s: (bi, (s + 1) * r, 0)),
            pl.BlockSpec((1, 2, w), lambda bi, s: (bi, 0, 0)),
            full(cw), full(cb), full(wa), full(ba), full(wx), full(bx), full(lam),
        ],
        out_specs=[tile_spec, tile_spec],
        out_shape=[jax.ShapeDtypeStruct((b, n, w), jnp.float32)] * 2,
        scratch_shapes=[pltpu.VMEM((t, w), jnp.float32), pltpu.VMEM((t, w), jnp.float32),
                        pltpu.VMEM((1, w), jnp.float32)],
        compiler_params=_cparams("parallel", "arbitrary"),
        name="lru_fwd",
    )(u_all, u_all, u_all, h0, cw, cb, wa, ba, wx, bx, lam)


def _lru_bwd_out(uc, h0, wa, ba, wx, bx, lam, hf, gate, xc, modsel, wout, ln_g, ln_b):
    b, n, w = uc.shape
    d = xc.shape[2]
    t = min(SCAN_TILE, n)
    n_tiles = n // t
    full = lambda a: pl.BlockSpec(a.shape, lambda bi, s: (0,) * a.ndim)
    tile_spec = lambda width: pl.BlockSpec((1, t, width), lambda bi, s: (bi, n_tiles - 1 - s, 0))
    return pl.pallas_call(
        _lru_bwd_kernel,
        grid=(b, n_tiles),
        in_specs=[
            tile_spec(w),
            pl.BlockSpec((1, 2, w), lambda bi, s: (bi, 0, 0)),
            full(wa), full(ba), full(wx), full(bx), full(lam),
            tile_spec(w), tile_spec(w), tile_spec(d),
            pl.BlockSpec((1, 1, 3, d), lambda bi, s: (bi, 0, 0, 0)),
            full(wout), full(ln_g), full(ln_b),
        ],
        out_specs=tile_spec(d),
        out_shape=jax.ShapeDtypeStruct((b, n, d), jnp.float32),
        scratch_shapes=[pltpu.VMEM((t, w), jnp.float32)] * 3 + [pltpu.VMEM((1, w), jnp.float32)],
        compiler_params=_cparams("parallel", "arbitrary"),
        name="lru_bwd_out",
    )(uc, h0, wa, ba, wx, bx, lam, hf, gate, xc, modsel, wout, ln_g, ln_b)


def _rope_tables(n, c):
    t = np.arange(n)
    rows = (t // GRID_W).astype(np.float32)
    cols = (t % GRID_W).astype(np.float32)

    def ang(rot_dim):
        n_freq = rot_dim // 4
        freqs = np.float32(ROPE_THETA) ** (-np.arange(n_freq, dtype=np.float32) / np.float32(n_freq))
        a = np.concatenate([rows[:, None] * freqs, cols[:, None] * freqs], -1)
        return np.cos(a.astype(np.float64)), np.sin(a.astype(np.float64))

    (cos_a, sin_a), (cos_b, sin_b) = ang(MLA_ROPE), ang(DIFF_QK)
    one = np.ones((n, MLA_NOPE))
    zero = np.zeros((n, MLA_NOPE))
    pad1 = np.ones((n, LANE - MLA_QK))
    pad0 = np.zeros((n, LANE - MLA_QK))
    ca = np.concatenate([one, cos_a, cos_a, pad1], 1)
    sa = np.concatenate([zero, -sin_a, sin_a, pad0], 1)
    cb = np.concatenate([cos_b, cos_b] * 2, 1)
    sb = np.concatenate([-sin_b, sin_b] * 2, 1)
    ident = lambda tab, v: jnp.asarray(np.concatenate([tab, np.full((c, LANE), v)], 0), jnp.float32)
    return ident(ca, 1.0), ident(sa, 0.0), ident(cb, 1.0), ident(sb, 0.0)


def _even_weights(w_in, w_uq, w_ukv):
    d = w_in.shape[0]
    bf = jnp.bfloat16
    o = [0, 384, 640, 672, 1184, 1696, 2208, 3232]
    cq, ckv, kr, dq, dk, dv, gate = (w_in[:, o[i]:o[i + 1]] for i in range(7))
    kr_group = jnp.concatenate([jnp.zeros((d, MLA_NOPE), w_in.dtype), kr,
                                jnp.zeros((d, LANE - MLA_QK), w_in.dtype)], 1)
    w1 = jnp.concatenate([cq, ckv, kr_group, dq, dk, dv, gate], 1).astype(bf)
    wuq = jnp.pad(w_uq.reshape(MLA_Q_LORA, MLA_HEADS, MLA_QK),
                  ((0, 0), (0, 0), (0, LANE - MLA_QK))).reshape(MLA_Q_LORA, MLA_HEADS * LANE).astype(bf)
    ukv = w_ukv.reshape(MLA_KV_LORA, MLA_HEADS, MLA_NOPE + MLA_V)
    wuk = jnp.pad(ukv[..., :MLA_NOPE], ((0, 0), (0, 0), (0, LANE - MLA_NOPE))).reshape(MLA_KV_LORA, MLA_HEADS * LANE)
    wuv = ukv[..., MLA_NOPE:].reshape(MLA_KV_LORA, MLA_HEADS * MLA_V)
    wkv = jnp.concatenate([wuk, wuv], 1).astype(bf)
    return w1, wuq, wkv


def _mod_select(mods_l, b):
    d = mods_l.shape[1] // 3
    mx = mods_l[:b].reshape(b, 3, d)
    mc = jnp.broadcast_to(mods_l[b].reshape(1, 3, d), (b, 3, d))
    return jnp.stack([mx, mc], axis=1)


def kernel(x, c, ctx, c_ctx, ada_w, ada_b, post_ln_g, post_ln_b, e_w_in, e_q_norm_g, e_w_uq, e_kv_norm_g, e_w_ukv, e_lam_q1, e_lam_k1, e_lam_q2, e_lam_k2, e_subln_g, e_w_out, o_w_in, o_conv_w, o_conv_b, o_gate_a_w, o_gate_a_b, o_gate_x_w, o_gate_x_b, o_lru_lambda, o_w_out):
    b, n, d = x.shape
    cl = ctx.shape[1]
    assert ada_w.shape[0] == DEPTH and b < SUBLANE
    assert n % Q_TILE == 0 and cl % TOKEN_TILE == 0 and n % cl == 0
    bf = jnp.bfloat16

    cond = jnp.concatenate([c, c_ctx[None], jnp.zeros((SUBLANE - b - 1, d), c.dtype)], 0)
    lamv = jnp.concatenate([e_lam_q1, e_lam_k1, e_lam_q2, e_lam_k2], 0)
    mods, lam = _adaln(cond, ada_w, ada_b, lamv)
    mod0, mod1 = _mod_select(mods[0], b), _mod_select(mods[1], b)

    w1, wuq, wkv = _even_weights(e_w_in[0], e_w_uq[0], e_w_ukv[0])
    tabs = _rope_tables(n, cl)
    qt, ka, vat, dqt, dk, dvt, gate = _even_proj(
        x, ctx, mod0, w1, e_q_norm_g[0][None], wuq, e_kv_norm_g[0][None], wkv, tabs)
    subln_col = e_subln_g[0][:, None]
    oax = _mla_attn(qt, ka, vat, n, n + cl)
    obx = _diff_attn(dqt, dk, dvt, lam, subln_col, n, n + cl)
    oac = _mla_attn(qt, ka, vat, cl, cl, tail=True)
    obc = _diff_attn(dqt, dk, dvt, lam, subln_col, cl, cl, tail=True)
    xc, u_all, gate1 = _even_merge(x, ctx, mod0, oax, oac, obx, obc, gate, e_w_out[0].astype(bf),
                                   post_ln_g[0][None], post_ln_b[0][None], mod1, o_w_in[0].astype(bf))

    cw, cb = o_conv_w[0], o_conv_b[0][None]
    wa, wx = o_gate_a_w[0].astype(bf), o_gate_x_w[0].astype(bf)
    w = u_all.shape[2]
    ba = o_gate_a_b[0].reshape(2, 1, w)
    bx = o_gate_x_b[0].reshape(2, 1, w)
    lru_lam = o_lru_lambda[0].reshape(2, 1, w)
    h0 = _lru_ctx(u_all, n, cl, cw, cb, wa, ba, wx, bx, lru_lam)
    hf, uc = _lru_fwd(u_all, n, h0, cw, cb, wa, ba, wx, bx, lru_lam)
    return _lru_bwd_out(uc, h0, wa, ba, wx, bx, lru_lam, hf, gate1, xc, mod1, o_w_out[0].astype(bf),
                        post_ln_g[1][None], post_ln_b[1][None])
```
